```python
import math
import jax
import jax.numpy as jnp
from jax import lax
import numpy as np

D_MODEL = 1024
BATCH = 2
SEQ = 8192
DEPTH = 2
DEC_BATCH = 32
DEC_SEQ = 4
PAST_LEN = 8192
PAGE_SIZE = 128

N_A_LAYERS = DEPTH // 2
N_B_LAYERS = DEPTH - N_A_LAYERS
D_FF = 4 * D_MODEL
D_RNN = D_MODEL
N_RNN_BLOCKS = 4
RNN_BLOCK = D_RNN // N_RNN_BLOCKS
CONV_W = 4
LRU_C = 8.0
N_HEADS = 16
HEAD_DIM = D_MODEL // N_HEADS
N_KV_GROUPS = 4
GROUP_SIZE = N_HEADS // N_KV_GROUPS
L_CMP = 32
CMP_STRIDE = 16
CMP_HIDDEN = 128
SEL_BLOCK = 64
N_SELECT = 16
WINDOW = 512
Q_BLOCK = 128
NUM_BUCKETS = 32
REL_MAX_DIST = 1024
N_KV_SETS = 6
EPS = 1e-6
NEG_INF = -1e30

kernel_name = 'hybrid_rglru_nsa_yoco_step'


def rms_norm(x, g):
    xf = x.astype(jnp.float32)
    y = xf * lax.rsqrt(jnp.mean(xf * xf, axis=-1, keepdims=True) + EPS)
    return (y * g.astype(jnp.float32)).astype(x.dtype)


def rel_bucket(dist):
    n = jnp.maximum(dist, 0)
    max_exact = NUM_BUCKETS // 2
    nf = jnp.maximum(n, 1).astype(jnp.float32)
    large = max_exact + (jnp.log(nf / max_exact) / math.log(REL_MAX_DIST / max_exact)
                         * (NUM_BUCKETS - max_exact)).astype(jnp.int32)
    return jnp.where(n < max_exact, n, jnp.minimum(large, NUM_BUCKETS - 1))


def rel_bias(dist, table_gr):
    return table_gr[rel_bucket(dist)].astype(jnp.float32).transpose(2, 3, 0, 1)


def masked_softmax(s, mask):
    s = jnp.where(mask, s.astype(jnp.float32), NEG_INF)
    p = jax.nn.softmax(s, axis=-1)
    return jnp.where(mask, p, 0.0)


def linear_scan_op(c1, c2):
    a1, b1 = c1
    a2, b2 = c2
    return a1 * a2, a2 * b1 + b2


def rglru_mixer(u, h0, conv0, w_in, conv_w, conv_b, w_a, b_a, w_x, b_x, lam, w_out):
    n, t, _ = u.shape
    proj = u @ w_in
    gate = jax.nn.gelu(proj[..., :D_RNN])
    xb = proj[..., D_RNN:]
    xcat = jnp.concatenate([conv0.astype(xb.dtype), xb], axis=1)
    xc = conv_b + sum(xcat[:, k:k + t] * conv_w[k] for k in range(CONV_W))
    new_conv = xcat[:, t:]
    xblk = xc.reshape(n, t, N_RNN_BLOCKS, RNN_BLOCK)
    r = jax.nn.sigmoid(jnp.einsum('ntbi,bij->ntbj', xblk, w_a).reshape(n, t, D_RNN) + b_a).astype(jnp.float32)
    i = jax.nn.sigmoid(jnp.einsum('ntbi,bij->ntbj', xblk, w_x).reshape(n, t, D_RNN) + b_x).astype(jnp.float32)
    log_a = -LRU_C * r * jax.nn.softplus(-lam.astype(jnp.float32))
    a = jnp.exp(log_a)
    b = jnp.sqrt(-jnp.expm1(2.0 * log_a)) * i * xc.astype(jnp.float32)
    b = b.at[:, 0].add(a[:, 0] * h0.astype(jnp.float32))
    _, h = lax.associative_scan(linear_scan_op, (a, b), axis=1)
    y = (h.astype(u.dtype) * gate) @ w_out
    return y, h[:, -1].astype(h0.dtype), new_conv.astype(conv0.dtype)


def sq_relu_mlp(u, w_up, w_down):
    h = jax.nn.relu(u @ w_up)
    return (h * h) @ w_down


def compress_blocks(kv, pe, w1, b1, w2):
    n, s = kv.shape[:2]
    nh = s // CMP_STRIDE
    halves = kv[:, :nh * CMP_STRIDE].reshape(n, nh, CMP_STRIDE, N_KV_GROUPS, HEAD_DIM)
    w1r = w1.reshape(L_CMP, HEAD_DIM, CMP_HIDDEN)
    first = jnp.einsum('nhtgd,tdc->nhgc', halves + pe[:CMP_STRIDE, None, :], w1r[:CMP_STRIDE])
    second = jnp.einsum('nhtgd,tdc->nhgc', halves + pe[CMP_STRIDE:, None, :], w1r[CMP_STRIDE:])
    hid = jax.nn.gelu(first[:, :-1] + second[:, 1:] + b1)
    return hid @ w2


def sel_map(n_cmp, n_sel):
    ratio = SEL_BLOCK // CMP_STRIDE
    i = jnp.arange(n_cmp)
    return (jax.nn.one_hot(i // ratio, n_sel, dtype=jnp.float32)
            + jax.nn.one_hot((i + 1) // ratio, n_sel, dtype=jnp.float32))


def build_shared_kv(x, kv_past, win_past, w):
    n, t, _ = x.shape
    s_in = rms_norm(x, w['kv_norm'])
    rows = (s_in @ w['w_kv']).reshape(n, t, N_KV_SETS, N_KV_GROUPS, HEAD_DIM)
    kv_rows, win_rows = rows[:, :, :4], rows[:, :, 4:]
    if kv_past is None:
        full, win_all, pos0, wpos0 = kv_rows, win_rows, 0, 0
        win_new = win_rows[:, -min(WINDOW, t):]
    else:
        full = jnp.concatenate([kv_past.astype(rows.dtype), kv_rows], axis=1)
        win_all = jnp.concatenate([win_past.astype(rows.dtype), win_rows], axis=1)
        pos0 = kv_past.shape[1]
        wpos0 = pos0 - win_past.shape[1]
        win_new = win_all[:, -win_past.shape[1]:]
    kc = compress_blocks(full[:, :, 0], w['cmp_pe'][0], w['cmp_w1'][0], w['cmp_b1'][0], w['cmp_w2'][0])
    vc = compress_blocks(full[:, :, 1], w['cmp_pe'][1], w['cmp_w1'][1], w['cmp_b1'][1], w['cmp_w2'][1])
    s_len = full.shape[1]
    n_sel = -(-s_len // SEL_BLOCK)
    sel = jnp.pad(full[:, :, 2:4], ((0, 0), (0, n_sel * SEL_BLOCK - s_len), (0, 0), (0, 0), (0, 0)))
    sel = sel.reshape(n, n_sel, SEL_BLOCK, 2, N_KV_GROUPS, HEAD_DIM).transpose(3, 0, 4, 1, 2, 5)
    shared = dict(prompt=kv_past is None, pos0=pos0, wpos0=wpos0, kc=kc, vc=vc,
                  cend=jnp.arange(kc.shape[1]) * CMP_STRIDE + (L_CMP - 1),
                  ksb=sel[0], vsb=sel[1], kw=win_all[:, :, 0], vw=win_all[:, :, 1])
    return shared, kv_rows, win_new


def nsa_block(q, g, qpos, kw, vw, kwpos, kv, table_gr):
    n, tq = q.shape[:2]
    G, R = N_KV_GROUPS, GROUP_SIZE
    qg = (q * HEAD_DIM ** -0.5).reshape(n, tq, G, R, HEAD_DIM).transpose(0, 2, 3, 1, 4)
    kc, vc = kv['kc'], kv['vc']
    dist_c = qpos[:, None] - kv['cend'][None, :]
    s_c = jnp.einsum('ngrqd,ncgd->ngrqc', qg, kc) + rel_bias(dist_c, table_gr)
    p_c = masked_softmax(s_c, dist_c >= 0)
    o_c = jnp.einsum('ngrqc,ncgd->ngrqd', p_c, vc.astype(jnp.float32))
    ksb, vsb = kv['ksb'], kv['vsb']
    n_sel = ksb.shape[2]
    imp = jnp.einsum('ngrqc,cj->ngqj', p_c, sel_map(kc.shape[1], n_sel))
    blk = jnp.arange(n_sel)
    cur = (qpos // SEL_BLOCK)[:, None]
    forced = (blk == 0) | (blk == cur) | (blk == cur - 1)
    valid = blk * SEL_BLOCK <= qpos[:, None]
    imp = jnp.where(forced, 1e9, jnp.where(valid, imp, -1e9))
    _, idx = lax.top_k(imp, min(N_SELECT, n_sel))
    n_i = jnp.arange(n)[:, None, None, None]
    g_i = jnp.arange(G)[None, :, None, None]
    ks = ksb[n_i, g_i, idx].reshape(n, G, tq, -1, HEAD_DIM)
    vs = vsb[n_i, g_i, idx].reshape(n, G, tq, -1, HEAD_DIM)
    kpos = (idx[..., None] * SEL_BLOCK + jnp.arange(SEL_BLOCK)).reshape(n, G, tq, -1)
    dist_s = qpos[None, None, :, None] - kpos
    bias_s = table_gr[rel_bucket(dist_s), g_i].astype(jnp.float32).transpose(0, 1, 4, 2, 3)
    s_s = jnp.einsum('ngrqd,ngqsd->ngrqs', qg, ks) + bias_s
    p_s = masked_softmax(s_s, (dist_s >= 0)[:, :, None])
    o_s = jnp.einsum('ngrqs,ngqsd->ngrqd', p_s, vs.astype(jnp.float32))
    dist_w = qpos[:, None] - kwpos[None, :]
    mask_w = (dist_w >= 0) & (dist_w < WINDOW) & (kwpos[None, :] >= 0)
    s_w = jnp.einsum('ngrqd,nkgd->ngrqk', qg, kw) + rel_bias(dist_w, table_gr)
    p_w = masked_softmax(s_w, mask_w)
    o_w = jnp.einsum('ngrqk,nkgd->ngrqd', p_w, vw.astype(jnp.float32))
    o = jnp.stack([o_c, o_s, o_w], axis=-1).transpose(0, 3, 1, 2, 4, 5).reshape(n, tq, N_HEADS, HEAD_DIM, 3)
    out = jnp.einsum('nqhdc,nqhc->nqhd', o, g.astype(jnp.float32))
    return out.astype(q.dtype)


def nsa_mixer(u, kv, w_qg, b_g, w_o, table_gr):
    n, t, _ = u.shape
    proj = u @ w_qg
    q = proj[..., :N_HEADS * HEAD_DIM].reshape(n, t, N_HEADS, HEAD_DIM)
    g = jax.nn.sigmoid(proj[..., N_HEADS * HEAD_DIM:] + b_g).reshape(n, t, N_HEADS, 3)
    if kv['prompt']:
        nqb = t // Q_BLOCK
        kw_pad = jnp.pad(kv['kw'], ((0, 0), (WINDOW, 0), (0, 0), (0, 0)))
        vw_pad = jnp.pad(kv['vw'], ((0, 0), (WINDOW, 0), (0, 0), (0, 0)))
        qb = q.reshape(n, nqb, Q_BLOCK, N_HEADS, HEAD_DIM).swapaxes(0, 1)
        gb = g.reshape(n, nqb, Q_BLOCK, N_HEADS, 3).swapaxes(0, 1)

        def body(args):
            q_c, g_c, blk = args
            t0 = blk * Q_BLOCK
            kw_c = lax.dynamic_slice_in_dim(kw_pad, t0, WINDOW + Q_BLOCK, axis=1)
            vw_c = lax.dynamic_slice_in_dim(vw_pad, t0, WINDOW + Q_BLOCK, axis=1)
            qpos = t0 + jnp.arange(Q_BLOCK)
            kwpos = t0 - WINDOW + jnp.arange(WINDOW + Q_BLOCK)
            return nsa_block(q_c, g_c, qpos, kw_c, vw_c, kwpos, kv, table_gr)

        o = lax.map(body, (qb, gb, jnp.arange(nqb))).swapaxes(0, 1)
    else:
        qpos = kv['pos0'] + jnp.arange(t)
        kwpos = kv['wpos0'] + jnp.arange(kv['kw'].shape[1])
        o = nsa_block(q, g, qpos, kv['kw'], kv['vw'], kwpos, kv, table_gr)
    return o.reshape(n, t, N_HEADS * HEAD_DIM) @ w_o


def run_trunk(x, h0, conv0, kv_past, win_past, w):
    table_gr = w['rel_bias_table'].reshape(NUM_BUCKETS, N_KV_GROUPS, GROUP_SIZE)
    new_h, new_conv = [], []
    kv, kv_rows, win_new = None, None, None
    for layer in range(DEPTH):
        u = rms_norm(x, w['norm_pre_mix'][layer])
        if layer < N_A_LAYERS:
            a = layer
            y, h_l, c_l = rglru_mixer(u, h0[a], conv0[a], w['rg_w_in'][a], w['rg_conv_w'][a], w['rg_conv_b'][a],
                                      w['rg_w_a'][a], w['rg_b_a'][a], w['rg_w_x'][a], w['rg_b_x'][a],
                                      w['rg_lambda'][a], w['rg_w_out'][a])
            new_h.append(h_l)
            new_conv.append(c_l)
        else:
            b = layer - N_A_LAYERS
            y = nsa_mixer(u, kv, w['nsa_w_qg'][b], w['nsa_b_g'][b], w['nsa_w_o'][b], table_gr)
        x = x + rms_norm(y, w['norm_post_mix'][layer])
        m = sq_relu_mlp(rms_norm(x, w['norm_pre_mlp'][layer]), w['w_mlp_up'][layer], w['w_mlp_down'][layer])
        x = x + rms_norm(m, w['norm_post_mlp'][layer])
        if layer == N_A_LAYERS - 1:
            kv, kv_rows, win_new = build_shared_kv(x, kv_past, win_past, w)
    return x, jnp.stack(new_h), jnp.stack(new_conv), kv_rows, win_new


def setup_inputs(seed: int = 0) -> dict:
    key = jax.random.key(seed)
    ks = jax.random.split(key, 40)
    f32 = jnp.float32
    n_pages = PAST_LEN // PAGE_SIZE
    n_phys = (DEC_BATCH * n_pages * 5) // 4
    w_buf = min(WINDOW, PAST_LEN)
    qg_out = N_HEADS * HEAD_DIM + 3 * N_HEADS
    kv_out = N_KV_SETS * N_KV_GROUPS * HEAD_DIM

    def nrm(k, shape, scale=1.0):
        return jax.random.normal(k, shape, f32) * scale

    def gain(k, shape):
        return 1.0 + 0.05 * jax.random.normal(k, shape, f32)

    a0 = jax.random.uniform(ks[20], (N_A_LAYERS, D_RNN), f32, 0.9, 0.999)
    page_table = jax.random.permutation(ks[6], n_phys)[:DEC_BATCH * n_pages].reshape(DEC_BATCH, n_pages).astype(jnp.int32)
    return {
        'x_prompt': nrm(ks[0], (BATCH, SEQ, D_MODEL)),
        'x_sample': nrm(ks[1], (DEC_BATCH, DEC_SEQ, D_MODEL)),
        'state_rglru_h': nrm(ks[2], (N_A_LAYERS, DEC_BATCH, D_RNN), 0.5),
        'state_rglru_conv': nrm(ks[3], (N_A_LAYERS, DEC_BATCH, CONV_W - 1, D_RNN)),
        'cache_kv': nrm(ks[4], (n_phys, PAGE_SIZE, 4, N_KV_GROUPS, HEAD_DIM)),
        'cache_win': nrm(ks[5], (DEC_BATCH, w_buf, 2, N_KV_GROUPS, HEAD_DIM)),
        'page_table': page_table,
        'norm_pre_mix': gain(ks[7], (DEPTH, D_MODEL)),
        'norm_post_mix': gain(ks[8], (DEPTH, D_MODEL)),
        'norm_pre_mlp': gain(ks[9], (DEPTH, D_MODEL)),
        'norm_post_mlp': gain(ks[10], (DEPTH, D_MODEL)),
        'w_mlp_up': nrm(ks[11], (DEPTH, D_MODEL, D_FF), D_MODEL ** -0.5),
        'w_mlp_down': nrm(ks[12], (DEPTH, D_FF, D_MODEL), D_FF ** -0.5),
        'rg_w_in': nrm(ks[13], (N_A_LAYERS, D_MODEL, 2 * D_RNN), D_MODEL ** -0.5),
        'rg_conv_w': nrm(ks[14], (N_A_LAYERS, CONV_W, D_RNN), CONV_W ** -0.5),
        'rg_conv_b': nrm(ks[15], (N_A_LAYERS, D_RNN), 0.05),
        'rg_w_a': nrm(ks[16], (N_A_LAYERS, N_RNN_BLOCKS, RNN_BLOCK, RNN_BLOCK), RNN_BLOCK ** -0.5),
        'rg_b_a': nrm(ks[17], (N_A_LAYERS, D_RNN), 0.1),
        'rg_w_x': nrm(ks[18], (N_A_LAYERS, N_RNN_BLOCKS, RNN_BLOCK, RNN_BLOCK), RNN_BLOCK ** -0.5),
        'rg_b_x': nrm(ks[19], (N_A_LAYERS, D_RNN), 0.1),
        'rg_lambda': jnp.log(a0) - jnp.log1p(-a0),
        'rg_w_out': nrm(ks[21], (N_A_LAYERS, D_RNN, D_MODEL), D_RNN ** -0.5),
        'kv_norm': gain(ks[22], (D_MODEL,)),
        'w_kv': nrm(ks[23], (D_MODEL, kv_out), D_MODEL ** -0.5),
        'cmp_pe': nrm(ks[24], (2, L_CMP, HEAD_DIM), 0.1),
        'cmp_w1': nrm(ks[25], (2, L_CMP * HEAD_DIM, CMP_HIDDEN), (L_CMP * HEAD_DIM) ** -0.5),
        'cmp_b1': nrm(ks[26], (2, CMP_HIDDEN), 0.05),
        'cmp_w2': nrm(ks[27], (2, CMP_HIDDEN, HEAD_DIM), CMP_HIDDEN ** -0.5),
        'nsa_w_qg': nrm(ks[28], (N_B_LAYERS, D_MODEL, qg_out), D_MODEL ** -0.5),
        'nsa_b_g': nrm(ks[29], (N_B_LAYERS, 3 * N_HEADS), 0.1),
        'nsa_w_o': nrm(ks[30], (N_B_LAYERS, N_HEADS * HEAD_DIM, D_MODEL), (N_HEADS * HEAD_DIM) ** -0.5),
        'rel_bias_table': nrm(ks[31], (NUM_BUCKETS, N_HEADS), 0.3),
    }


def reference(x_prompt, x_sample, state_rglru_h, state_rglru_conv, cache_kv, cache_win, page_table,
              norm_pre_mix, norm_post_mix, norm_pre_mlp, norm_post_mlp, w_mlp_up, w_mlp_down,
              rg_w_in, rg_conv_w, rg_conv_b, rg_w_a, rg_b_a, rg_w_x, rg_b_x, rg_lambda, rg_w_out,
              kv_norm, w_kv, cmp_pe, cmp_w1, cmp_b1, cmp_w2, nsa_w_qg, nsa_b_g, nsa_w_o, rel_bias_table):
    w = dict(norm_pre_mix=norm_pre_mix, norm_post_mix=norm_post_mix, norm_pre_mlp=norm_pre_mlp,
             norm_post_mlp=norm_post_mlp, w_mlp_up=w_mlp_up, w_mlp_down=w_mlp_down,
             rg_w_in=rg_w_in, rg_conv_w=rg_conv_w, rg_conv_b=rg_conv_b, rg_w_a=rg_w_a, rg_b_a=rg_b_a,
             rg_w_x=rg_w_x, rg_b_x=rg_b_x, rg_lambda=rg_lambda, rg_w_out=rg_w_out,
             kv_norm=kv_norm, w_kv=w_kv, cmp_pe=cmp_pe, cmp_w1=cmp_w1, cmp_b1=cmp_b1, cmp_w2=cmp_w2,
             nsa_w_qg=nsa_w_qg, nsa_b_g=nsa_b_g, nsa_w_o=nsa_w_o, rel_bias_table=rel_bias_table)
    nb = x_prompt.shape[0]
    z_h = jnp.zeros((N_A_LAYERS, nb, D_RNN), x_prompt.dtype)
    z_c = jnp.zeros((N_A_LAYERS, nb, CONV_W - 1, D_RNN), x_prompt.dtype)
    y_p, p_h, p_conv, p_kv, p_win = run_trunk(x_prompt, z_h, z_c, None, None, w)
    past = cache_kv[page_table]
    past = past.reshape(past.shape[0], past.shape[1] * past.shape[2], *past.shape[3:])
    y_s, s_h, s_conv, s_kv, s_win = run_trunk(x_sample, state_rglru_h, state_rglru_conv, past, cache_win, w)
    return (y_p, y_s, p_h, p_conv, p_kv, p_win, s_h, s_conv, s_kv, s_win)
```

```python
import functools
import math

import jax
import jax.numpy as jnp
from jax import lax
from jax.experimental import pallas as pl
from jax.experimental.pallas import tpu as pltpu

F32 = jnp.float32
BF16 = jnp.bfloat16

N_HEADS = 16
N_KV_GROUPS = 4
GROUP_SIZE = N_HEADS // N_KV_GROUPS
HEAD_DIM = 64
N_RNN_BLOCKS = 4
CONV_W = 4
LRU_C = 8.0
L_CMP = 32
CMP_STRIDE = 16
SEL_BLOCK = 64
N_SELECT = 16
WINDOW = 512
Q_BLOCK = 128
NUM_BUCKETS = 32
REL_MAX_DIST = 1024
EPS = 1e-6
NEG_INF = -1e30

SUBLANES = 8
LANES = 128
VMEM_LIMIT_BYTES = 56 * 1024 * 1024


def _cparams(*sem):
    return pltpu.CompilerParams(dimension_semantics=sem, vmem_limit_bytes=VMEM_LIMIT_BYTES)


def _const_spec(shape):
    nd = len(shape)
    return pl.BlockSpec(shape, lambda *_: (0,) * nd, pipeline_mode=pl.Buffered(1))


def _rms(x, g):
    return x * lax.rsqrt(jnp.mean(x * x, axis=-1, keepdims=True) + EPS) * g


def _dot(a, b):
    return jnp.dot(a, b, preferred_element_type=F32)


def _rglru_kernel(x_ref, h0_ref, c0_ref, gpre_ref, win_ref, cw_ref, cb_ref, wax_ref, bax_ref, lam_ref,
                  wout_ref, gpost_ref, x1_ref, hlast_ref, clast_ref, xcat_sc, hprev_sc, *, stride, tm, pad):
    d = x_ref.shape[-1]
    tail = (CONV_W - 1) * stride
    j = pl.program_id(1)

    @pl.when(j == 0)
    def _():
        xcat_sc[pad - tail:pad, :] = c0_ref[0]
        hprev_sc[...] = jnp.zeros_like(hprev_sc)
        hprev_sc[tm - stride:tm, :] = h0_ref[0]

    x = x_ref[0]
    u = _rms(x, gpre_ref[...]).astype(BF16)
    proj = _dot(u, win_ref[...])
    gate = jax.nn.gelu(proj[:, :d])
    xcat_sc[pad:pad + tm, :] = proj[:, d:]
    xc = cb_ref[...] + cw_ref[CONV_W - 1:CONV_W, :] * xcat_sc[pad:pad + tm, :]
    for lag in range(1, CONV_W):
        xc = xc + cw_ref[CONV_W - 1 - lag:CONV_W - lag, :] * xcat_sc[pad - lag * stride:pad - lag * stride + tm, :]
    xcat_sc[pad - tail:pad, :] = xcat_sc[pad + tm - tail:pad + tm, :]
    clast_ref[0] = xcat_sc[pad - tail:pad, :]

    xcb = xc.astype(BF16)
    bw = d // N_RNN_BLOCKS
    ra, rx = [], []
    for blk in range(N_RNN_BLOCKS):
        pre = _dot(xcb[:, blk * bw:(blk + 1) * bw], wax_ref[blk])
        ra.append(pre[:, :bw])
        rx.append(pre[:, bw:])
    r = jax.nn.sigmoid(jnp.concatenate(ra, axis=1) + bax_ref[0:1, :])
    i = jax.nn.sigmoid(jnp.concatenate(rx, axis=1) + bax_ref[1:2, :])
    log_a = r * (-LRU_C * jax.nn.softplus(-lam_ref[...]))
    a = jnp.exp(log_a)
    b = jnp.sqrt(-jnp.tanh(log_a) * (a * a + 1.0)) * i * xc

    row = lax.broadcasted_iota(jnp.int32, (tm, d), 0)
    b = b + jnp.where(row < stride, a * pltpu.roll(hprev_sc[...], stride, 0), 0.0)
    s = stride
    while s < tm:
        keep = row >= s
        b = a * jnp.where(keep, pltpu.roll(b, s, 0), 0.0) + b
        a = a * jnp.where(keep, pltpu.roll(a, s, 0), 1.0)
        s *= 2
    h = b
    hprev_sc[...] = h
    hlast_ref[0] = h[tm - stride:tm, :]

    y = _dot((h * gate).astype(BF16), wout_ref[...])
    x1_ref[0] = x + _rms(y, gpost_ref[...])


def _rglru_layer(x, h0, c0, w, *, stride, tm):
    n, rows, d = x.shape
    tail = (CONV_W - 1) * stride
    pad = -(-tail // SUBLANES) * SUBLANES
    kern = functools.partial(_rglru_kernel, stride=stride, tm=tm, pad=pad)
    seq = lambda shape: pl.BlockSpec(shape, lambda b, j: (b, j, 0))
    per_n = lambda shape: pl.BlockSpec(shape, lambda b, j: (b, 0, 0))
    return pl.pallas_call(
        kern,
        grid=(n, rows // tm),
        in_specs=[seq((1, tm, d)), per_n((1, stride, d)), per_n((1, tail, d)),
                  _const_spec((1, d)), _const_spec(w['w_in'].shape), _const_spec((CONV_W, d)), _const_spec((1, d)),
                  _const_spec(w['w_ax'].shape), _const_spec((2, d)), _const_spec((1, d)),
                  _const_spec(w['w_out'].shape), _const_spec((1, d))],
        out_specs=[seq((1, tm, d)), per_n((1, stride, d)), per_n((1, tail, d))],
        out_shape=[jax.ShapeDtypeStruct((n, rows, d), F32), jax.ShapeDtypeStruct((n, stride, d), F32),
                   jax.ShapeDtypeStruct((n, tail, d), F32)],
        scratch_shapes=[pltpu.VMEM((pad + tm, d), F32), pltpu.VMEM((tm, d), F32)],
        compiler_params=_cparams("arbitrary", "arbitrary"),
        name="rglru_layer",
    )(x, h0, c0, w['g_pre'], w['w_in'], w['conv_w'], w['conv_b'], w['w_ax'], w['b_ax'], w['lam'], w['w_out'], w['g_post'])


MLP_HIDDEN_CHUNK = 1024


def _mlp_body(x, g1_ref, wup_ref, wdn_ref, g2_ref):
    u = _rms(x, g1_ref[...]).astype(BF16)
    f = wup_ref.shape[1]
    acc = jnp.zeros(x.shape, F32)
    for c in range(f // MLP_HIDDEN_CHUNK):
        cols = slice(c * MLP_HIDDEN_CHUNK, (c + 1) * MLP_HIDDEN_CHUNK)
        hid = jnp.maximum(_dot(u, wup_ref[:, cols]), 0.0)
        acc = acc + _dot((hid * hid).astype(BF16), wdn_ref[cols, :])
    return x + _rms(acc, g2_ref[...])


def _mlp_kernel(x_ref, g1_ref, wup_ref, wdn_ref, g2_ref, o_ref):
    o_ref[...] = _mlp_body(x_ref[...], g1_ref, wup_ref, wdn_ref, g2_ref)


def _mix_mlp_kernel(x_ref, a_ref, wo_ref, gmix_ref, g1_ref, wup_ref, wdn_ref, g2_ref, o_ref):
    x = x_ref[...] + _rms(_dot(a_ref[...], wo_ref[...]), gmix_ref[...])
    o_ref[...] = _mlp_body(x, g1_ref, wup_ref, wdn_ref, g2_ref)


def _mlp_layer(x, g1, wup, wdn, g2, *, tm, mix=None):
    rows, d = x.shape
    f = wup.shape[1]
    tok = pl.BlockSpec((tm, d), lambda i: (i, 0))
    mlp_specs = [_const_spec((1, d)), _const_spec((d, f)), _const_spec((f, d)), _const_spec((1, d))]
    if mix is None:
        kern, specs, args = _mlp_kernel, [tok] + mlp_specs, (x, g1, wup, wdn, g2)
    else:
        a, wo, gmix = mix
        kern = _mix_mlp_kernel
        specs = [tok, pl.BlockSpec((tm, a.shape[1]), lambda i: (i, 0)), _const_spec(wo.shape), _const_spec((1, d))] + mlp_specs
        args = (x, a, wo, gmix, g1, wup, wdn, g2)
    return pl.pallas_call(
        kern,
        grid=(rows // tm,),
        in_specs=specs,
        out_specs=tok,
        out_shape=jax.ShapeDtypeStruct((rows, d), F32),
        compiler_params=_cparams("arbitrary"),
        name="mlp_layer",
    )(*args)


def _proj_kernel(x_ref, gkv_ref, wkv_ref, gq_ref, wq_ref, wg_ref, bg_ref, kv_ref, win_ref, sw_ref, q_ref, gate_ref):
    x = x_ref[...]
    rows = _dot(_rms(x, gkv_ref[...]).astype(BF16), wkv_ref[...])
    n_kv = kv_ref.shape[1]
    kv_ref[...] = rows[:, :n_kv]
    win_ref[...] = rows[:, n_kv:]
    sw_ref[...] = rows[:, n_kv // 2:].astype(BF16)
    u = _rms(x, gq_ref[...]).astype(BF16)
    q_ref[...] = (_dot(u, wq_ref[...]) * HEAD_DIM ** -0.5).astype(BF16)
    gate_ref[...] = jax.nn.sigmoid(_dot(u, wg_ref[...]) + bg_ref[...])


def _proj_layer(x, gkv, wkv, gq, wq, wg, bg, *, tm):
    rows, d = x.shape
    n_all = wkv.shape[1]
    n_kv = 4 * N_KV_GROUPS * HEAD_DIM
    tok = lambda w: pl.BlockSpec((tm, w), lambda i: (i, 0))
    return pl.pallas_call(
        _proj_kernel,
        grid=(rows // tm,),
        in_specs=[tok(d), _const_spec((1, d)), _const_spec(wkv.shape), _const_spec((1, d)), _const_spec(wq.shape),
                  _const_spec(wg.shape), _const_spec(bg.shape)],
        out_specs=[tok(n_kv), tok(n_all - n_kv), tok(n_all - n_kv // 2), tok(wq.shape[1]), tok(wg.shape[1])],
        out_shape=[jax.ShapeDtypeStruct((rows, n_kv), F32), jax.ShapeDtypeStruct((rows, n_all - n_kv), F32),
                   jax.ShapeDtypeStruct((rows, n_all - n_kv // 2), BF16), jax.ShapeDtypeStruct((rows, wq.shape[1]), BF16),
                   jax.ShapeDtypeStruct((rows, wg.shape[1]), F32)],
        compiler_params=_cparams("arbitrary"),
        name="kv_q_proj",
    )(x, gkv, wkv, gq, wq, wg, bg)


CMP_GROUP_PAIR = 2 * HEAD_DIM


def _compress_kernel(*refs, n_pref, n_in, row_w):
    x_refs = refs[n_pref:n_pref + n_in]
    pe_ref, w1_ref, b1_ref, w2_ref, kc_ref, vc_ref, carry_sc = refs[n_pref + n_in:]
    step = pl.program_id(1)

    @pl.when(step == 0)
    def _():
        carry_sc[...] = jnp.zeros_like(carry_sc)

    x = jnp.concatenate([r[0] for r in x_refs], axis=0) if n_in > 1 else x_refs[0][0]
    m = x.shape[0]
    set_w = N_KV_GROUPS * HEAD_DIM
    row = lax.broadcasted_iota(jnp.int32, (m, 2 * LANES), 0)
    for s, out_ref in enumerate((kc_ref, vc_ref)):
        for jp in range(set_w // CMP_GROUP_PAIR):
            off = s * set_w + jp * CMP_GROUP_PAIR
            xj = jnp.concatenate([x[:, tt * row_w + off:tt * row_w + off + CMP_GROUP_PAIR]
                                  for tt in range(CMP_STRIDE)], axis=1)
            first = _dot((xj + pe_ref[s, 0]).astype(BF16), w1_ref[s, 0])
            second = _dot((xj + pe_ref[s, 1]).astype(BF16), w1_ref[s, 1])
            slot = s * 2 + jp
            prev = jnp.where(row == 0, carry_sc[slot, 0:1, :], pltpu.roll(first, 1, 0))
            carry_sc[slot, 0:1, :] = first[m - 1:m, :]
            hid = jax.nn.gelu(prev + second + b1_ref[s])
            out_ref[0, :, jp * CMP_GROUP_PAIR:(jp + 1) * CMP_GROUP_PAIR] = _dot(hid.astype(BF16), w2_ref[s]).astype(BF16)


def _compress(src, cw, *, n_seq, rows_per_seq, block_rows, page_table=None, pages_per_step=1):
    row_w = src.shape[-1] // CMP_STRIDE
    n_in = pages_per_step if page_table is not None else 1
    m = n_in * block_rows
    steps = rows_per_seq // m
    set_w = N_KV_GROUPS * HEAD_DIM
    if page_table is None:
        x_specs = [pl.BlockSpec((1, block_rows, src.shape[-1]), lambda b, s: (b, s, 0))]
        n_pref = 0
    else:
        x_specs = [pl.BlockSpec((1, block_rows, src.shape[-1]),
                                functools.partial(lambda b, s, pt, p: (pt[b, s * n_in + p], 0, 0), p=p))
                   for p in range(n_in)]
        n_pref = 1
    cidx = lambda nd: (lambda *_: (0,) * nd)
    const = lambda a: pl.BlockSpec(a.shape, cidx(a.ndim), pipeline_mode=pl.Buffered(1))
    out_spec = pl.BlockSpec((1, m, set_w), lambda b, s, *_: (b, s, 0))
    grid_spec = pltpu.PrefetchScalarGridSpec(
        num_scalar_prefetch=n_pref,
        grid=(n_seq, steps),
        in_specs=x_specs + [const(cw['pe']), const(cw['w1']), const(cw['b1']), const(cw['w2'])],
        out_specs=[out_spec, out_spec],
        scratch_shapes=[pltpu.VMEM((4, SUBLANES, 2 * LANES), F32)],
    )
    call = pl.pallas_call(
        functools.partial(_compress_kernel, n_pref=n_pref, n_in=n_in, row_w=row_w),
        grid_spec=grid_spec,
        out_shape=[jax.ShapeDtypeStruct((n_seq, rows_per_seq, set_w), BF16)] * 2,
        compiler_params=_cparams("arbitrary", "arbitrary"),
        name="compress",
    )
    args = ([page_table] if page_table is not None else []) + [src] * n_in + [cw['pe'], cw['w1'], cw['b1'], cw['w2']]
    return call(*args)


KEY_TILE = 128
N_NEAR_TILES = REL_MAX_DIST // KEY_TILE + 1
TILE_FUTURE, TILE_FAR, TILE_WINDOW = 0, N_NEAR_TILES + 1, N_NEAR_TILES + 2


def _rel_bucket(dist):
    n = jnp.maximum(dist, 0)
    max_exact = NUM_BUCKETS // 2
    nf = jnp.maximum(n, 1).astype(F32)
    large = max_exact + (jnp.log(nf / max_exact) / math.log(REL_MAX_DIST / max_exact)
                         * (NUM_BUCKETS - max_exact)).astype(jnp.int32)
    return jnp.where(n < max_exact, n, jnp.minimum(large, NUM_BUCKETS - 1))


def _bias_by_dist(table):
    return table[_rel_bucket(jnp.arange(REL_MAX_DIST))].T.astype(F32)


def _bucket_thresholds():
    b = _rel_bucket(jnp.arange(REL_MAX_DIST))
    return jnp.sum(b[None, :] < jnp.arange(NUM_BUCKETS)[:, None], axis=1).astype(jnp.int32)


def _toeplitz_tiles(bd):
    i = jnp.arange(Q_BLOCK)[:, None]
    j = jnp.arange(KEY_TILE)[None, :]
    neg = jnp.full((bd.shape[0], Q_BLOCK, KEY_TILE), NEG_INF, F32)
    tiles = [neg]
    for off in range(N_NEAR_TILES):
        dist = KEY_TILE * off + i - j
        tiles.append(jnp.where(dist >= 0, bd[:, jnp.clip(dist, 0, REL_MAX_DIST - 1)], NEG_INF))
    tiles.append(jnp.broadcast_to(bd[:, REL_MAX_DIST - 1][:, None, None], neg.shape))
    dist = WINDOW + i - j
    tiles.append(jnp.where(dist < WINDOW, bd[:, jnp.clip(dist, 0, REL_MAX_DIST - 1)], NEG_INF))
    return jnp.stack(tiles)


SEL_CHUNK = 256
WIN_KEYS = WINDOW + Q_BLOCK


def _masked_softmax(s, mask):
    s = jnp.where(mask, s, NEG_INF)
    e = jnp.exp(s - jnp.max(s, axis=-1, keepdims=True))
    p = e / jnp.sum(e, axis=-1, keepdims=True)
    return jnp.where(mask, p, 0.0)


def _topk_mask_t(imp_sc, n_rows):
    v = imp_sc[...]
    j = lax.broadcasted_iota(jnp.int32, v.shape, 0)

    def body(k, cnt):
        vk = jnp.broadcast_to(imp_sc[pl.ds(k, 1), :], v.shape)
        tie = jnp.where(k < j, 1.0, 0.0)
        return cnt + jnp.where(vk > v, 1.0, 0.0) + jnp.where(vk == v, tie, 0.0)

    cnt = lax.fori_loop(0, n_rows, body, jnp.zeros(v.shape, F32))
    return jnp.where(cnt < float(N_SELECT), 1.0, 0.0)


def _nsa_prompt_kernel(thr_ref, q_ref, gate_ref, kct_ref, vc_ref, kst_ref, vs_ref, kwt_ref, vw_ref, tb_ref, tab_ref,
                       selmap_ref, eg_ref, o_ref, bucket_sc, imp_sc, m_sc, l_sc, acc_sc, o_sc):
    qb = pl.program_id(1)
    t0 = qb * Q_BLOCK
    gw = N_KV_GROUPS * HEAD_DIM
    n_cmp = kct_ref.shape[2]
    rows = GROUP_SIZE * Q_BLOCK

    qi = lax.broadcasted_iota(jnp.int32, (Q_BLOCK, n_cmp), 0)
    hc = lax.broadcasted_iota(jnp.int32, (Q_BLOCK, n_cmp), 1)
    dist_c = jnp.where(hc >= 1, t0 + qi - (CMP_STRIDE * hc + CMP_STRIDE - 1), -1)
    mask_c = dist_c >= 0
    bucket = jnp.zeros((Q_BLOCK, n_cmp), jnp.int32)
    for m in range(1, NUM_BUCKETS):
        bucket = bucket + jnp.where(dist_c >= thr_ref[m], 1, 0)
    bucket_sc[...] = bucket

    gexp = jnp.dot(gate_ref[0], eg_ref[...], preferred_element_type=F32, precision=lax.Precision.HIGHEST)
    lane_g = lax.broadcasted_iota(jnp.int32, (Q_BLOCK, gw), 1) // HEAD_DIM
    o_sc[...] = jnp.zeros_like(o_sc)

    nb = selmap_ref.shape[0]
    jb = lax.broadcasted_iota(jnp.int32, (nb, Q_BLOCK), 0)
    qpos_t = t0 + lax.broadcasted_iota(jnp.int32, (nb, Q_BLOCK), 1)
    cur = qpos_t // SEL_BLOCK
    forced = jnp.where(jb == 0, 1, 0) + jnp.where(jb == cur, 1, 0) + jnp.where(jb == cur - 1, 1, 0)
    valid_t = jb * SEL_BLOCK <= qpos_t
    n_valid_blocks = (t0 + Q_BLOCK - 1) // SEL_BLOCK + 1
    n_chunks = (t0 + Q_BLOCK - 1) // SEL_CHUNK + 1
    wkey = lax.broadcasted_iota(jnp.int32, (Q_BLOCK, WIN_KEYS), 1)
    win_ok = t0 - WINDOW + wkey >= 0

    def group_body(g, carry):
        qm = jnp.concatenate(
            [jnp.where(lane_g == g, q_ref[0, :, r * gw:(r + 1) * gw], 0.0).astype(BF16) for r in range(GROUP_SIZE)], axis=0)

        s_c = _dot(qm, kct_ref[0])
        p_rows = []
        for r in range(GROUP_SIZE):
            tab = jnp.broadcast_to(tab_ref[pl.ds(g * GROUP_SIZE + r, 1), :], (Q_BLOCK, LANES))
            bias = jnp.concatenate(
                [jnp.take_along_axis(tab, bucket_sc[:, c * LANES:(c + 1) * LANES], axis=1) for c in range(n_cmp // LANES)], axis=1)
            p_rows.append(_masked_softmax(s_c[r * Q_BLOCK:(r + 1) * Q_BLOCK] + bias, mask_c).astype(BF16))
        p_c = jnp.concatenate(p_rows, axis=0)
        o_cmp = _dot(p_c, vc_ref[0])

        imp = jnp.zeros((nb, Q_BLOCK), F32)
        for r in range(GROUP_SIZE):
            imp = imp + lax.dot_general(selmap_ref[...], p_rows[r], (((1,), (1,)), ((), ())), preferred_element_type=F32)
        imp_sc[...] = jnp.where(forced > 0, 1e9, jnp.where(valid_t, imp, -1e9))
        sel = _topk_mask_t(imp_sc, n_valid_blocks).T.astype(BF16)

        m_sc[...] = jnp.full_like(m_sc, NEG_INF)
        l_sc[...] = jnp.zeros_like(l_sc)
        acc_sc[...] = jnp.zeros_like(acc_sc)

        def chunk_body(c, carry2):
            off = pl.multiple_of(c * SEL_CHUNK, SEL_CHUNK)
            s = _dot(qm, kst_ref[0, :, pl.ds(off, SEL_CHUNK)])
            eb = lax.broadcasted_iota(jnp.int32, (nb, SEL_CHUNK), 0)
            ek = lax.broadcasted_iota(jnp.int32, (nb, SEL_CHUNK), 1)
            expand = jnp.where(eb == c * (SEL_CHUNK // SEL_BLOCK) + ek // SEL_BLOCK, 1.0, 0.0).astype(BF16)
            picked = _dot(sel, expand) > 0.5
            parts = []
            for r in range(GROUP_SIZE):
                bias = jnp.concatenate(
                    [tb_ref[jnp.clip(qb - (c * (SEL_CHUNK // KEY_TILE) + h), -1, N_NEAR_TILES) + 1, g * GROUP_SIZE + r]
                     for h in range(SEL_CHUNK // KEY_TILE)], axis=1)
                parts.append(jnp.where(picked, s[r * Q_BLOCK:(r + 1) * Q_BLOCK] + bias, NEG_INF))
            s = jnp.concatenate(parts, axis=0)
            m_old = m_sc[...]
            m_new = jnp.maximum(m_old, jnp.max(s, axis=-1, keepdims=True))
            alpha = jnp.exp(m_old - m_new)
            p = jnp.exp(s - m_new)
            l_sc[...] = alpha * l_sc[...] + jnp.sum(p, axis=-1, keepdims=True)
            acc_sc[...] = alpha * acc_sc[...] + _dot(p.astype(BF16), vs_ref[0, pl.ds(off, SEL_CHUNK), :])
            m_sc[...] = m_new
            return carry2

        lax.fori_loop(0, n_chunks, chunk_body, 0)
        o_sel = acc_sc[...] / l_sc[...]

        woff = pl.multiple_of(t0, Q_BLOCK)
        s_w = _dot(qm, kwt_ref[0, :, pl.ds(woff, WIN_KEYS)])
        pw_rows = []
        for r in range(GROUP_SIZE):
            tiles = [tb_ref[TILE_WINDOW, g * GROUP_SIZE + r]]
            tiles += [tb_ref[WINDOW // KEY_TILE - w + 1, g * GROUP_SIZE + r] for w in range(1, WIN_KEYS // KEY_TILE)]
            s_r = jnp.where(win_ok, s_w[r * Q_BLOCK:(r + 1) * Q_BLOCK] + jnp.concatenate(tiles, axis=1), NEG_INF)
            e = jnp.exp(s_r - jnp.max(s_r, axis=-1, keepdims=True))
            pw_rows.append((e / jnp.sum(e, axis=-1, keepdims=True)).astype(BF16))
        o_win = _dot(jnp.concatenate(pw_rows, axis=0), vw_ref[0, pl.ds(woff, WIN_KEYS), :])

        for r in range(GROUP_SIZE):
            rs = slice(r * Q_BLOCK, (r + 1) * Q_BLOCK)
            mix = (gexp[:, (3 * r) * gw:(3 * r + 1) * gw] * o_cmp[rs]
                   + gexp[:, (3 * r + 1) * gw:(3 * r + 2) * gw] * o_sel[rs]
                   + gexp[:, (3 * r + 2) * gw:(3 * r + 3) * gw] * o_win[rs])
            o_sc[:, r * gw:(r + 1) * gw] += jnp.where(lane_g == g, mix, 0.0)
        return carry

    lax.fori_loop(0, N_KV_GROUPS, group_body, 0)
    o_ref[0] = o_sc[...].astype(BF16)


def _nsa_prompt(q, gates, kct, vc, kst, selwin, kwt, vwp, tiles, tab, selmap, eg, thr):
    n, t, hw = q.shape
    gw = N_KV_GROUPS * HEAD_DIM
    n_cmp = kct.shape[2]
    rows = GROUP_SIZE * Q_BLOCK
    per_n = lambda shape, col=0: pl.BlockSpec(shape, lambda b, j: (b, 0, col), pipeline_mode=pl.Buffered(1))
    blk = lambda w: pl.BlockSpec((1, Q_BLOCK, w), lambda b, j: (b, j, 0))
    return pl.pallas_call(
        _nsa_prompt_kernel,
        grid=(n, t // Q_BLOCK),
        in_specs=[pl.BlockSpec(memory_space=pltpu.SMEM), blk(hw), blk(gates.shape[2]),
                  per_n((1, gw, n_cmp)), per_n((1, n_cmp, gw)), per_n((1, gw, t)), per_n((1, t, gw), 1),
                  per_n((1, gw, WINDOW + t)), per_n((1, WINDOW + t, gw)),
                  _const_spec(tiles.shape), _const_spec(tab.shape), _const_spec(selmap.shape), _const_spec(eg.shape)],
        out_specs=blk(hw),
        out_shape=jax.ShapeDtypeStruct((n, t, hw), BF16),
        scratch_shapes=[pltpu.VMEM((Q_BLOCK, n_cmp), jnp.int32), pltpu.VMEM((selmap.shape[0], Q_BLOCK), F32),
                        pltpu.VMEM((rows, 1), F32), pltpu.VMEM((rows, 1), F32), pltpu.VMEM((rows, gw), F32),
                        pltpu.VMEM((Q_BLOCK, hw), F32)],
        compiler_params=_cparams("arbitrary", "arbitrary"),
        name="nsa_prompt",
    )(thr, q, gates, kct, vc, kst, selwin, kwt, vwp, tiles, tab, selmap, eg)


DEC_T_PAD = SUBLANES
SAMPLE_BLOCKS_PAD = 256
NEW_KEYS_PAD = LANES


def _nsa_sample_kernel(*refs, n_pages, n_blocks):
    pt_ref, q_ref, gate_ref, kct_ref, vc_ref = refs[:5]
    page_refs = refs[5:5 + n_pages]
    (bsel_ref, bcmp_ref, bwin_ref, bnew_ref, new_ref, cwin_ref, selmap_ref, eg_ref, o_ref,
     imp_sc, sel_sc, m_sc, l_sc, acc_sc, oc_sc) = refs[5 + n_pages:]
    c = pl.program_id(1)
    gw = N_KV_GROUPS * HEAD_DIM
    rows = N_HEADS * DEC_T_PAD
    page = page_refs[0].shape[1]
    lane_g = lax.broadcasted_iota(jnp.int32, (DEC_T_PAD, gw), 1) // HEAD_DIM
    qm = jnp.concatenate(
        [jnp.where(lane_g == g, q_ref[0, :, r * gw:(r + 1) * gw], 0.0)
         for r in range(GROUP_SIZE) for g in range(N_KV_GROUPS)], axis=0).astype(BF16)
    nt = (((1,), (1,)), ((), ()))

    @pl.when(c == 0)
    def _():
        p_c = _masked_softmax(_dot(qm, kct_ref[0]) + bcmp_ref[...], bcmp_ref[...] > 0.5 * NEG_INF).astype(BF16)
        oc_sc[...] = _dot(p_c, vc_ref[0])
        imp_all = lax.dot_general(selmap_ref[...], p_c, nt, preferred_element_type=F32)
        per_r = rows // GROUP_SIZE
        imp = imp_all
        for r in range(1, GROUP_SIZE):
            imp = imp + pltpu.roll(imp_all, r * per_r, 1)
        jb = lax.broadcasted_iota(jnp.int32, imp.shape, 0)
        last = n_blocks - 1
        forced = jnp.where(jb == 0, 1, 0) + jnp.where(jb == last, 1, 0) + jnp.where(jb == last - 1, 1, 0)
        imp_sc[...] = jnp.where(forced > 0, 1e9, jnp.where(jb <= last, imp, -1e9))
        sel_sc[...] = _topk_mask_t(imp_sc, n_blocks).T.astype(BF16)
        m_sc[...] = jnp.full_like(m_sc, NEG_INF)
        l_sc[...] = jnp.zeros_like(l_sc)
        acc_sc[...] = jnp.zeros_like(acc_sc)

    def online_update(s, v):
        m_old = m_sc[...]
        m_new = jnp.maximum(m_old, jnp.max(s, axis=-1, keepdims=True))
        alpha = jnp.exp(m_old - m_new)
        p = jnp.exp(s - m_new)
        l_sc[...] = alpha * l_sc[...] + jnp.sum(p, axis=-1, keepdims=True)
        acc_sc[...] = alpha * acc_sc[...] + _dot(p.astype(BF16), v)
        m_sc[...] = m_new

    keys = n_pages * page
    s = jnp.concatenate(
        [lax.dot_general(qm, pr[0, :, :gw].astype(BF16), nt, preferred_element_type=F32) for pr in page_refs], axis=1)
    eb = lax.broadcasted_iota(jnp.int32, (SAMPLE_BLOCKS_PAD, keys), 0)
    ek = lax.broadcasted_iota(jnp.int32, (SAMPLE_BLOCKS_PAD, keys), 1)
    expand = jnp.where(eb == c * (keys // SEL_BLOCK) + ek // SEL_BLOCK, 1.0, 0.0).astype(BF16)
    picked = _dot(sel_sc[...], expand) > 0.5
    v = jnp.concatenate([pr[0, :, gw:].astype(BF16) for pr in page_refs], axis=0)
    online_update(jnp.where(picked, s + bsel_ref[...], NEG_INF), v)

    @pl.when(c == pl.num_programs(1) - 1)
    def _():
        s_new = lax.dot_general(qm, new_ref[0, :, :gw], nt, preferred_element_type=F32) + bnew_ref[0]
        online_update(s_new, new_ref[0, :, gw:2 * gw])
        o_sel = acc_sc[...] / l_sc[...]

        s_w = lax.dot_general(qm, cwin_ref[0, :, :gw].astype(BF16), nt, preferred_element_type=F32) + bwin_ref[...]
        s_wn = lax.dot_general(qm, new_ref[0, :, 2 * gw:3 * gw], nt, preferred_element_type=F32) + bnew_ref[1]
        m_w = jnp.maximum(jnp.max(s_w, axis=-1, keepdims=True), jnp.max(s_wn, axis=-1, keepdims=True))
        e_w = jnp.exp(s_w - m_w)
        e_wn = jnp.exp(s_wn - m_w)
        l_w = jnp.sum(e_w, axis=-1, keepdims=True) + jnp.sum(e_wn, axis=-1, keepdims=True)
        o_win = _dot((e_w / l_w).astype(BF16), cwin_ref[0, :, gw:].astype(BF16)) + _dot(
            (e_wn / l_w).astype(BF16), new_ref[0, :, 3 * gw:])

        gexp = jnp.dot(gate_ref[0], eg_ref[...], preferred_element_type=F32, precision=lax.Precision.HIGHEST)
        o_cmp = oc_sc[...]
        for r in range(GROUP_SIZE):
            out_r = jnp.zeros((DEC_T_PAD, gw), F32)
            for g in range(N_KV_GROUPS):
                rs = slice((r * N_KV_GROUPS + g) * DEC_T_PAD, (r * N_KV_GROUPS + g + 1) * DEC_T_PAD)
                mix = (gexp[:, (3 * r) * gw:(3 * r + 1) * gw] * o_cmp[rs]
                       + gexp[:, (3 * r + 1) * gw:(3 * r + 2) * gw] * o_sel[rs]
                       + gexp[:, (3 * r + 2) * gw:(3 * r + 3) * gw] * o_win[rs])
                out_r = out_r + jnp.where(lane_g == g, mix, 0.0)
            o_ref[0, :, r * gw:(r + 1) * gw] = out_r


PAGES_PER_STEP = 8


def _nsa_sample(page_table, q, gates, kct, vc, cache, bias, new_rows, cache_win, selmap, eg, n_blocks):
    n, _, hw = q.shape
    gw = N_KV_GROUPS * HEAD_DIM
    n_cmp = kct.shape[2]
    rows = N_HEADS * DEC_T_PAD
    page = cache.shape[1]
    steps = page_table.shape[1] // PAGES_PER_STEP
    keys = PAGES_PER_STEP * page
    per_n = lambda a: pl.BlockSpec((1,) + a.shape[1:], lambda b, c, pt: (b, 0, 0))
    cidx = lambda nd: (lambda *_: (0,) * nd)
    const = lambda a: pl.BlockSpec(a.shape, cidx(a.ndim), pipeline_mode=pl.Buffered(1))
    page_specs = [pl.BlockSpec((1, page, 2 * gw), functools.partial(
        lambda b, c, pt, p: (pt[b, c * PAGES_PER_STEP + p], 0, 1), p=p)) for p in range(PAGES_PER_STEP)]
    grid_spec = pltpu.PrefetchScalarGridSpec(
        num_scalar_prefetch=1,
        grid=(n, steps),
        in_specs=[per_n(q), per_n(gates), per_n(kct), per_n(vc)] + page_specs + [
            pl.BlockSpec((rows, keys), lambda b, c, pt: (0, c)), const(bias['cmp']), const(bias['win']), const(bias['new']),
            per_n(new_rows), per_n(cache_win), const(selmap), const(eg)],
        out_specs=pl.BlockSpec((1, DEC_T_PAD, hw), lambda b, c, pt: (b, 0, 0)),
        scratch_shapes=[pltpu.VMEM((SAMPLE_BLOCKS_PAD, rows), F32), pltpu.VMEM((rows, SAMPLE_BLOCKS_PAD), BF16),
                        pltpu.VMEM((rows, 1), F32), pltpu.VMEM((rows, 1), F32), pltpu.VMEM((rows, gw), F32),
                        pltpu.VMEM((rows, gw), F32)],
    )
    return pl.pallas_call(
        functools.partial(_nsa_sample_kernel, n_pages=PAGES_PER_STEP, n_blocks=n_blocks),
        grid_spec=grid_spec,
        out_shape=jax.ShapeDtypeStruct((n, DEC_T_PAD, hw), F32),
        compiler_params=_cparams("arbitrary", "arbitrary"),
        name="nsa_sample",
    )(page_table, q, gates, kct, vc, *([cache] * PAGES_PER_STEP), bias['sel'], bias['cmp'], bias['win'], bias['new'],
      new_rows, cache_win, selmap, eg)


def _sample_bias(bd, past, dec_t, win_len, n_cmp_rows):
    rho = jnp.arange(N_HEADS * DEC_T_PAD)
    t = (rho % DEC_T_PAD)[:, None]
    rg = rho // DEC_T_PAD
    head = (rg % N_KV_GROUPS) * GROUP_SIZE + rg // N_KV_GROUPS
    look = lambda dist: bd[head[:, None], jnp.clip(dist, 0, REL_MAX_DIST - 1)]
    sel = look(past + t - jnp.arange(past)[None, :])
    h = jnp.arange(n_cmp_rows)[None, :]
    dist_c = past + t - (CMP_STRIDE * h + CMP_STRIDE - 1)
    cmp_b = jnp.where((h >= 1) & (dist_c >= 0), look(dist_c), NEG_INF)
    j = jnp.arange(win_len)[None, :]
    dist_w = win_len + t - j
    win = jnp.where((dist_w < WINDOW) & (past - win_len + j >= 0), look(dist_w), NEG_INF)
    tn = jnp.arange(NEW_KEYS_PAD)[None, :]
    new = jnp.where((tn <= t) & (tn < dec_t), look(t - tn), NEG_INF)
    return dict(sel=sel, cmp=cmp_b, win=win, new=jnp.stack([new, new]))


GATE_LANES = LANES


def _prepare(p):
    d = p['norm_pre_mix'].shape[1]
    hw = N_HEADS * HEAD_DIM
    gw = N_KV_GROUPS * HEAD_DIM
    row = lambda v: v.reshape(1, -1)
    w = {}
    w['rg'] = dict(
        g_pre=row(p['norm_pre_mix'][0]), w_in=p['rg_w_in'][0].astype(BF16), conv_w=p['rg_conv_w'][0],
        conv_b=row(p['rg_conv_b'][0]),
        w_ax=jnp.concatenate([p['rg_w_a'][0], p['rg_w_x'][0]], axis=2).astype(BF16),
        b_ax=jnp.stack([p['rg_b_a'][0], p['rg_b_x'][0]]), lam=row(p['rg_lambda'][0]),
        w_out=p['rg_w_out'][0].astype(BF16), g_post=row(p['norm_post_mix'][0]))
    w['mlp'] = [dict(g1=row(p['norm_pre_mlp'][l]), wup=p['w_mlp_up'][l].astype(BF16),
                     wdn=p['w_mlp_down'][l].astype(BF16), g2=row(p['norm_post_mlp'][l])) for l in range(2)]
    wqg = p['nsa_w_qg'][0]
    n_gate = wqg.shape[1] - hw
    w['proj'] = dict(
        gkv=row(p['kv_norm']), wkv=p['w_kv'].astype(BF16), gq=row(p['norm_pre_mix'][1]),
        wq=wqg[:, :hw].reshape(d, N_KV_GROUPS, GROUP_SIZE, HEAD_DIM).transpose(0, 2, 1, 3).reshape(d, hw).astype(BF16),
        wg=jnp.pad(wqg[:, hw:], ((0, 0), (0, GATE_LANES - n_gate))).astype(BF16),
        bg=jnp.pad(p['nsa_b_g'][0], (0, GATE_LANES - n_gate)).reshape(1, -1))
    w['wo'] = p['nsa_w_o'][0].reshape(N_KV_GROUPS, GROUP_SIZE, HEAD_DIM, d).transpose(1, 0, 2, 3).reshape(hw, d).astype(BF16)
    w['g_post_mix1'] = row(p['norm_post_mix'][1])
    eye2 = jnp.eye(2, dtype=F32)
    half = L_CMP // CMP_STRIDE
    w['cmp'] = dict(
        pe=jnp.broadcast_to(p['cmp_pe'].reshape(2, half, CMP_STRIDE, 1, HEAD_DIM),
                            (2, half, CMP_STRIDE, 2, HEAD_DIM)).reshape(2, half, 1, CMP_STRIDE * CMP_GROUP_PAIR),
        w1=jnp.einsum('sftdc,gh->sftgdhc', p['cmp_w1'].reshape(2, half, CMP_STRIDE, HEAD_DIM, -1), eye2)
        .reshape(2, half, CMP_STRIDE * CMP_GROUP_PAIR, -1).astype(BF16),
        b1=jnp.tile(p['cmp_b1'], (1, 2)).reshape(2, 1, -1),
        w2=jnp.einsum('scd,gh->sgchd', p['cmp_w2'], eye2).reshape(2, -1, CMP_GROUP_PAIR).astype(BF16))
    bd = _bias_by_dist(p['rel_bias_table'])
    w['bd'] = bd
    w['tiles'] = _toeplitz_tiles(bd)
    w['tab'] = jnp.pad(p['rel_bias_table'].T.astype(F32), ((0, 0), (0, LANES - NUM_BUCKETS)))
    w['thr'] = _bucket_thresholds()
    h_i = jnp.arange(N_HEADS)
    src = (h_i[:, None] * 3 + jnp.arange(3)[None, :]).reshape(-1)
    dst0 = (((h_i % GROUP_SIZE)[:, None] * 3 + jnp.arange(3)[None, :]) * gw
            + (h_i // GROUP_SIZE)[:, None] * HEAD_DIM).reshape(-1)
    lanes = jnp.arange(3 * GROUP_SIZE * gw)
    w['eg'] = jnp.zeros((GATE_LANES, lanes.shape[0]), F32).at[src].set(
        ((lanes[None, :] >= dst0[:, None]) & (lanes[None, :] < dst0[:, None] + HEAD_DIM)).astype(F32))
    return w


def _selmap_t(n_blocks, n_cmp_rows):
    ratio = SEL_BLOCK // CMP_STRIDE
    h = jnp.arange(n_cmp_rows)[None, :]
    j = jnp.arange(n_blocks)[:, None]
    m = ((h - 1) // ratio == j).astype(F32) + (h // ratio == j).astype(F32)
    return jnp.where(h >= 1, m, 0.0).astype(BF16)


def _trunk_prompt(x, w):
    n, t, d = x.shape
    gw = N_KV_GROUPS * HEAD_DIM
    zeros = lambda r: jnp.zeros((n, r, d), F32)
    x1, h_last, c_last = _rglru_layer(x, zeros(1), zeros(CONV_W - 1), w['rg'], stride=1, tm=256)
    m0 = w['mlp'][0]
    x2 = _mlp_layer(x1.reshape(n * t, d), m0['g1'], m0['wup'], m0['wdn'], m0['g2'], tm=512)
    pj = w['proj']
    kv4, win, selwin, q, gates = _proj_layer(x2, pj['gkv'], pj['wkv'], pj['gq'], pj['wq'], pj['wg'], pj['bg'], tm=512)
    nh = t // CMP_STRIDE
    kc, vc = _compress(kv4.reshape(n, nh, CMP_STRIDE * kv4.shape[1]), w['cmp'], n_seq=n, rows_per_seq=nh, block_rows=64)
    selwin = selwin.reshape(n, t, -1)
    kst = selwin[:, :, :gw].transpose(0, 2, 1)
    front = ((0, 0), (WINDOW, 0), (0, 0))
    kwt = jnp.pad(selwin[:, :, 2 * gw:3 * gw], front).transpose(0, 2, 1)
    vwp = jnp.pad(selwin[:, :, 3 * gw:], front)
    attn = _nsa_prompt(q.reshape(n, t, -1), gates.reshape(n, t, -1), kc.transpose(0, 2, 1), vc, kst, selwin, kwt, vwp,
                       w['tiles'], w['tab'], _selmap_t(t // SEL_BLOCK, nh), w['eg'], w['thr'])
    m1 = w['mlp'][1]
    y = _mlp_layer(x2, m1['g1'], m1['wup'], m1['wdn'], m1['g2'], tm=512,
                   mix=(attn.reshape(n * t, -1), w['wo'], w['g_post_mix1']))
    wlen = min(WINDOW, t)
    return (y.reshape(n, t, d), h_last.reshape(1, n, d), c_last.reshape(1, n, CONV_W - 1, d),
            kv4.reshape(n, t, 4, N_KV_GROUPS, HEAD_DIM),
            win.reshape(n, t, 2, N_KV_GROUPS, HEAD_DIM)[:, t - wlen:])


def _trunk_sample(x, h0, c0, cache_kv, cache_win, page_table, w):
    n, t, d = x.shape
    n_phys, page = cache_kv.shape[:2]
    past = page_table.shape[1] * page
    win_len = cache_win.shape[1]
    assert t < CMP_STRIDE and past % CMP_STRIDE == 0 and t <= DEC_T_PAD and page % CMP_STRIDE == 0
    assert page_table.shape[1] % PAGES_PER_STEP == 0 and page == KEY_TILE
    to_tn = lambda a: a.transpose(1, 0, 2).reshape(a.shape[0] * a.shape[1], -1)
    to_nt = lambda a, k: a.reshape(k, n, -1).transpose(1, 0, 2)
    tail = CONV_W - 1
    x1, h_last, c_last = _rglru_layer(to_tn(x)[None], h0[None], to_tn(c0)[None], w['rg'], stride=n, tm=t * n)
    m0 = w['mlp'][0]
    x2 = _mlp_layer(x1[0], m0['g1'], m0['wup'], m0['wdn'], m0['g2'], tm=t * n)
    pj = w['proj']
    kv4, win, selwin, q, gates = _proj_layer(x2, pj['gkv'], pj['wkv'], pj['gq'], pj['wq'], pj['wg'], pj['bg'], tm=t * n)

    row_w = cache_kv.shape[2] * cache_kv.shape[3] * cache_kv.shape[4]
    halves = page // CMP_STRIDE
    nh = past // CMP_STRIDE
    kc, vc = _compress(cache_kv.reshape(n_phys, halves, CMP_STRIDE * row_w), w['cmp'], n_seq=n, rows_per_seq=nh,
                       block_rows=halves, page_table=page_table, pages_per_step=PAGES_PER_STEP)
    pad_t = lambda a, k: jnp.pad(a, ((0, 0), (0, k - a.shape[1]), (0, 0)))
    n_blocks = -(-(past + t) // SEL_BLOCK)
    attn = _nsa_sample(
        page_table, pad_t(to_nt(q, t).astype(F32), DEC_T_PAD), pad_t(to_nt(gates, t), DEC_T_PAD), kc.transpose(0, 2, 1), vc,
        cache_kv.reshape(n_phys, page, row_w), _sample_bias(w['bd'], past, t, win_len, nh),
        pad_t(to_nt(selwin, t), NEW_KEYS_PAD), cache_win.reshape(n, win_len, -1), _selmap_t(SAMPLE_BLOCKS_PAD, nh), w['eg'],
        n_blocks)
    m1 = w['mlp'][1]
    y = _mlp_layer(x2, m1['g1'], m1['wup'], m1['wdn'], m1['g2'], tm=t * n,
                   mix=(to_tn(attn[:, :t]).astype(BF16), w['wo'], w['g_post_mix1']))
    win_rows = to_nt(win, t).reshape(n, t, 2, N_KV_GROUPS, HEAD_DIM)
    win_new = jnp.concatenate([cache_win, win_rows], axis=1)[:, t:]
    return (to_nt(y, t), h_last, to_nt(c_last[0], tail)[None],
            to_nt(kv4, t).reshape(n, t, 4, N_KV_GROUPS, HEAD_DIM), win_new)


def kernel(x_prompt, x_sample, state_rglru_h, state_rglru_conv, cache_kv, cache_win, page_table, norm_pre_mix,
           norm_post_mix, norm_pre_mlp, norm_post_mlp, w_mlp_up, w_mlp_down, rg_w_in, rg_conv_w, rg_conv_b, rg_w_a, rg_b_a,
           rg_w_x, rg_b_x, rg_lambda, rg_w_out, kv_norm, w_kv, cmp_pe, cmp_w1, cmp_b1, cmp_w2, nsa_w_qg, nsa_b_g, nsa_w_o,
           rel_bias_table):
    w = _prepare(dict(
        norm_pre_mix=norm_pre_mix, norm_post_mix=norm_post_mix, norm_pre_mlp=norm_pre_mlp, norm_post_mlp=norm_post_mlp,
        w_mlp_up=w_mlp_up, w_mlp_down=w_mlp_down, rg_w_in=rg_w_in, rg_conv_w=rg_conv_w, rg_conv_b=rg_conv_b,
        rg_w_a=rg_w_a, rg_b_a=rg_b_a, rg_w_x=rg_w_x, rg_b_x=rg_b_x, rg_lambda=rg_lambda, rg_w_out=rg_w_out,
        kv_norm=kv_norm, w_kv=w_kv, cmp_pe=cmp_pe, cmp_w1=cmp_w1, cmp_b1=cmp_b1, cmp_w2=cmp_w2, nsa_w_qg=nsa_w_qg,
        nsa_b_g=nsa_b_g, nsa_w_o=nsa_w_o, rel_bias_table=rel_bias_table))
    y_p, p_h, p_conv, p_kv, p_win = _trunk_prompt(x_prompt, w)
    y_s, s_h, s_conv, s_kv, s_win = _trunk_sample(x_sample, state_rglru_h[0], state_rglru_conv[0], cache_kv, cache_win,
                                                  page_table, w)
    return (y_p, y_s, p_h, p_conv, p_kv, p_win, s_h, s_conv, s_kv, s_win)
```

```python
import functools
import math

import jax
import jax.numpy as jnp
from jax import lax
from jax.experimental import pallas as pl
from jax.experimental.pallas import tpu as pltpu

F32 = jnp.float32
BF16 = jnp.bfloat16

N_HEADS = 16
N_KV_GROUPS = 4
GROUP_SIZE = N_HEADS // N_KV_GROUPS
HEAD_DIM = 64
N_RNN_BLOCKS = 4
CONV_W = 4
LRU_C = 8.0
L_CMP = 32
CMP_STRIDE = 16
SEL_BLOCK = 64
N_SELECT = 16
WINDOW = 512
Q_BLOCK = 128
NUM_BUCKETS = 32
REL_MAX_DIST = 1024
EPS = 1e-6
NEG_INF = -1e30

SUBLANES = 8
LANES = 128
VMEM_LIMIT_BYTES = 56 * 1024 * 1024


def _cparams(*sem):
    return pltpu.CompilerParams(dimension_semantics=sem, vmem_limit_bytes=VMEM_LIMIT_BYTES)


def _const_spec(shape):
    nd = len(shape)
    return pl.BlockSpec(shape, lambda *_: (0,) * nd, pipeline_mode=pl.Buffered(1))


def _rms(x, g):
    return x * lax.rsqrt(jnp.mean(x * x, axis=-1, keepdims=True) + EPS) * g


def _dot(a, b):
    return jnp.dot(a, b, preferred_element_type=F32)


def _rglru_kernel(x_ref, h0_ref, c0_ref, gpre_ref, win_ref, cw_ref, cb_ref, wax_ref, bax_ref, lam_ref,
                  wout_ref, gpost_ref, x1_ref, hlast_ref, clast_ref, xcat_sc, hprev_sc, *, stride, tm, pad):
    d = x_ref.shape[-1]
    tail = (CONV_W - 1) * stride
    j = pl.program_id(1)

    @pl.when(j == 0)
    def _():
        xcat_sc[pad - tail:pad, :] = c0_ref[0]
        hprev_sc[...] = jnp.zeros_like(hprev_sc)
        hprev_sc[tm - stride:tm, :] = h0_ref[0]

    x = x_ref[0]
    u = _rms(x, gpre_ref[...]).astype(BF16)
    proj = _dot(u, win_ref[...])
    gate = jax.nn.gelu(proj[:, :d])
    xcat_sc[pad:pad + tm, :] = proj[:, d:]
    xc = cb_ref[...] + cw_ref[CONV_W - 1:CONV_W, :] * xcat_sc[pad:pad + tm, :]
    for lag in range(1, CONV_W):
        xc = xc + cw_ref[CONV_W - 1 - lag:CONV_W - lag, :] * xcat_sc[pad - lag * stride:pad - lag * stride + tm, :]
    xcat_sc[pad - tail:pad, :] = xcat_sc[pad + tm - tail:pad + tm, :]
    clast_ref[0] = xcat_sc[pad - tail:pad, :]

    xcb = xc.astype(BF16)
    bw = d // N_RNN_BLOCKS
    ra, rx = [], []
    for blk in range(N_RNN_BLOCKS):
        pre = _dot(xcb[:, blk * bw:(blk + 1) * bw], wax_ref[blk])
        ra.append(pre[:, :bw])
        rx.append(pre[:, bw:])
    r = jax.nn.sigmoid(jnp.concatenate(ra, axis=1) + bax_ref[0:1, :])
    i = jax.nn.sigmoid(jnp.concatenate(rx, axis=1) + bax_ref[1:2, :])
    log_a = r * (-LRU_C * jax.nn.softplus(-lam_ref[...]))
    a = jnp.exp(log_a)
    b = jnp.sqrt(-jnp.tanh(log_a) * (a * a + 1.0)) * i * xc

    row = lax.broadcasted_iota(jnp.int32, (tm, d), 0)
    b = b + jnp.where(row < stride, a * pltpu.roll(hprev_sc[...], stride, 0), 0.0)
    s = stride
    while s < tm:
        keep = row >= s
        b = a * jnp.where(keep, pltpu.roll(b, s, 0), 0.0) + b
        a = a * jnp.where(keep, pltpu.roll(a, s, 0), 1.0)
        s *= 2
    h = b
    hprev_sc[...] = h
    hlast_ref[0] = h[tm - stride:tm, :]

    y = _dot((h * gate).astype(BF16), wout_ref[...])
    x1_ref[0] = x + _rms(y, gpost_ref[...])


def _rglru_layer(x, h0, c0, w, *, stride, tm):
    n, rows, d = x.shape
    tail = (CONV_W - 1) * stride
    pad = -(-tail // SUBLANES) * SUBLANES
    kern = functools.partial(_rglru_kernel, stride=stride, tm=tm, pad=pad)
    seq = lambda shape: pl.BlockSpec(shape, lambda b, j: (b, j, 0))
    per_n = lambda shape: pl.BlockSpec(shape, lambda b, j: (b, 0, 0))
    return pl.pallas_call(
        kern,
        grid=(n, rows // tm),
        in_specs=[seq((1, tm, d)), per_n((1, stride, d)), per_n((1, tail, d)),
                  _const_spec((1, d)), _const_spec(w['w_in'].shape), _const_spec((CONV_W, d)), _const_spec((1, d)),
                  _const_spec(w['w_ax'].shape), _const_spec((2, d)), _const_spec((1, d)),
                  _const_spec(w['w_out'].shape), _const_spec((1, d))],
        out_specs=[seq((1, tm, d)), per_n((1, stride, d)), per_n((1, tail, d))],
        out_shape=[jax.ShapeDtypeStruct((n, rows, d), F32), jax.ShapeDtypeStruct((n, stride, d), F32),
                   jax.ShapeDtypeStruct((n, tail, d), F32)],
        scratch_shapes=[pltpu.VMEM((pad + tm, d), F32), pltpu.VMEM((tm, d), F32)],
        compiler_params=_cparams("arbitrary", "arbitrary"),
        name="rglru_layer",
    )(x, h0, c0, w['g_pre'], w['w_in'], w['conv_w'], w['conv_b'], w['w_ax'], w['b_ax'], w['lam'], w['w_out'], w['g_post'])


MLP_HIDDEN_CHUNK = 1024


def _mlp_body(x, g1_ref, wup_ref, wdn_ref, g2_ref):
    u = _rms(x, g1_ref[...]).astype(BF16)
    f = wup_ref.shape[1]
    acc = jnp.zeros(x.shape, F32)
    for c in range(f // MLP_HIDDEN_CHUNK):
        cols = slice(c * MLP_HIDDEN_CHUNK, (c + 1) * MLP_HIDDEN_CHUNK)
        hid = jnp.maximum(_dot(u, wup_ref[:, cols]), 0.0)
        acc = acc + _dot((hid * hid).astype(BF16), wdn_ref[cols, :])
    return x + _rms(acc, g2_ref[...])


def _mlp_kernel(x_ref, g1_ref, wup_ref, wdn_ref, g2_ref, o_ref):
    o_ref[...] = _mlp_body(x_ref[...], g1_ref, wup_ref, wdn_ref, g2_ref)


def _mix_mlp_kernel(x_ref, a_ref, wo_ref, gmix_ref, g1_ref, wup_ref, wdn_ref, g2_ref, o_ref):
    x = x_ref[...] + _rms(_dot(a_ref[...], wo_ref[...]), gmix_ref[...])
    o_ref[...] = _mlp_body(x, g1_ref, wup_ref, wdn_ref, g2_ref)


def _mlp_layer(x, g1, wup, wdn, g2, *, tm, mix=None):
    rows, d = x.shape
    f = wup.shape[1]
    tok = pl.BlockSpec((tm, d), lambda i: (i, 0))
    mlp_specs = [_const_spec((1, d)), _const_spec((d, f)), _const_spec((f, d)), _const_spec((1, d))]
    if mix is None:
        kern, specs, args = _mlp_kernel, [tok] + mlp_specs, (x, g1, wup, wdn, g2)
    else:
        a, wo, gmix = mix
        kern = _mix_mlp_kernel
        specs = [tok, pl.BlockSpec((tm, a.shape[1]), lambda i: (i, 0)), _const_spec(wo.shape), _const_spec((1, d))] + mlp_specs
        args = (x, a, wo, gmix, g1, wup, wdn, g2)
    return pl.pallas_call(
        kern,
        grid=(rows // tm,),
        in_specs=specs,
        out_specs=tok,
        out_shape=jax.ShapeDtypeStruct((rows, d), F32),
        compiler_params=_cparams("arbitrary"),
        name="mlp_layer",
    )(*args)


def _proj_kernel(x_ref, gkv_ref, wkv_ref, gq_ref, wq_ref, wg_ref, bg_ref, kv_ref, win_ref, sw_ref, q_ref, gate_ref,
                 *kt_refs):
    x = x_ref[...]
    rows = _dot(_rms(x, gkv_ref[...]).astype(BF16), wkv_ref[...])
    n_kv = kv_ref.shape[1]
    gw = N_KV_GROUPS * HEAD_DIM
    kv_ref[...] = rows[:, :n_kv]
    win_ref[...] = rows[:, n_kv:]
    sw_ref[...] = rows[:, n_kv // 2:].astype(BF16)
    if kt_refs:
        kst_ref, kwt_ref, vsa_ref = kt_refs
        kst_ref[0] = rows[:, 2 * gw:3 * gw].T.astype(BF16)
        kwt_ref[0] = rows[:, 4 * gw:5 * gw].T.astype(BF16)
        v_sel = rows[:, 3 * gw:4 * gw]
        lane_grp = lax.broadcasted_iota(jnp.int32, v_sel.shape, 1) // HEAD_DIM
        vsa_ref[...] = jnp.concatenate([jnp.where(lane_grp == ONES_GROUP[0], 1.0, v_sel),
                                        jnp.where(lane_grp == ONES_GROUP[1], 1.0, v_sel)], axis=1).astype(BF16)
    u = _rms(x, gq_ref[...]).astype(BF16)
    q_ref[...] = (_dot(u, wq_ref[...]) * HEAD_DIM ** -0.5).astype(BF16)
    gate_ref[...] = jax.nn.sigmoid(_dot(u, wg_ref[...]) + bg_ref[...])


def _proj_layer(x, gkv, wkv, gq, wq, wg, bg, *, tm, seq_len=None):
    rows, d = x.shape
    n_all = wkv.shape[1]
    n_kv = 4 * N_KV_GROUPS * HEAD_DIM
    gw = N_KV_GROUPS * HEAD_DIM
    tok = lambda w: pl.BlockSpec((tm, w), lambda i: (i, 0))
    out_specs = [tok(n_kv), tok(n_all - n_kv), tok(n_all - n_kv // 2), tok(wq.shape[1]), tok(wg.shape[1])]
    out_shape = [jax.ShapeDtypeStruct((rows, n_kv), F32), jax.ShapeDtypeStruct((rows, n_all - n_kv), F32),
                 jax.ShapeDtypeStruct((rows, n_all - n_kv // 2), BF16), jax.ShapeDtypeStruct((rows, wq.shape[1]), BF16),
                 jax.ShapeDtypeStruct((rows, wg.shape[1]), F32)]
    if seq_len is not None:
        tiles = seq_len // tm
        kt = pl.BlockSpec((1, gw, tm), lambda i: (i // tiles, 0, i % tiles))
        out_specs += [kt, kt, tok(2 * gw)]
        out_shape += [jax.ShapeDtypeStruct((rows // seq_len, gw, seq_len), BF16)] * 2
        out_shape += [jax.ShapeDtypeStruct((rows, 2 * gw), BF16)]
    return pl.pallas_call(
        _proj_kernel,
        grid=(rows // tm,),
        in_specs=[tok(d), _const_spec((1, d)), _const_spec(wkv.shape), _const_spec((1, d)), _const_spec(wq.shape),
                  _const_spec(wg.shape), _const_spec(bg.shape)],
        out_specs=out_specs,
        out_shape=out_shape,
        compiler_params=_cparams("arbitrary"),
        name="kv_q_proj",
    )(x, gkv, wkv, gq, wq, wg, bg)


CMP_GROUP_PAIR = 2 * HEAD_DIM


def _compress_kernel(*refs, n_pref, n_in, row_w):
    x_refs = refs[n_pref:n_pref + n_in]
    pe_ref, w1_ref, b1_ref, w2_ref, kc_ref, vc_ref, carry_sc = refs[n_pref + n_in:]
    step = pl.program_id(1)

    @pl.when(step == 0)
    def _():
        carry_sc[...] = jnp.zeros_like(carry_sc)

    x = jnp.concatenate([r[0] for r in x_refs], axis=0) if n_in > 1 else x_refs[0][0]
    m = x.shape[0]
    set_w = N_KV_GROUPS * HEAD_DIM
    row = lax.broadcasted_iota(jnp.int32, (m, 2 * LANES), 0)
    for s, out_ref in enumerate((kc_ref, vc_ref)):
        for jp in range(set_w // CMP_GROUP_PAIR):
            off = s * set_w + jp * CMP_GROUP_PAIR
            xj = jnp.concatenate([x[:, tt * row_w + off:tt * row_w + off + CMP_GROUP_PAIR]
                                  for tt in range(CMP_STRIDE)], axis=1)
            first = _dot((xj + pe_ref[s, 0]).astype(BF16), w1_ref[s, 0])
            second = _dot((xj + pe_ref[s, 1]).astype(BF16), w1_ref[s, 1])
            slot = s * 2 + jp
            prev = jnp.where(row == 0, carry_sc[slot, 0:1, :], pltpu.roll(first, 1, 0))
            carry_sc[slot, 0:1, :] = first[m - 1:m, :]
            hid = jax.nn.gelu(prev + second + b1_ref[s])
            out_ref[0, :, jp * CMP_GROUP_PAIR:(jp + 1) * CMP_GROUP_PAIR] = _dot(hid.astype(BF16), w2_ref[s]).astype(BF16)


def _compress(src, cw, *, n_seq, rows_per_seq, block_rows, page_table=None, pages_per_step=1):
    row_w = src.shape[-1] // CMP_STRIDE
    n_in = pages_per_step if page_table is not None else 1
    m = n_in * block_rows
    steps = rows_per_seq // m
    set_w = N_KV_GROUPS * HEAD_DIM
    if page_table is None:
        x_specs = [pl.BlockSpec((1, block_rows, src.shape[-1]), lambda b, s: (b, s, 0))]
        n_pref = 0
    else:
        x_specs = [pl.BlockSpec((1, block_rows, src.shape[-1]),
                                functools.partial(lambda b, s, pt, p: (pt[b, s * n_in + p], 0, 0), p=p))
                   for p in range(n_in)]
        n_pref = 1
    cidx = lambda nd: (lambda *_: (0,) * nd)
    const = lambda a: pl.BlockSpec(a.shape, cidx(a.ndim), pipeline_mode=pl.Buffered(1))
    out_spec = pl.BlockSpec((1, m, set_w), lambda b, s, *_: (b, s, 0))
    grid_spec = pltpu.PrefetchScalarGridSpec(
        num_scalar_prefetch=n_pref,
        grid=(n_seq, steps),
        in_specs=x_specs + [const(cw['pe']), const(cw['w1']), const(cw['b1']), const(cw['w2'])],
        out_specs=[out_spec, out_spec],
        scratch_shapes=[pltpu.VMEM((4, SUBLANES, 2 * LANES), F32)],
    )
    call = pl.pallas_call(
        functools.partial(_compress_kernel, n_pref=n_pref, n_in=n_in, row_w=row_w),
        grid_spec=grid_spec,
        out_shape=[jax.ShapeDtypeStruct((n_seq, rows_per_seq, set_w), BF16)] * 2,
        compiler_params=_cparams("arbitrary", "arbitrary"),
        name="compress",
    )
    args = ([page_table] if page_table is not None else []) + [src] * n_in + [cw['pe'], cw['w1'], cw['b1'], cw['w2']]
    return call(*args)


KEY_TILE = 128
N_NEAR_TILES = REL_MAX_DIST // KEY_TILE + 1
TILE_FUTURE, TILE_FAR, TILE_WINDOW = 0, N_NEAR_TILES + 1, N_NEAR_TILES + 2


def _rel_bucket(dist):
    n = jnp.maximum(dist, 0)
    max_exact = NUM_BUCKETS // 2
    nf = jnp.maximum(n, 1).astype(F32)
    large = max_exact + (jnp.log(nf / max_exact) / math.log(REL_MAX_DIST / max_exact)
                         * (NUM_BUCKETS - max_exact)).astype(jnp.int32)
    return jnp.where(n < max_exact, n, jnp.minimum(large, NUM_BUCKETS - 1))


def _bias_lookup(table, dist):
    onehot = (_rel_bucket(dist)[..., None] == jnp.arange(NUM_BUCKETS)).astype(F32)
    out = jnp.einsum('...b,bh->...h', onehot, table.astype(F32), precision=lax.Precision.HIGHEST)
    return jnp.moveaxis(out, -1, 0)


def _bucket_thresholds():
    b = _rel_bucket(jnp.arange(REL_MAX_DIST))
    return jnp.sum(b[None, :] < jnp.arange(NUM_BUCKETS)[:, None], axis=1).astype(jnp.int32)


def _toeplitz_tiles(table):
    i = jnp.arange(Q_BLOCK)[:, None]
    j = jnp.arange(KEY_TILE)[None, :]
    offs = jnp.array([-1] + list(range(N_NEAR_TILES)) + [N_NEAR_TILES + REL_MAX_DIST // KEY_TILE, WINDOW // KEY_TILE])
    dist = KEY_TILE * offs[:, None, None] + i - j
    ok = (dist >= 0) & ((jnp.arange(offs.shape[0]) != TILE_WINDOW)[:, None, None] | (dist < WINDOW))
    far = _bias_lookup(table, jnp.array(REL_MAX_DIST))
    return jnp.where(ok, _bias_lookup(table, dist) - far[:, None, None, None], NEG_INF).transpose(1, 0, 2, 3)


SEL_CHUNK = 512
ONES_GROUP = (1, 0)
WIN_KEYS = WINDOW + Q_BLOCK


def _masked_softmax(s, mask):
    s = jnp.where(mask, s, NEG_INF)
    e = jnp.exp(s - jnp.max(s, axis=-1, keepdims=True))
    p = e / jnp.sum(e, axis=-1, keepdims=True)
    return jnp.where(mask, p, 0.0)


def _topk_mask_t(imp_sc, n_rows):
    v = imp_sc[...]
    j = lax.broadcasted_iota(jnp.int32, v.shape, 0)

    def body(k, cnt):
        vk = jnp.broadcast_to(imp_sc[pl.ds(k, 1), :], v.shape)
        tie = jnp.where(k < j, 1.0, 0.0)
        return cnt + jnp.where(vk > v, 1.0, 0.0) + jnp.where(vk == v, tie, 0.0)

    cnt = lax.fori_loop(0, n_rows, body, jnp.zeros(v.shape, F32))
    return jnp.where(cnt < float(N_SELECT), 1.0, 0.0)


def _nsa_prompt_kernel(thr_ref, q_ref, gate_ref, kct_ref, vc_ref, kst_ref, vsa_ref, kwt_ref, vw_ref, tb_ref, tab_ref,
                       selmap_ref, selexp_ref, eg_ref, o_ref, bucket_sc, imp_sc, m_sc, acc_sc, s0_sc, s1_sc, o_sc):
    qb = pl.program_id(1)
    t0 = qb * Q_BLOCK
    gw = N_KV_GROUPS * HEAD_DIM
    n_cmp = kct_ref.shape[2]
    rows = GROUP_SIZE * Q_BLOCK

    qi = lax.broadcasted_iota(jnp.int32, (Q_BLOCK, n_cmp), 0)
    hc = lax.broadcasted_iota(jnp.int32, (Q_BLOCK, n_cmp), 1)
    dist_c = jnp.where(hc >= 1, t0 + qi - (CMP_STRIDE * hc + CMP_STRIDE - 1), -1)
    mask_c = dist_c >= 0
    bucket = jnp.zeros((Q_BLOCK, n_cmp), jnp.int32)
    for m in range(1, NUM_BUCKETS):
        bucket = bucket + jnp.where(dist_c >= thr_ref[m], 1, 0)
    bucket_sc[...] = bucket

    gexp = jnp.dot(gate_ref[0], eg_ref[...], preferred_element_type=F32, precision=lax.Precision.HIGHEST)
    lane_g = lax.broadcasted_iota(jnp.int32, (Q_BLOCK, gw), 1) // HEAD_DIM
    o_sc[...] = jnp.zeros_like(o_sc)

    nb = selmap_ref.shape[0]
    jb = lax.broadcasted_iota(jnp.int32, (nb, Q_BLOCK), 0)
    qpos_t = t0 + lax.broadcasted_iota(jnp.int32, (nb, Q_BLOCK), 1)
    cur = qpos_t // SEL_BLOCK
    forced = jnp.where(jb == 0, 1, 0) + jnp.where(jb == cur, 1, 0) + jnp.where(jb == cur - 1, 1, 0)
    valid_t = jb * SEL_BLOCK <= qpos_t
    n_valid_blocks = (t0 + Q_BLOCK - 1) // SEL_BLOCK + 1
    n_chunks = (t0 + Q_BLOCK - 1) // SEL_CHUNK + 1

    def group_body(g, carry):
        qm = jnp.concatenate(
            [jnp.where(lane_g == g, q_ref[0, :, r * gw:(r + 1) * gw], 0.0).astype(BF16) for r in range(GROUP_SIZE)], axis=0)

        s_c = _dot(qm, kct_ref[0])
        p_rows = []
        for r in range(GROUP_SIZE):
            tab = jnp.broadcast_to(tab_ref[pl.ds(g * GROUP_SIZE + r, 1), :], (Q_BLOCK, LANES))
            bias = jnp.concatenate(
                [jnp.take_along_axis(tab, bucket_sc[:, c * LANES:(c + 1) * LANES], axis=1) for c in range(n_cmp // LANES)], axis=1)
            p_rows.append(_masked_softmax(s_c[r * Q_BLOCK:(r + 1) * Q_BLOCK] + bias, mask_c).astype(BF16))
        p_c = jnp.concatenate(p_rows, axis=0)
        o_cmp = _dot(p_c, vc_ref[0])

        imp = jnp.zeros((nb, Q_BLOCK), F32)
        for r in range(GROUP_SIZE):
            imp = imp + lax.dot_general(selmap_ref[...], p_rows[r], (((1,), (1,)), ((), ())), preferred_element_type=F32)
        imp_sc[...] = jnp.where(forced > 0, 1e9, jnp.where(valid_t, imp, -1e9))
        sel = _topk_mask_t(imp_sc, n_valid_blocks).T.astype(BF16)

        m_sc[...] = jnp.full_like(m_sc, NEG_INF)
        acc_sc[...] = jnp.zeros_like(acc_sc)
        ones_at = jnp.where(g == ONES_GROUP[0], 1, 0)
        v_col = pl.multiple_of(ones_at * gw, gw)

        def scores(c):
            return _dot(qm, kst_ref[0, :, pl.ds(pl.multiple_of(c * SEL_CHUNK, SEL_CHUNK), SEL_CHUNK)])

        def sweep_step(c, s_cur, s_next):
            s_next[...] = scores(jnp.minimum(c + 1, n_chunks - 1))
            off = pl.multiple_of(c * SEL_CHUNK, SEL_CHUNK)
            picked = _dot(sel, selexp_ref[:, pl.ds(off, SEL_CHUNK)]) > 0.5
            parts = []
            for r in range(GROUP_SIZE):
                bias = jnp.concatenate(
                    [tb_ref[jnp.clip(qb - (c * (SEL_CHUNK // KEY_TILE) + h), -1, N_NEAR_TILES) + 1, g * GROUP_SIZE + r]
                     for h in range(SEL_CHUNK // KEY_TILE)], axis=1)
                parts.append(jnp.where(picked, s_cur[r * Q_BLOCK:(r + 1) * Q_BLOCK, :] + bias, NEG_INF))
            s = jnp.concatenate(parts, axis=0)
            m_old = m_sc[...]
            m_new = jnp.maximum(m_old, jnp.max(s, axis=-1, keepdims=True))
            alpha = jnp.exp(m_old - m_new)
            p = jnp.exp(s - jnp.concatenate([m_new] * (SEL_CHUNK // LANES), axis=1))
            pv = _dot(p.astype(BF16), vsa_ref[0, pl.ds(off, SEL_CHUNK), pl.ds(v_col, gw)])
            acc_sc[...] = jnp.concatenate([alpha] * (gw // LANES), axis=1) * acc_sc[...] + pv
            m_sc[...] = m_new

        s0_sc[...] = scores(0)

        def pair_body(pair, carry2):
            sweep_step(2 * pair, s0_sc, s1_sc)

            @pl.when(2 * pair + 1 < n_chunks)
            def _():
                sweep_step(2 * pair + 1, s1_sc, s0_sc)

            return carry2

        lax.fori_loop(0, (n_chunks + 1) // 2, pair_body, 0)
        acc = acc_sc[...]
        denom = jnp.where(ones_at == 1, acc[:, ONES_GROUP[1] * HEAD_DIM:ONES_GROUP[1] * HEAD_DIM + 1],
                          acc[:, ONES_GROUP[0] * HEAD_DIM:ONES_GROUP[0] * HEAD_DIM + 1])
        o_sel = acc / denom

        n_wt = WIN_KEYS // KEY_TILE
        w_off = [pl.multiple_of(jnp.maximum(qb - (n_wt - 1) + w, 0) * KEY_TILE, KEY_TILE) for w in range(n_wt)]
        w_tile = [jnp.where(qb - (n_wt - 1) + w >= 0, TILE_WINDOW if w == 0 else n_wt - w, TILE_FUTURE)
                  for w in range(n_wt)]
        s_w = _dot(qm, jnp.concatenate([kwt_ref[0, :, pl.ds(o, KEY_TILE)] for o in w_off], axis=1))
        pw_rows = []
        for r in range(GROUP_SIZE):
            bias = jnp.concatenate([tb_ref[ti, g * GROUP_SIZE + r] for ti in w_tile], axis=1)
            s_r = s_w[r * Q_BLOCK:(r + 1) * Q_BLOCK] + bias
            e = jnp.exp(s_r - jnp.max(s_r, axis=-1, keepdims=True))
            pw_rows.append((e / jnp.sum(e, axis=-1, keepdims=True)).astype(BF16))
        o_win = _dot(jnp.concatenate(pw_rows, axis=0),
                     jnp.concatenate([vw_ref[0, pl.ds(o, KEY_TILE), :] for o in w_off], axis=0))

        for r in range(GROUP_SIZE):
            rs = slice(r * Q_BLOCK, (r + 1) * Q_BLOCK)
            mix = (gexp[:, (3 * r) * gw:(3 * r + 1) * gw] * o_cmp[rs]
                   + gexp[:, (3 * r + 1) * gw:(3 * r + 2) * gw] * o_sel[rs]
                   + gexp[:, (3 * r + 2) * gw:(3 * r + 3) * gw] * o_win[rs])
            o_sc[:, r * gw:(r + 1) * gw] += jnp.where(lane_g == g, mix, 0.0)
        return carry

    lax.fori_loop(0, N_KV_GROUPS, group_body, 0)
    o_ref[0] = o_sc[...].astype(BF16)


def _nsa_prompt(q, gates, kct, vc, kst, vsa, kwt, selwin, tiles, tab, selmap, selexp, eg, thr):
    n, t, hw = q.shape
    gw = N_KV_GROUPS * HEAD_DIM
    n_cmp = kct.shape[2]
    rows = GROUP_SIZE * Q_BLOCK
    per_n = lambda shape, col=0: pl.BlockSpec(shape, lambda b, j: (b, 0, col), pipeline_mode=pl.Buffered(1))
    blk = lambda w: pl.BlockSpec((1, Q_BLOCK, w), lambda b, j: (b, j, 0))
    return pl.pallas_call(
        _nsa_prompt_kernel,
        grid=(n, t // Q_BLOCK),
        in_specs=[pl.BlockSpec(memory_space=pltpu.SMEM), blk(hw), blk(gates.shape[2]),
                  per_n((1, gw, n_cmp)), per_n((1, n_cmp, gw)), per_n((1, gw, t)), per_n((1, t, 2 * gw)),
                  per_n((1, gw, t)), per_n((1, t, gw), 3),
                  _const_spec(tiles.shape), _const_spec(tab.shape), _const_spec(selmap.shape), _const_spec(selexp.shape),
                  _const_spec(eg.shape)],
        out_specs=blk(hw),
        out_shape=jax.ShapeDtypeStruct((n, t, hw), BF16),
        scratch_shapes=[pltpu.VMEM((Q_BLOCK, n_cmp), jnp.int32), pltpu.VMEM((selmap.shape[0], Q_BLOCK), F32),
                        pltpu.VMEM((rows, LANES), F32), pltpu.VMEM((rows, gw), F32),
                        pltpu.VMEM((rows, SEL_CHUNK), F32), pltpu.VMEM((rows, SEL_CHUNK), F32),
                        pltpu.VMEM((Q_BLOCK, hw), F32)],
        compiler_params=_cparams("arbitrary", "arbitrary"),
        name="nsa_prompt",
    )(thr, q, gates, kct, vc, kst, vsa, kwt, selwin, tiles, tab, selmap, selexp, eg)


DEC_T_PAD = SUBLANES
SAMPLE_BLOCKS_PAD = 256
NEW_KEYS_PAD = LANES


def _nsa_sample_kernel(*refs, n_pages, n_blocks):
    pt_ref, q_ref, gate_ref, kct_ref, vc_ref = refs[:5]
    page_refs = refs[5:5 + n_pages]
    (bsel_ref, bcmp_ref, bwin_ref, bnew_ref, new_ref, cwin_ref, selmap_ref, eg_ref, o_ref,
     imp_sc, sel_sc, m_sc, l_sc, acc_sc, oc_sc) = refs[5 + n_pages:]
    c = pl.program_id(1)
    gw = N_KV_GROUPS * HEAD_DIM
    rows = N_HEADS * DEC_T_PAD
    page = page_refs[0].shape[1]
    lane_g = lax.broadcasted_iota(jnp.int32, (DEC_T_PAD, gw), 1) // HEAD_DIM
    qm = jnp.concatenate(
        [jnp.where(lane_g == g, q_ref[0, :, r * gw:(r + 1) * gw], 0.0)
         for r in range(GROUP_SIZE) for g in range(N_KV_GROUPS)], axis=0).astype(BF16)
    nt = (((1,), (1,)), ((), ()))

    @pl.when(c == 0)
    def _():
        p_c = _masked_softmax(_dot(qm, kct_ref[0]) + bcmp_ref[...], bcmp_ref[...] > 0.5 * NEG_INF).astype(BF16)
        oc_sc[...] = _dot(p_c, vc_ref[0])
        imp_all = lax.dot_general(selmap_ref[...], p_c, nt, preferred_element_type=F32)
        per_r = rows // GROUP_SIZE
        imp = imp_all
        for r in range(1, GROUP_SIZE):
            imp = imp + pltpu.roll(imp_all, r * per_r, 1)
        jb = lax.broadcasted_iota(jnp.int32, imp.shape, 0)
        last = n_blocks - 1
        forced = jnp.where(jb == 0, 1, 0) + jnp.where(jb == last, 1, 0) + jnp.where(jb == last - 1, 1, 0)
        imp_sc[...] = jnp.where(forced > 0, 1e9, jnp.where(jb <= last, imp, -1e9))
        sel_sc[...] = _topk_mask_t(imp_sc, n_blocks).T.astype(BF16)
        m_sc[...] = jnp.full_like(m_sc, NEG_INF)
        l_sc[...] = jnp.zeros_like(l_sc)
        acc_sc[...] = jnp.zeros_like(acc_sc)

    def online_update(s, v):
        m_old = m_sc[...]
        m_new = jnp.maximum(m_old, jnp.max(s, axis=-1, keepdims=True))
        alpha = jnp.exp(m_old - m_new)
        p = jnp.exp(s - m_new)
        l_sc[...] = alpha * l_sc[...] + jnp.sum(p, axis=-1, keepdims=True)
        acc_sc[...] = alpha * acc_sc[...] + _dot(p.astype(BF16), v)
        m_sc[...] = m_new

    keys = n_pages * page
    s = jnp.concatenate(
        [lax.dot_general(qm, pr[0, :, :gw].astype(BF16), nt, preferred_element_type=F32) for pr in page_refs], axis=1)
    eb = lax.broadcasted_iota(jnp.int32, (SAMPLE_BLOCKS_PAD, keys), 0)
    ek = lax.broadcasted_iota(jnp.int32, (SAMPLE_BLOCKS_PAD, keys), 1)
    expand = jnp.where(eb == c * (keys // SEL_BLOCK) + ek // SEL_BLOCK, 1.0, 0.0).astype(BF16)
    picked = _dot(sel_sc[...], expand) > 0.5
    v = jnp.concatenate([pr[0, :, gw:].astype(BF16) for pr in page_refs], axis=0)
    online_update(jnp.where(picked, s + bsel_ref[...], NEG_INF), v)

    @pl.when(c == pl.num_programs(1) - 1)
    def _():
        s_new = lax.dot_general(qm, new_ref[0, :, :gw], nt, preferred_element_type=F32) + bnew_ref[0]
        online_update(s_new, new_ref[0, :, gw:2 * gw])
        o_sel = acc_sc[...] / l_sc[...]

        s_w = lax.dot_general(qm, cwin_ref[0, :, :gw].astype(BF16), nt, preferred_element_type=F32) + bwin_ref[...]
        s_wn = lax.dot_general(qm, new_ref[0, :, 2 * gw:3 * gw], nt, preferred_element_type=F32) + bnew_ref[1]
        m_w = jnp.maximum(jnp.max(s_w, axis=-1, keepdims=True), jnp.max(s_wn, axis=-1, keepdims=True))
        e_w = jnp.exp(s_w - m_w)
        e_wn = jnp.exp(s_wn - m_w)
        l_w = jnp.sum(e_w, axis=-1, keepdims=True) + jnp.sum(e_wn, axis=-1, keepdims=True)
        o_win = _dot((e_w / l_w).astype(BF16), cwin_ref[0, :, gw:].astype(BF16)) + _dot(
            (e_wn / l_w).astype(BF16), new_ref[0, :, 3 * gw:])

        gexp = jnp.dot(gate_ref[0], eg_ref[...], preferred_element_type=F32, precision=lax.Precision.HIGHEST)
        o_cmp = oc_sc[...]
        for r in range(GROUP_SIZE):
            out_r = jnp.zeros((DEC_T_PAD, gw), F32)
            for g in range(N_KV_GROUPS):
                rs = slice((r * N_KV_GROUPS + g) * DEC_T_PAD, (r * N_KV_GROUPS + g + 1) * DEC_T_PAD)
                mix = (gexp[:, (3 * r) * gw:(3 * r + 1) * gw] * o_cmp[rs]
                       + gexp[:, (3 * r + 1) * gw:(3 * r + 2) * gw] * o_sel[rs]
                       + gexp[:, (3 * r + 2) * gw:(3 * r + 3) * gw] * o_win[rs])
                out_r = out_r + jnp.where(lane_g == g, mix, 0.0)
            o_ref[0, :, r * gw:(r + 1) * gw] = out_r


PAGES_PER_STEP = 8


def _nsa_sample(page_table, q, gates, kct, vc, cache, bias, new_rows, cache_win, selmap, eg, n_blocks):
    n, _, hw = q.shape
    gw = N_KV_GROUPS * HEAD_DIM
    n_cmp = kct.shape[2]
    rows = N_HEADS * DEC_T_PAD
    page = cache.shape[1]
    steps = page_table.shape[1] // PAGES_PER_STEP
    keys = PAGES_PER_STEP * page
    per_n = lambda a: pl.BlockSpec((1,) + a.shape[1:], lambda b, c, pt: (b, 0, 0))
    cidx = lambda nd: (lambda *_: (0,) * nd)
    const = lambda a: pl.BlockSpec(a.shape, cidx(a.ndim), pipeline_mode=pl.Buffered(1))
    page_specs = [pl.BlockSpec((1, page, 2 * gw), functools.partial(
        lambda b, c, pt, p: (pt[b, c * PAGES_PER_STEP + p], 0, 1), p=p)) for p in range(PAGES_PER_STEP)]
    grid_spec = pltpu.PrefetchScalarGridSpec(
        num_scalar_prefetch=1,
        grid=(n, steps),
        in_specs=[per_n(q), per_n(gates), per_n(kct), per_n(vc)] + page_specs + [
            pl.BlockSpec((rows, keys), lambda b, c, pt: (0, c)), const(bias['cmp']), const(bias['win']), const(bias['new']),
            per_n(new_rows), per_n(cache_win), const(selmap), const(eg)],
        out_specs=pl.BlockSpec((1, DEC_T_PAD, hw), lambda b, c, pt: (b, 0, 0)),
        scratch_shapes=[pltpu.VMEM((SAMPLE_BLOCKS_PAD, rows), F32), pltpu.VMEM((rows, SAMPLE_BLOCKS_PAD), BF16),
                        pltpu.VMEM((rows, 1), F32), pltpu.VMEM((rows, 1), F32), pltpu.VMEM((rows, gw), F32),
                        pltpu.VMEM((rows, gw), F32)],
    )
    return pl.pallas_call(
        functools.partial(_nsa_sample_kernel, n_pages=PAGES_PER_STEP, n_blocks=n_blocks),
        grid_spec=grid_spec,
        out_shape=jax.ShapeDtypeStruct((n, DEC_T_PAD, hw), F32),
        compiler_params=_cparams("arbitrary", "arbitrary"),
        name="nsa_sample",
    )(page_table, q, gates, kct, vc, *([cache] * PAGES_PER_STEP), bias['sel'], bias['cmp'], bias['win'], bias['new'],
      new_rows, cache_win, selmap, eg)


def _sample_bias(table, past, dec_t, win_len, n_cmp_rows):
    t = jnp.arange(DEC_T_PAD)[:, None]

    def rows(dist, ok):
        b = jnp.where(ok, _bias_lookup(table, dist), NEG_INF)
        b = b.reshape(N_KV_GROUPS, GROUP_SIZE, DEC_T_PAD, -1).transpose(1, 0, 2, 3)
        return b.reshape(N_HEADS * DEC_T_PAD, -1)

    dist_s = past + t - jnp.arange(past)[None, :]
    h = jnp.arange(n_cmp_rows)[None, :]
    dist_c = past + t - (CMP_STRIDE * h + CMP_STRIDE - 1)
    j = jnp.arange(win_len)[None, :]
    dist_w = win_len + t - j
    tn = jnp.arange(NEW_KEYS_PAD)[None, :]
    new = rows(t - tn, (tn <= t) & (tn < dec_t))
    return dict(sel=rows(dist_s, dist_s >= 0), cmp=rows(dist_c, (h >= 1) & (dist_c >= 0)),
                win=rows(dist_w, (dist_w < WINDOW) & (past - win_len + j >= 0)), new=jnp.stack([new, new]))


GATE_LANES = LANES


def _prepare(p):
    d = p['norm_pre_mix'].shape[1]
    hw = N_HEADS * HEAD_DIM
    gw = N_KV_GROUPS * HEAD_DIM
    row = lambda v: v.reshape(1, -1)
    w = {}
    w['rg'] = dict(
        g_pre=row(p['norm_pre_mix'][0]), w_in=p['rg_w_in'][0].astype(BF16), conv_w=p['rg_conv_w'][0],
        conv_b=row(p['rg_conv_b'][0]),
        w_ax=jnp.concatenate([p['rg_w_a'][0], p['rg_w_x'][0]], axis=2).astype(BF16),
        b_ax=jnp.stack([p['rg_b_a'][0], p['rg_b_x'][0]]), lam=row(p['rg_lambda'][0]),
        w_out=p['rg_w_out'][0].astype(BF16), g_post=row(p['norm_post_mix'][0]))
    w['mlp'] = [dict(g1=row(p['norm_pre_mlp'][l]), wup=p['w_mlp_up'][l].astype(BF16),
                     wdn=p['w_mlp_down'][l].astype(BF16), g2=row(p['norm_post_mlp'][l])) for l in range(2)]
    wqg = p['nsa_w_qg'][0]
    n_gate = wqg.shape[1] - hw
    w['proj'] = dict(
        gkv=row(p['kv_norm']), wkv=p['w_kv'].astype(BF16), gq=row(p['norm_pre_mix'][1]),
        wq=wqg[:, :hw].reshape(d, N_KV_GROUPS, GROUP_SIZE, HEAD_DIM).transpose(0, 2, 1, 3).reshape(d, hw).astype(BF16),
        wg=jnp.pad(wqg[:, hw:], ((0, 0), (0, GATE_LANES - n_gate))).astype(BF16),
        bg=jnp.pad(p['nsa_b_g'][0], (0, GATE_LANES - n_gate)).reshape(1, -1))
    w['wo'] = p['nsa_w_o'][0].reshape(N_KV_GROUPS, GROUP_SIZE, HEAD_DIM, d).transpose(1, 0, 2, 3).reshape(hw, d).astype(BF16)
    w['g_post_mix1'] = row(p['norm_post_mix'][1])
    eye2 = jnp.eye(2, dtype=F32)
    half = L_CMP // CMP_STRIDE
    w['cmp'] = dict(
        pe=jnp.broadcast_to(p['cmp_pe'].reshape(2, half, CMP_STRIDE, 1, HEAD_DIM),
                            (2, half, CMP_STRIDE, 2, HEAD_DIM)).reshape(2, half, 1, CMP_STRIDE * CMP_GROUP_PAIR),
        w1=jnp.einsum('sftdc,gh->sftgdhc', p['cmp_w1'].reshape(2, half, CMP_STRIDE, HEAD_DIM, -1), eye2)
        .reshape(2, half, CMP_STRIDE * CMP_GROUP_PAIR, -1).astype(BF16),
        b1=jnp.tile(p['cmp_b1'], (1, 2)).reshape(2, 1, -1),
        w2=jnp.einsum('scd,gh->sgchd', p['cmp_w2'], eye2).reshape(2, -1, CMP_GROUP_PAIR).astype(BF16))
    w['table'] = p['rel_bias_table']
    w['tiles'] = _toeplitz_tiles(p['rel_bias_table'])
    w['tab'] = jnp.pad(p['rel_bias_table'].T.astype(F32), ((0, 0), (0, LANES - NUM_BUCKETS)))
    w['thr'] = _bucket_thresholds()
    h_i = jnp.arange(N_HEADS)
    src = (h_i[:, None] * 3 + jnp.arange(3)[None, :]).reshape(-1)
    dst0 = (((h_i % GROUP_SIZE)[:, None] * 3 + jnp.arange(3)[None, :]) * gw
            + (h_i // GROUP_SIZE)[:, None] * HEAD_DIM).reshape(-1)
    lanes = jnp.arange(3 * GROUP_SIZE * gw)
    w['eg'] = jnp.zeros((GATE_LANES, lanes.shape[0]), F32).at[src].set(
        ((lanes[None, :] >= dst0[:, None]) & (lanes[None, :] < dst0[:, None] + HEAD_DIM)).astype(F32))
    return w


def _selmap_t(n_blocks, n_cmp_rows):
    ratio = SEL_BLOCK // CMP_STRIDE
    h = jnp.arange(n_cmp_rows)[None, :]
    j = jnp.arange(n_blocks)[:, None]
    m = ((h - 1) // ratio == j).astype(F32) + (h // ratio == j).astype(F32)
    return jnp.where(h >= 1, m, 0.0).astype(BF16)


def _trunk_prompt(x, w):
    n, t, d = x.shape
    gw = N_KV_GROUPS * HEAD_DIM
    zeros = lambda r: jnp.zeros((n, r, d), F32)
    x1, h_last, c_last = _rglru_layer(x, zeros(1), zeros(CONV_W - 1), w['rg'], stride=1, tm=256)
    m0 = w['mlp'][0]
    x2 = _mlp_layer(x1.reshape(n * t, d), m0['g1'], m0['wup'], m0['wdn'], m0['g2'], tm=512)
    pj = w['proj']
    kv4, win, selwin, q, gates, kst, kwt, vsa = _proj_layer(
        x2, pj['gkv'], pj['wkv'], pj['gq'], pj['wq'], pj['wg'], pj['bg'], tm=512, seq_len=t)
    nh = t // CMP_STRIDE
    n_blocks = t // SEL_BLOCK
    kc, vc = _compress(kv4.reshape(n, nh, CMP_STRIDE * kv4.shape[1]), w['cmp'], n_seq=n, rows_per_seq=nh, block_rows=64)
    selexp = (jnp.arange(t)[None, :] // SEL_BLOCK == jnp.arange(n_blocks)[:, None]).astype(BF16)
    attn = _nsa_prompt(q.reshape(n, t, -1), gates.reshape(n, t, -1), kc.transpose(0, 2, 1), vc, kst, vsa.reshape(n, t, -1),
                       kwt, selwin.reshape(n, t, -1), w['tiles'], w['tab'], _selmap_t(n_blocks, nh), selexp, w['eg'], w['thr'])
    m1 = w['mlp'][1]
    y = _mlp_layer(x2, m1['g1'], m1['wup'], m1['wdn'], m1['g2'], tm=512,
                   mix=(attn.reshape(n * t, -1), w['wo'], w['g_post_mix1']))
    wlen = min(WINDOW, t)
    return (y.reshape(n, t, d), h_last.reshape(1, n, d), c_last.reshape(1, n, CONV_W - 1, d),
            kv4.reshape(n, t, 4, N_KV_GROUPS, HEAD_DIM),
            win.reshape(n, t, 2, N_KV_GROUPS, HEAD_DIM)[:, t - wlen:])


def _trunk_sample(x, h0, c0, cache_kv, cache_win, page_table, w):
    n, t, d = x.shape
    n_phys, page = cache_kv.shape[:2]
    past = page_table.shape[1] * page
    win_len = cache_win.shape[1]
    assert t < CMP_STRIDE and past % CMP_STRIDE == 0 and t <= DEC_T_PAD and page % CMP_STRIDE == 0
    assert page_table.shape[1] % PAGES_PER_STEP == 0 and page == KEY_TILE
    to_tn = lambda a: a.transpose(1, 0, 2).reshape(a.shape[0] * a.shape[1], -1)
    to_nt = lambda a, k: a.reshape(k, n, -1).transpose(1, 0, 2)
    tail = CONV_W - 1
    x1, h_last, c_last = _rglru_layer(to_tn(x)[None], h0[None], to_tn(c0)[None], w['rg'], stride=n, tm=t * n)
    m0 = w['mlp'][0]
    x2 = _mlp_layer(x1[0], m0['g1'], m0['wup'], m0['wdn'], m0['g2'], tm=t * n)
    pj = w['proj']
    kv4, win, selwin, q, gates = _proj_layer(x2, pj['gkv'], pj['wkv'], pj['gq'], pj['wq'], pj['wg'], pj['bg'], tm=t * n)

    row_w = cache_kv.shape[2] * cache_kv.shape[3] * cache_kv.shape[4]
    halves = page // CMP_STRIDE
    nh = past // CMP_STRIDE
    kc, vc = _compress(cache_kv.reshape(n_phys, halves, CMP_STRIDE * row_w), w['cmp'], n_seq=n, rows_per_seq=nh,
                       block_rows=halves, page_table=page_table, pages_per_step=PAGES_PER_STEP)
    pad_t = lambda a, k: jnp.pad(a, ((0, 0), (0, k - a.shape[1]), (0, 0)))
    n_blocks = -(-(past + t) // SEL_BLOCK)
    attn = _nsa_sample(
        page_table, pad_t(to_nt(q, t).astype(F32), DEC_T_PAD), pad_t(to_nt(gates, t), DEC_T_PAD), kc.transpose(0, 2, 1), vc,
        cache_kv.reshape(n_phys, page, row_w), _sample_bias(w['table'], past, t, win_len, nh),
        pad_t(to_nt(selwin, t), NEW_KEYS_PAD), cache_win.reshape(n, win_len, -1), _selmap_t(SAMPLE_BLOCKS_PAD, nh), w['eg'],
        n_blocks)
    m1 = w['mlp'][1]
    y = _mlp_layer(x2, m1['g1'], m1['wup'], m1['wdn'], m1['g2'], tm=t * n,
                   mix=(to_tn(attn[:, :t]).astype(BF16), w['wo'], w['g_post_mix1']))
    win_rows = to_nt(win, t).reshape(n, t, 2, N_KV_GROUPS, HEAD_DIM)
    win_new = jnp.concatenate([cache_win, win_rows], axis=1)[:, t:]
    return (to_nt(y, t), h_last, to_nt(c_last[0], tail)[None],
            to_nt(kv4, t).reshape(n, t, 4, N_KV_GROUPS, HEAD_DIM), win_new)


def kernel(x_prompt, x_sample, state_rglru_h, state_rglru_conv, cache_kv, cache_win, page_table, norm_pre_mix,
           norm_post_mix, norm_pre_mlp, norm_post_mlp, w_mlp_up, w_mlp_down, rg_w_in, rg_conv_w, rg_conv_b, rg_w_a, rg_b_a,
           rg_w_x, rg_b_x, rg_lambda, rg_w_out, kv_norm, w_kv, cmp_pe, cmp_w1, cmp_b1, cmp_w2, nsa_w_qg, nsa_b_g, nsa_w_o,
           rel_bias_table):
    w = _prepare(dict(
        norm_pre_mix=norm_pre_mix, norm_post_mix=norm_post_mix, norm_pre_mlp=norm_pre_mlp, norm_post_mlp=norm_post_mlp,
        w_mlp_up=w_mlp_up, w_mlp_down=w_mlp_down, rg_w_in=rg_w_in, rg_conv_w=rg_conv_w, rg_conv_b=rg_conv_b,
        rg_w_a=rg_w_a, rg_b_a=rg_b_a, rg_w_x=rg_w_x, rg_b_x=rg_b_x, rg_lambda=rg_lambda, rg_w_out=rg_w_out,
        kv_norm=kv_norm, w_kv=w_kv, cmp_pe=cmp_pe, cmp_w1=cmp_w1, cmp_b1=cmp_b1, cmp_w2=cmp_w2, nsa_w_qg=nsa_w_qg,
        nsa_b_g=nsa_b_g, nsa_w_o=nsa_w_o, rel_bias_table=rel_bias_table))
    y_p, p_h, p_conv, p_kv, p_win = _trunk_prompt(x_prompt, w)
    y_s, s_h, s_conv, s_kv, s_win = _trunk_sample(x_sample, state_rglru_h[0], state_rglru_conv[0], cache_kv, cache_win,
                                                  page_table, w)
    return (y_p, y_s, p_h, p_conv, p_kv, p_win, s_h, s_conv, s_kv, s_win)
```

```python
import functools
import math

import jax
import jax.numpy as jnp
from jax import lax
from jax.experimental import pallas as pl
from jax.experimental.pallas import tpu as pltpu

F32 = jnp.float32
BF16 = jnp.bfloat16

N_HEADS = 16
N_KV_GROUPS = 4
GROUP_SIZE = N_HEADS // N_KV_GROUPS
HEAD_DIM = 64
N_RNN_BLOCKS = 4
CONV_W = 4
LRU_C = 8.0
L_CMP = 32
CMP_STRIDE = 16
SEL_BLOCK = 64
N_SELECT = 16
WINDOW = 512
Q_BLOCK = 128
NUM_BUCKETS = 32
REL_MAX_DIST = 1024
EPS = 1e-6
NEG_INF = -1e30

SUBLANES = 8
LANES = 128
VMEM_LIMIT_BYTES = 56 * 1024 * 1024


def _cparams(*sem):
    return pltpu.CompilerParams(dimension_semantics=sem, vmem_limit_bytes=VMEM_LIMIT_BYTES)


def _const_spec(shape):
    nd = len(shape)
    return pl.BlockSpec(shape, lambda *_: (0,) * nd, pipeline_mode=pl.Buffered(1))


def _rms(x, g):
    return x * lax.rsqrt(jnp.mean(x * x, axis=-1, keepdims=True) + EPS) * g


def _dot(a, b):
    return jnp.dot(a, b, preferred_element_type=F32)


def _dot_split_rows(a, b):
    half = a.shape[0] // 2
    return jnp.concatenate([_dot(a[:half], b), _dot(a[half:], b)], axis=0)


def _rglru_kernel(x_ref, h0_ref, c0_ref, gpre_ref, win_ref, cw_ref, cb_ref, wax_ref, bax_ref, lam_ref,
                  wout_ref, gpost_ref, x1_ref, hlast_ref, clast_ref, xcat_sc, hprev_sc, *, stride, tm, pad):
    d = x_ref.shape[-1]
    tail = (CONV_W - 1) * stride
    j = pl.program_id(1)

    @pl.when(j == 0)
    def _():
        xcat_sc[pad - tail:pad, :] = c0_ref[0]
        hprev_sc[...] = jnp.zeros_like(hprev_sc)
        hprev_sc[tm - stride:tm, :] = h0_ref[0]

    x = x_ref[0]
    u = _rms(x, gpre_ref[...]).astype(BF16)
    proj = _dot(u, win_ref[...])
    gate = jax.nn.gelu(proj[:, :d])
    xcat_sc[pad:pad + tm, :] = proj[:, d:]
    xc = cb_ref[...] + cw_ref[CONV_W - 1:CONV_W, :] * xcat_sc[pad:pad + tm, :]
    for lag in range(1, CONV_W):
        xc = xc + cw_ref[CONV_W - 1 - lag:CONV_W - lag, :] * xcat_sc[pad - lag * stride:pad - lag * stride + tm, :]
    xcat_sc[pad - tail:pad, :] = xcat_sc[pad + tm - tail:pad + tm, :]
    clast_ref[0] = xcat_sc[pad - tail:pad, :]

    xcb = xc.astype(BF16)
    bw = d // N_RNN_BLOCKS
    ra, rx = [], []
    for blk in range(N_RNN_BLOCKS):
        pre = _dot(xcb[:, blk * bw:(blk + 1) * bw], wax_ref[blk])
        ra.append(pre[:, :bw])
        rx.append(pre[:, bw:])
    r = jax.nn.sigmoid(jnp.concatenate(ra, axis=1) + bax_ref[0:1, :])
    i = jax.nn.sigmoid(jnp.concatenate(rx, axis=1) + bax_ref[1:2, :])
    log_a = r * (-LRU_C * jax.nn.softplus(-lam_ref[...]))
    a = jnp.exp(log_a)
    b = jnp.sqrt(-jnp.tanh(log_a) * (a * a + 1.0)) * i * xc

    row = lax.broadcasted_iota(jnp.int32, (tm, d), 0)
    b = b + jnp.where(row < stride, a * pltpu.roll(hprev_sc[...], stride, 0), 0.0)
    s = stride
    while s < tm:
        keep = row >= s
        b = a * jnp.where(keep, pltpu.roll(b, s, 0), 0.0) + b
        a = a * jnp.where(keep, pltpu.roll(a, s, 0), 1.0)
        s *= 2
    h = b
    hprev_sc[...] = h
    hlast_ref[0] = h[tm - stride:tm, :]

    y = _dot((h * gate).astype(BF16), wout_ref[...])
    x1_ref[0] = x + _rms(y, gpost_ref[...])


def _rglru_layer(x, h0, c0, w, *, stride, tm):
    n, rows, d = x.shape
    tail = (CONV_W - 1) * stride
    pad = -(-tail // SUBLANES) * SUBLANES
    kern = functools.partial(_rglru_kernel, stride=stride, tm=tm, pad=pad)
    seq = lambda shape: pl.BlockSpec(shape, lambda b, j: (b, j, 0))
    per_n = lambda shape: pl.BlockSpec(shape, lambda b, j: (b, 0, 0))
    return pl.pallas_call(
        kern,
        grid=(n, rows // tm),
        in_specs=[seq((1, tm, d)), per_n((1, stride, d)), per_n((1, tail, d)),
                  _const_spec((1, d)), _const_spec(w['w_in'].shape), _const_spec((CONV_W, d)), _const_spec((1, d)),
                  _const_spec(w['w_ax'].shape), _const_spec((2, d)), _const_spec((1, d)),
                  _const_spec(w['w_out'].shape), _const_spec((1, d))],
        out_specs=[seq((1, tm, d)), per_n((1, stride, d)), per_n((1, tail, d))],
        out_shape=[jax.ShapeDtypeStruct((n, rows, d), F32), jax.ShapeDtypeStruct((n, stride, d), F32),
                   jax.ShapeDtypeStruct((n, tail, d), F32)],
        scratch_shapes=[pltpu.VMEM((pad + tm, d), F32), pltpu.VMEM((tm, d), F32)],
        compiler_params=_cparams("arbitrary", "arbitrary"),
        name="rglru_layer",
    )(x, h0, c0, w['g_pre'], w['w_in'], w['conv_w'], w['conv_b'], w['w_ax'], w['b_ax'], w['lam'], w['w_out'], w['g_post'])


MLP_HIDDEN_CHUNK = 1024


def _mlp_body(x, g1_ref, wup_ref, wdn_ref, g2_ref):
    u = _rms(x, g1_ref[...]).astype(BF16)
    f = wup_ref.shape[1]
    acc = jnp.zeros(x.shape, F32)
    for c in range(f // MLP_HIDDEN_CHUNK):
        cols = slice(c * MLP_HIDDEN_CHUNK, (c + 1) * MLP_HIDDEN_CHUNK)
        hid = jnp.maximum(_dot(u, wup_ref[:, cols]), 0.0)
        acc = acc + _dot((hid * hid).astype(BF16), wdn_ref[cols, :])
    return x + _rms(acc, g2_ref[...])


def _mlp_kernel(x_ref, g1_ref, wup_ref, wdn_ref, g2_ref, o_ref):
    o_ref[...] = _mlp_body(x_ref[...], g1_ref, wup_ref, wdn_ref, g2_ref)


def _mix_mlp_kernel(x_ref, a_ref, wo_ref, gmix_ref, g1_ref, wup_ref, wdn_ref, g2_ref, o_ref):
    x = x_ref[...] + _rms(_dot(a_ref[...], wo_ref[...]), gmix_ref[...])
    o_ref[...] = _mlp_body(x, g1_ref, wup_ref, wdn_ref, g2_ref)


def _mlp_layer(x, g1, wup, wdn, g2, *, tm, mix=None):
    rows, d = x.shape
    f = wup.shape[1]
    tok = pl.BlockSpec((tm, d), lambda i: (i, 0))
    mlp_specs = [_const_spec((1, d)), _const_spec((d, f)), _const_spec((f, d)), _const_spec((1, d))]
    if mix is None:
        kern, specs, args = _mlp_kernel, [tok] + mlp_specs, (x, g1, wup, wdn, g2)
    else:
        a, wo, gmix = mix
        kern = _mix_mlp_kernel
        specs = [tok, pl.BlockSpec((tm, a.shape[1]), lambda i: (i, 0)), _const_spec(wo.shape), _const_spec((1, d))] + mlp_specs
        args = (x, a, wo, gmix, g1, wup, wdn, g2)
    return pl.pallas_call(
        kern,
        grid=(rows // tm,),
        in_specs=specs,
        out_specs=tok,
        out_shape=jax.ShapeDtypeStruct((rows, d), F32),
        compiler_params=_cparams("arbitrary"),
        name="mlp_layer",
    )(*args)


def _proj_kernel(x_ref, gkv_ref, wkv_ref, gq_ref, wq_ref, wg_ref, bg_ref, kv_ref, win_ref, sw_ref, q_ref, gate_ref,
                 *kt_refs):
    x = x_ref[...]
    rows = _dot(_rms(x, gkv_ref[...]).astype(BF16), wkv_ref[...])
    n_kv = kv_ref.shape[1]
    gw = N_KV_GROUPS * HEAD_DIM
    kv_ref[...] = rows[:, :n_kv]
    win_ref[...] = rows[:, n_kv:]
    sw_ref[...] = rows[:, n_kv // 2:].astype(BF16)
    if kt_refs:
        kst_ref, kwt_ref, vsa_ref, kvt_ref, wint_ref = kt_refs
        rows_t = rows.T
        kvt_ref[0] = rows_t[:n_kv]
        wint_ref[0] = rows_t[n_kv:]
        kst_ref[0] = rows_t[2 * gw:3 * gw].astype(BF16)
        kwt_ref[0] = rows_t[4 * gw:5 * gw].astype(BF16)
        v_sel = rows[:, 3 * gw:4 * gw]
        lane_grp = lax.broadcasted_iota(jnp.int32, v_sel.shape, 1) // HEAD_DIM
        vsa_ref[...] = jnp.concatenate([jnp.where(lane_grp == ONES_GROUP[0], 1.0, v_sel),
                                        jnp.where(lane_grp == ONES_GROUP[1], 1.0, v_sel)], axis=1).astype(BF16)
    u = _rms(x, gq_ref[...]).astype(BF16)
    q_ref[...] = (_dot(u, wq_ref[...]) * HEAD_DIM ** -0.5).astype(BF16)
    gate_ref[...] = jax.nn.sigmoid(_dot(u, wg_ref[...]) + bg_ref[...])


def _proj_layer(x, gkv, wkv, gq, wq, wg, bg, *, tm, seq_len=None):
    rows, d = x.shape
    n_all = wkv.shape[1]
    n_kv = 4 * N_KV_GROUPS * HEAD_DIM
    gw = N_KV_GROUPS * HEAD_DIM
    tok = lambda w: pl.BlockSpec((tm, w), lambda i: (i, 0))
    out_specs = [tok(n_kv), tok(n_all - n_kv), tok(n_all - n_kv // 2), tok(wq.shape[1]), tok(wg.shape[1])]
    out_shape = [jax.ShapeDtypeStruct((rows, n_kv), F32), jax.ShapeDtypeStruct((rows, n_all - n_kv), F32),
                 jax.ShapeDtypeStruct((rows, n_all - n_kv // 2), BF16), jax.ShapeDtypeStruct((rows, wq.shape[1]), BF16),
                 jax.ShapeDtypeStruct((rows, wg.shape[1]), F32)]
    if seq_len is not None:
        tiles = seq_len // tm
        kt = pl.BlockSpec((1, gw, tm), lambda i: (i // tiles, 0, i % tiles))
        ktw = lambda w: pl.BlockSpec((1, w, tm), lambda i: (i // tiles, 0, i % tiles))
        out_specs += [kt, kt, tok(2 * gw), ktw(n_kv), ktw(n_all - n_kv)]
        out_shape += [jax.ShapeDtypeStruct((rows // seq_len, gw, seq_len), BF16)] * 2
        out_shape += [jax.ShapeDtypeStruct((rows, 2 * gw), BF16),
                      jax.ShapeDtypeStruct((rows // seq_len, n_kv, seq_len), F32),
                      jax.ShapeDtypeStruct((rows // seq_len, n_all - n_kv, seq_len), F32)]
    return pl.pallas_call(
        _proj_kernel,
        grid=(rows // tm,),
        in_specs=[tok(d), _const_spec((1, d)), _const_spec(wkv.shape), _const_spec((1, d)), _const_spec(wq.shape),
                  _const_spec(wg.shape), _const_spec(bg.shape)],
        out_specs=out_specs,
        out_shape=out_shape,
        compiler_params=_cparams("arbitrary"),
        name="kv_q_proj",
    )(x, gkv, wkv, gq, wq, wg, bg)


CMP_GROUP_PAIR = 2 * HEAD_DIM


def _compress_kernel(*refs, n_pref, n_in):
    x_refs = refs[n_pref:n_pref + n_in]
    pe_ref, w1_ref, b1_ref, w2_ref, kc_ref, vc_ref, carry_sc, xs_sc = refs[n_pref + n_in:]
    step = pl.program_id(1)

    @pl.when(step == 0)
    def _():
        carry_sc[...] = jnp.zeros_like(carry_sc)

    set_w = N_KV_GROUPS * HEAD_DIM
    n_chunks = 2 * set_w // CMP_GROUP_PAIR
    page = x_refs[0].shape[2]
    halves = page // CMP_STRIDE
    for p, x_ref in enumerate(x_refs):
        x = x_ref[0].T
        for cc in range(n_chunks):
            xs_sc[p, cc] = x[:, cc * CMP_GROUP_PAIR:(cc + 1) * CMP_GROUP_PAIR]
    m = n_in * halves
    row = lax.broadcasted_iota(jnp.int32, (m, 2 * LANES), 0)
    for s, out_ref in enumerate((kc_ref, vc_ref)):
        for jp in range(set_w // CMP_GROUP_PAIR):
            cc = s * (set_w // CMP_GROUP_PAIR) + jp
            xj = jnp.concatenate(
                [jnp.concatenate([xs_sc[p, cc, pl.ds(tt, halves, stride=CMP_STRIDE), :] for tt in range(CMP_STRIDE)], axis=1)
                 for p in range(n_in)], axis=0)
            first = _dot((xj + pe_ref[s, 0]).astype(BF16), w1_ref[s, 0])
            second = _dot((xj + pe_ref[s, 1]).astype(BF16), w1_ref[s, 1])
            slot = s * 2 + jp
            prev = jnp.where(row == 0, carry_sc[slot, 0:1, :], pltpu.roll(first, 1, 0))
            carry_sc[slot, 0:1, :] = first[m - 1:m, :]
            hid = jax.nn.gelu(prev + second + b1_ref[s])
            out_ref[0, :, jp * CMP_GROUP_PAIR:(jp + 1) * CMP_GROUP_PAIR] = _dot(hid.astype(BF16), w2_ref[s]).astype(BF16)


CMP_PAGES_PER_STEP = 16


def _compress(src, cw, *, n_seq, seq_pages, page, page_table=None):
    n_in = CMP_PAGES_PER_STEP
    set_w = N_KV_GROUPS * HEAD_DIM
    halves = page // CMP_STRIDE
    m = n_in * halves
    steps = seq_pages // n_in
    block = (1, 2 * set_w, page)
    if page_table is None:
        x_specs = [pl.BlockSpec(block, functools.partial(lambda b, s, p: (b, 0, s * n_in + p), p=p)) for p in range(n_in)]
        n_pref = 0
    else:
        x_specs = [pl.BlockSpec(block, functools.partial(lambda b, s, pt, p: (pt[b, s * n_in + p], 0, 0), p=p))
                   for p in range(n_in)]
        n_pref = 1
    cidx = lambda nd: (lambda *_: (0,) * nd)
    const = lambda a: pl.BlockSpec(a.shape, cidx(a.ndim), pipeline_mode=pl.Buffered(1))
    out_spec = pl.BlockSpec((1, m, set_w), lambda b, s, *_: (b, s, 0))
    grid_spec = pltpu.PrefetchScalarGridSpec(
        num_scalar_prefetch=n_pref,
        grid=(n_seq, steps),
        in_specs=x_specs + [const(cw['pe']), const(cw['w1']), const(cw['b1']), const(cw['w2'])],
        out_specs=[out_spec, out_spec],
        scratch_shapes=[pltpu.VMEM((4, SUBLANES, 2 * LANES), F32),
                        pltpu.VMEM((n_in, 2 * set_w // CMP_GROUP_PAIR, page, CMP_GROUP_PAIR), F32)],
    )
    call = pl.pallas_call(
        functools.partial(_compress_kernel, n_pref=n_pref, n_in=n_in),
        grid_spec=grid_spec,
        out_shape=[jax.ShapeDtypeStruct((n_seq, seq_pages * halves, set_w), BF16)] * 2,
        compiler_params=_cparams("arbitrary", "arbitrary"),
        name="compress",
    )
    args = ([page_table] if page_table is not None else []) + [src] * n_in + [cw['pe'], cw['w1'], cw['b1'], cw['w2']]
    return call(*args)


KEY_TILE = 128
N_NEAR_TILES = REL_MAX_DIST // KEY_TILE + 1
TILE_FUTURE, TILE_FAR, TILE_WINDOW = 0, N_NEAR_TILES + 1, N_NEAR_TILES + 2


def _rel_bucket(dist):
    n = jnp.maximum(dist, 0)
    max_exact = NUM_BUCKETS // 2
    nf = jnp.maximum(n, 1).astype(F32)
    large = max_exact + (jnp.log(nf / max_exact) / math.log(REL_MAX_DIST / max_exact)
                         * (NUM_BUCKETS - max_exact)).astype(jnp.int32)
    return jnp.where(n < max_exact, n, jnp.minimum(large, NUM_BUCKETS - 1))


def _bias_lookup(table, dist):
    onehot = (_rel_bucket(dist)[..., None] == jnp.arange(NUM_BUCKETS)).astype(F32)
    out = jnp.einsum('...b,bh->...h', onehot, table.astype(F32), precision=lax.Precision.HIGHEST)
    return jnp.moveaxis(out, -1, 0)


def _bucket_thresholds():
    b = _rel_bucket(jnp.arange(REL_MAX_DIST))
    return jnp.sum(b[None, :] < jnp.arange(NUM_BUCKETS)[:, None], axis=1).astype(jnp.int32)


def _toeplitz_tiles(table):
    i = jnp.arange(Q_BLOCK)[:, None]
    j = jnp.arange(KEY_TILE)[None, :]
    offs = jnp.array([-1] + list(range(N_NEAR_TILES)) + [N_NEAR_TILES + REL_MAX_DIST // KEY_TILE, WINDOW // KEY_TILE])
    dist = KEY_TILE * offs[:, None, None] + i - j
    ok = (dist >= 0) & ((jnp.arange(offs.shape[0]) != TILE_WINDOW)[:, None, None] | (dist < WINDOW))
    far = _bias_lookup(table, jnp.array(REL_MAX_DIST))
    return jnp.where(ok, _bias_lookup(table, dist) - far[:, None, None, None], NEG_INF).transpose(1, 0, 2, 3)


SEL_CHUNK = 512
ONES_GROUP = (1, 0)
WIN_KEYS = WINDOW + Q_BLOCK


def _masked_softmax(s, mask):
    s = jnp.where(mask, s, NEG_INF)
    e = jnp.exp(s - jnp.max(s, axis=-1, keepdims=True))
    p = e / jnp.sum(e, axis=-1, keepdims=True)
    return jnp.where(mask, p, 0.0)


def _topk_mask_t(v, key_sc, corr_sc, n_rows):
    key = jnp.where(v > 0.0, pltpu.bitcast(v, jnp.int32), jnp.where(v < 0.0, -1, 0))
    key_sc[...] = key
    corr_sc[...] = jnp.zeros_like(corr_sc)
    key_m1 = key - 1
    jg = lax.broadcasted_iota(jnp.int32, v.shape, 0) // SUBLANES
    jj = lax.broadcasted_iota(jnp.int32, (SUBLANES, v.shape[1]), 0)

    def group_body(kg, cnt):
        base = pl.multiple_of(kg * SUBLANES, SUBLANES)
        thr = jnp.where(jg > kg, key_m1, key)
        diag = key_sc[pl.ds(base, SUBLANES), :]
        corr = jnp.zeros(diag.shape, jnp.int32)
        for kk in range(SUBLANES):
            k_row = key_sc[pl.ds(base + kk, 1), :]
            cnt = cnt + jnp.where(jnp.broadcast_to(k_row, v.shape) > thr, 1, 0)
            corr = corr + jnp.where(jnp.broadcast_to(k_row, diag.shape) == diag, jnp.where(jj > kk, 1, 0), 0)
        corr_sc[pl.ds(base, SUBLANES), :] = corr
        return cnt

    cnt = lax.fori_loop(0, (n_rows + SUBLANES - 1) // SUBLANES, group_body, jnp.zeros(v.shape, jnp.int32))
    return jnp.where(cnt + corr_sc[...] < N_SELECT, 1.0, 0.0)


def _nsa_prompt_kernel(thr_ref, q_ref, gate_ref, kct_ref, vc_ref, kst_ref, vsa_ref, kwt_ref, vw_ref, tb_ref, tab_ref,
                       selmap_ref, selexp_ref, eg_ref, o_ref, bucket_sc, key_sc, corr_sc, m_sc, acc_sc, s0_sc, s1_sc, o_sc):
    qb = pl.program_id(1)
    t0 = qb * Q_BLOCK
    gw = N_KV_GROUPS * HEAD_DIM
    n_cmp = kct_ref.shape[2]
    rows = GROUP_SIZE * Q_BLOCK

    qi = lax.broadcasted_iota(jnp.int32, (Q_BLOCK, n_cmp), 0)
    hc = lax.broadcasted_iota(jnp.int32, (Q_BLOCK, n_cmp), 1)
    dist_c = jnp.where(hc >= 1, t0 + qi - (CMP_STRIDE * hc + CMP_STRIDE - 1), -1)
    mask_c = dist_c >= 0
    bucket = jnp.zeros((Q_BLOCK, n_cmp), jnp.int32)
    for m in range(1, NUM_BUCKETS):
        bucket = bucket + jnp.where(dist_c >= thr_ref[m], 1, 0)
    bucket_sc[...] = bucket

    gexp = jnp.dot(gate_ref[0], eg_ref[...], preferred_element_type=F32, precision=lax.Precision.HIGHEST)
    lane_g = lax.broadcasted_iota(jnp.int32, (Q_BLOCK, gw), 1) // HEAD_DIM
    o_sc[...] = jnp.zeros_like(o_sc)

    nb = selmap_ref.shape[0]
    jb = lax.broadcasted_iota(jnp.int32, (nb, Q_BLOCK), 0)
    qpos_t = t0 + lax.broadcasted_iota(jnp.int32, (nb, Q_BLOCK), 1)
    cur = qpos_t // SEL_BLOCK
    forced = jnp.where(jb == 0, 1, 0) + jnp.where(jb == cur, 1, 0) + jnp.where(jb == cur - 1, 1, 0)
    valid_t = jb * SEL_BLOCK <= qpos_t
    n_valid_blocks = (t0 + Q_BLOCK - 1) // SEL_BLOCK + 1
    n_chunks = (t0 + Q_BLOCK - 1) // SEL_CHUNK + 1

    def group_body(g, carry):
        qm = jnp.concatenate(
            [jnp.where(lane_g == g, q_ref[0, :, r * gw:(r + 1) * gw], 0.0).astype(BF16) for r in range(GROUP_SIZE)], axis=0)

        s_c = _dot(qm, kct_ref[0])
        p_rows = []
        for r in range(GROUP_SIZE):
            tab = jnp.broadcast_to(tab_ref[pl.ds(g * GROUP_SIZE + r, 1), :], (Q_BLOCK, LANES))
            bias = jnp.concatenate(
                [jnp.take_along_axis(tab, bucket_sc[:, c * LANES:(c + 1) * LANES], axis=1) for c in range(n_cmp // LANES)], axis=1)
            p_rows.append(_masked_softmax(s_c[r * Q_BLOCK:(r + 1) * Q_BLOCK] + bias, mask_c).astype(BF16))
        p_c = jnp.concatenate(p_rows, axis=0)
        o_cmp = _dot(p_c, vc_ref[0])

        imp = jnp.zeros((nb, Q_BLOCK), F32)
        for r in range(GROUP_SIZE):
            imp = imp + lax.dot_general(selmap_ref[...], p_rows[r], (((1,), (1,)), ((), ())), preferred_element_type=F32)
        imp = jnp.where(forced > 0, 1e9, jnp.where(valid_t, imp, -1e9))
        sel = _topk_mask_t(imp, key_sc, corr_sc, n_valid_blocks).T.astype(BF16)

        m_sc[...] = jnp.full_like(m_sc, NEG_INF)
        acc_sc[...] = jnp.zeros_like(acc_sc)
        ones_at = jnp.where(g == ONES_GROUP[0], 1, 0)
        v_col = pl.multiple_of(ones_at * gw, gw)

        def scores(c):
            return _dot_split_rows(qm, kst_ref[0, :, pl.ds(pl.multiple_of(c * SEL_CHUNK, SEL_CHUNK), SEL_CHUNK)])

        def sweep_step(c, s_cur, s_next):
            off = pl.multiple_of(c * SEL_CHUNK, SEL_CHUNK)
            picked = _dot(sel, selexp_ref[:, pl.ds(off, SEL_CHUNK)]) > 0.5
            s_next[...] = scores(jnp.minimum(c + 1, n_chunks - 1))
            parts = []
            for r in range(GROUP_SIZE):
                bias = jnp.concatenate(
                    [tb_ref[jnp.clip(qb - (c * (SEL_CHUNK // KEY_TILE) + h), -1, N_NEAR_TILES) + 1, g * GROUP_SIZE + r]
                     for h in range(SEL_CHUNK // KEY_TILE)], axis=1)
                parts.append(jnp.where(picked, s_cur[r * Q_BLOCK:(r + 1) * Q_BLOCK, :] + bias, NEG_INF))
            s = jnp.concatenate(parts, axis=0)
            m_old = m_sc[...]
            m_new = jnp.maximum(m_old, jnp.max(s, axis=-1, keepdims=True))
            alpha = jnp.exp(m_old - m_new)
            p = jnp.exp(s - jnp.concatenate([m_new] * (SEL_CHUNK // LANES), axis=1))
            pv = _dot_split_rows(p.astype(BF16), vsa_ref[0, pl.ds(off, SEL_CHUNK), pl.ds(v_col, gw)])
            acc_sc[...] = jnp.concatenate([alpha] * (gw // LANES), axis=1) * acc_sc[...] + pv
            m_sc[...] = m_new

        s0_sc[...] = scores(0)

        def pair_body(pair, carry2):
            sweep_step(2 * pair, s0_sc, s1_sc)

            @pl.when(2 * pair + 1 < n_chunks)
            def _():
                sweep_step(2 * pair + 1, s1_sc, s0_sc)

            return carry2

        lax.fori_loop(0, (n_chunks + 1) // 2, pair_body, 0)
        acc = acc_sc[...]
        denom = jnp.where(ones_at == 1, acc[:, ONES_GROUP[1] * HEAD_DIM:ONES_GROUP[1] * HEAD_DIM + 1],
                          acc[:, ONES_GROUP[0] * HEAD_DIM:ONES_GROUP[0] * HEAD_DIM + 1])
        o_sel = acc / denom

        n_wt = WIN_KEYS // KEY_TILE
        w_off = [pl.multiple_of(jnp.maximum(qb - (n_wt - 1) + w, 0) * KEY_TILE, KEY_TILE) for w in range(n_wt)]
        w_tile = [jnp.where(qb - (n_wt - 1) + w >= 0, TILE_WINDOW if w == 0 else n_wt - w, TILE_FUTURE)
                  for w in range(n_wt)]
        s_w = _dot(qm, jnp.concatenate([kwt_ref[0, :, pl.ds(o, KEY_TILE)] for o in w_off], axis=1))
        pw_rows = []
        for r in range(GROUP_SIZE):
            bias = jnp.concatenate([tb_ref[ti, g * GROUP_SIZE + r] for ti in w_tile], axis=1)
            s_r = s_w[r * Q_BLOCK:(r + 1) * Q_BLOCK] + bias
            e = jnp.exp(s_r - jnp.max(s_r, axis=-1, keepdims=True))
            pw_rows.append((e / jnp.sum(e, axis=-1, keepdims=True)).astype(BF16))
        o_win = _dot(jnp.concatenate(pw_rows, axis=0),
                     jnp.concatenate([vw_ref[0, pl.ds(o, KEY_TILE), :] for o in w_off], axis=0))

        for r in range(GROUP_SIZE):
            rs = slice(r * Q_BLOCK, (r + 1) * Q_BLOCK)
            mix = (gexp[:, (3 * r) * gw:(3 * r + 1) * gw] * o_cmp[rs]
                   + gexp[:, (3 * r + 1) * gw:(3 * r + 2) * gw] * o_sel[rs]
                   + gexp[:, (3 * r + 2) * gw:(3 * r + 3) * gw] * o_win[rs])
            o_sc[:, r * gw:(r + 1) * gw] += jnp.where(lane_g == g, mix, 0.0)
        return carry

    lax.fori_loop(0, N_KV_GROUPS, group_body, 0)
    o_ref[0] = o_sc[...].astype(BF16)


def _nsa_prompt(q, gates, kct, vc, kst, vsa, kwt, selwin, tiles, tab, selmap, selexp, eg, thr):
    n, t, hw = q.shape
    gw = N_KV_GROUPS * HEAD_DIM
    n_cmp = kct.shape[2]
    rows = GROUP_SIZE * Q_BLOCK
    per_n = lambda shape, col=0: pl.BlockSpec(shape, lambda b, j: (b, 0, col), pipeline_mode=pl.Buffered(1))
    blk = lambda w: pl.BlockSpec((1, Q_BLOCK, w), lambda b, j: (b, j, 0))
    return pl.pallas_call(
        _nsa_prompt_kernel,
        grid=(n, t // Q_BLOCK),
        in_specs=[pl.BlockSpec(memory_space=pltpu.SMEM), blk(hw), blk(gates.shape[2]),
                  per_n((1, gw, n_cmp)), per_n((1, n_cmp, gw)), per_n((1, gw, t)), per_n((1, t, 2 * gw)),
                  per_n((1, gw, t)), per_n((1, t, gw), 3),
                  _const_spec(tiles.shape), _const_spec(tab.shape), _const_spec(selmap.shape), _const_spec(selexp.shape),
                  _const_spec(eg.shape)],
        out_specs=blk(hw),
        out_shape=jax.ShapeDtypeStruct((n, t, hw), BF16),
        scratch_shapes=[pltpu.VMEM((Q_BLOCK, n_cmp), jnp.int32), pltpu.VMEM((selmap.shape[0], Q_BLOCK), jnp.int32),
                        pltpu.VMEM((selmap.shape[0], Q_BLOCK), jnp.int32),
                        pltpu.VMEM((rows, LANES), F32), pltpu.VMEM((rows, gw), F32),
                        pltpu.VMEM((rows, SEL_CHUNK), F32), pltpu.VMEM((rows, SEL_CHUNK), F32),
                        pltpu.VMEM((Q_BLOCK, hw), F32)],
        compiler_params=_cparams("arbitrary", "arbitrary"),
        name="nsa_prompt",
    )(thr, q, gates, kct, vc, kst, vsa, kwt, selwin, tiles, tab, selmap, selexp, eg)


DEC_T_PAD = SUBLANES
SAMPLE_BLOCKS_PAD = 256
NEW_KEYS_PAD = LANES


def _nsa_sample_kernel(*refs, n_pages, n_blocks):
    pt_ref, q_ref, gate_ref, kct_ref, vc_ref = refs[:5]
    page_refs = refs[5:5 + n_pages]
    (bsel_ref, bcmp_ref, bwin_ref, bnew_ref, new_ref, cwin_ref, selmap_ref, eg_ref, o_ref,
     key_sc, corr_sc, sel_sc, m_sc, l_sc, acc_sc, oc_sc) = refs[5 + n_pages:]
    c = pl.program_id(1)
    gw = N_KV_GROUPS * HEAD_DIM
    rows = N_HEADS * DEC_T_PAD
    page = page_refs[0].shape[2]
    lane_g = lax.broadcasted_iota(jnp.int32, (DEC_T_PAD, gw), 1) // HEAD_DIM
    qm = jnp.concatenate(
        [jnp.where(lane_g == g, q_ref[0, :, r * gw:(r + 1) * gw], 0.0)
         for r in range(GROUP_SIZE) for g in range(N_KV_GROUPS)], axis=0).astype(BF16)
    nt = (((1,), (1,)), ((), ()))

    @pl.when(c == 0)
    def _():
        p_c = _masked_softmax(_dot(qm, kct_ref[0]) + bcmp_ref[...], bcmp_ref[...] > 0.5 * NEG_INF).astype(BF16)
        oc_sc[...] = _dot(p_c, vc_ref[0])
        imp_all = lax.dot_general(selmap_ref[...], p_c, nt, preferred_element_type=F32)
        per_r = rows // GROUP_SIZE
        imp = imp_all
        for r in range(1, GROUP_SIZE):
            imp = imp + pltpu.roll(imp_all, r * per_r, 1)
        jb = lax.broadcasted_iota(jnp.int32, imp.shape, 0)
        last = n_blocks - 1
        forced = jnp.where(jb == 0, 1, 0) + jnp.where(jb == last, 1, 0) + jnp.where(jb == last - 1, 1, 0)
        imp = jnp.where(forced > 0, 1e9, jnp.where(jb <= last, imp, -1e9))
        sel_sc[...] = _topk_mask_t(imp, key_sc, corr_sc, n_blocks).T.astype(BF16)
        m_sc[...] = jnp.full_like(m_sc, NEG_INF)
        l_sc[...] = jnp.zeros_like(l_sc)
        acc_sc[...] = jnp.zeros_like(acc_sc)

    def online_update(s, pv_of):
        m_old = m_sc[...]
        m_new = jnp.maximum(m_old, jnp.max(s, axis=-1, keepdims=True))
        alpha = jnp.exp(m_old - m_new)
        p = jnp.exp(s - m_new)
        l_sc[...] = alpha * l_sc[...] + jnp.sum(p, axis=-1, keepdims=True)
        acc_sc[...] = alpha * acc_sc[...] + pv_of(p.astype(BF16))
        m_sc[...] = m_new

    keys = n_pages * page
    s = jnp.concatenate([_dot(qm, pr[0, :gw, :].astype(BF16)) for pr in page_refs], axis=1)
    eb = lax.broadcasted_iota(jnp.int32, (SAMPLE_BLOCKS_PAD, keys), 0)
    ek = lax.broadcasted_iota(jnp.int32, (SAMPLE_BLOCKS_PAD, keys), 1)
    expand = jnp.where(eb == c * (keys // SEL_BLOCK) + ek // SEL_BLOCK, 1.0, 0.0).astype(BF16)
    picked = _dot(sel_sc[...], expand) > 0.5
    v_t = jnp.concatenate([pr[0, gw:, :].astype(BF16) for pr in page_refs], axis=1)
    online_update(jnp.where(picked, s + bsel_ref[...], NEG_INF),
                  lambda p: lax.dot_general(p, v_t, nt, preferred_element_type=F32))

    @pl.when(c == pl.num_programs(1) - 1)
    def _():
        s_new = lax.dot_general(qm, new_ref[0, :, :gw], nt, preferred_element_type=F32) + bnew_ref[0]
        online_update(s_new, lambda p: _dot(p, new_ref[0, :, gw:2 * gw]))
        o_sel = acc_sc[...] / l_sc[...]

        s_w = _dot(qm, cwin_ref[0, :gw, :].astype(BF16)) + bwin_ref[...]
        s_wn = lax.dot_general(qm, new_ref[0, :, 2 * gw:3 * gw], nt, preferred_element_type=F32) + bnew_ref[1]
        m_w = jnp.maximum(jnp.max(s_w, axis=-1, keepdims=True), jnp.max(s_wn, axis=-1, keepdims=True))
        e_w = jnp.exp(s_w - m_w)
        e_wn = jnp.exp(s_wn - m_w)
        l_w = jnp.sum(e_w, axis=-1, keepdims=True) + jnp.sum(e_wn, axis=-1, keepdims=True)
        o_win = lax.dot_general((e_w / l_w).astype(BF16), cwin_ref[0, gw:, :].astype(BF16), nt,
                                preferred_element_type=F32) + _dot((e_wn / l_w).astype(BF16), new_ref[0, :, 3 * gw:])

        gexp = jnp.dot(gate_ref[0], eg_ref[...], preferred_element_type=F32, precision=lax.Precision.HIGHEST)
        o_cmp = oc_sc[...]
        for r in range(GROUP_SIZE):
            out_r = jnp.zeros((DEC_T_PAD, gw), F32)
            for g in range(N_KV_GROUPS):
                rs = slice((r * N_KV_GROUPS + g) * DEC_T_PAD, (r * N_KV_GROUPS + g + 1) * DEC_T_PAD)
                mix = (gexp[:, (3 * r) * gw:(3 * r + 1) * gw] * o_cmp[rs]
                       + gexp[:, (3 * r + 1) * gw:(3 * r + 2) * gw] * o_sel[rs]
                       + gexp[:, (3 * r + 2) * gw:(3 * r + 3) * gw] * o_win[rs])
                out_r = out_r + jnp.where(lane_g == g, mix, 0.0)
            o_ref[0, :, r * gw:(r + 1) * gw] = out_r


PAGES_PER_STEP = 8


def _nsa_sample(page_table, q, gates, kct, vc, cache, bias, new_rows, cache_win, selmap, eg, n_blocks):
    n, _, hw = q.shape
    gw = N_KV_GROUPS * HEAD_DIM
    n_cmp = kct.shape[2]
    rows = N_HEADS * DEC_T_PAD
    page = cache.shape[2]
    steps = page_table.shape[1] // PAGES_PER_STEP
    keys = PAGES_PER_STEP * page
    per_n = lambda a: pl.BlockSpec((1,) + a.shape[1:], lambda b, c, pt: (b, 0, 0))
    cidx = lambda nd: (lambda *_: (0,) * nd)
    const = lambda a: pl.BlockSpec(a.shape, cidx(a.ndim), pipeline_mode=pl.Buffered(1))
    page_specs = [pl.BlockSpec((1, 2 * gw, page), functools.partial(
        lambda b, c, pt, p: (pt[b, c * PAGES_PER_STEP + p], 1, 0), p=p)) for p in range(PAGES_PER_STEP)]
    grid_spec = pltpu.PrefetchScalarGridSpec(
        num_scalar_prefetch=1,
        grid=(n, steps),
        in_specs=[per_n(q), per_n(gates), per_n(kct), per_n(vc)] + page_specs + [
            pl.BlockSpec((rows, keys), lambda b, c, pt: (0, c)), const(bias['cmp']), const(bias['win']), const(bias['new']),
            per_n(new_rows), per_n(cache_win), const(selmap), const(eg)],
        out_specs=pl.BlockSpec((1, DEC_T_PAD, hw), lambda b, c, pt: (b, 0, 0)),
        scratch_shapes=[pltpu.VMEM((SAMPLE_BLOCKS_PAD, rows), jnp.int32), pltpu.VMEM((SAMPLE_BLOCKS_PAD, rows), jnp.int32),
                        pltpu.VMEM((rows, SAMPLE_BLOCKS_PAD), BF16),
                        pltpu.VMEM((rows, 1), F32), pltpu.VMEM((rows, 1), F32), pltpu.VMEM((rows, gw), F32),
                        pltpu.VMEM((rows, gw), F32)],
    )
    return pl.pallas_call(
        functools.partial(_nsa_sample_kernel, n_pages=PAGES_PER_STEP, n_blocks=n_blocks),
        grid_spec=grid_spec,
        out_shape=jax.ShapeDtypeStruct((n, DEC_T_PAD, hw), F32),
        compiler_params=_cparams("arbitrary", "arbitrary"),
        name="nsa_sample",
    )(page_table, q, gates, kct, vc, *([cache] * PAGES_PER_STEP), bias['sel'], bias['cmp'], bias['win'], bias['new'],
      new_rows, cache_win, selmap, eg)


def _sample_bias(table, past, dec_t, win_len, n_cmp_rows):
    t = jnp.arange(DEC_T_PAD)[:, None]

    def rows(dist, ok):
        b = jnp.where(ok, _bias_lookup(table, dist), NEG_INF)
        b = b.reshape(N_KV_GROUPS, GROUP_SIZE, DEC_T_PAD, -1).transpose(1, 0, 2, 3)
        return b.reshape(N_HEADS * DEC_T_PAD, -1)

    dist_s = past + t - jnp.arange(past)[None, :]
    h = jnp.arange(n_cmp_rows)[None, :]
    dist_c = past + t - (CMP_STRIDE * h + CMP_STRIDE - 1)
    j = jnp.arange(win_len)[None, :]
    dist_w = win_len + t - j
    tn = jnp.arange(NEW_KEYS_PAD)[None, :]
    new = rows(t - tn, (tn <= t) & (tn < dec_t))
    return dict(sel=rows(dist_s, dist_s >= 0), cmp=rows(dist_c, (h >= 1) & (dist_c >= 0)),
                win=rows(dist_w, (dist_w < WINDOW) & (past - win_len + j >= 0)), new=jnp.stack([new, new]))


GATE_LANES = LANES


def _prepare(p):
    d = p['norm_pre_mix'].shape[1]
    hw = N_HEADS * HEAD_DIM
    gw = N_KV_GROUPS * HEAD_DIM
    row = lambda v: v.reshape(1, -1)
    w = {}
    w['rg'] = dict(
        g_pre=row(p['norm_pre_mix'][0]), w_in=p['rg_w_in'][0].astype(BF16), conv_w=p['rg_conv_w'][0],
        conv_b=row(p['rg_conv_b'][0]),
        w_ax=jnp.concatenate([p['rg_w_a'][0], p['rg_w_x'][0]], axis=2).astype(BF16),
        b_ax=jnp.stack([p['rg_b_a'][0], p['rg_b_x'][0]]), lam=row(p['rg_lambda'][0]),
        w_out=p['rg_w_out'][0].astype(BF16), g_post=row(p['norm_post_mix'][0]))
    w['mlp'] = [dict(g1=row(p['norm_pre_mlp'][l]), wup=p['w_mlp_up'][l].astype(BF16),
                     wdn=p['w_mlp_down'][l].astype(BF16), g2=row(p['norm_post_mlp'][l])) for l in range(2)]
    wqg = p['nsa_w_qg'][0]
    n_gate = wqg.shape[1] - hw
    w['proj'] = dict(
        gkv=row(p['kv_norm']), wkv=p['w_kv'].astype(BF16), gq=row(p['norm_pre_mix'][1]),
        wq=wqg[:, :hw].reshape(d, N_KV_GROUPS, GROUP_SIZE, HEAD_DIM).transpose(0, 2, 1, 3).reshape(d, hw).astype(BF16),
        wg=jnp.pad(wqg[:, hw:], ((0, 0), (0, GATE_LANES - n_gate))).astype(BF16),
        bg=jnp.pad(p['nsa_b_g'][0], (0, GATE_LANES - n_gate)).reshape(1, -1))
    w['wo'] = p['nsa_w_o'][0].reshape(N_KV_GROUPS, GROUP_SIZE, HEAD_DIM, d).transpose(1, 0, 2, 3).reshape(hw, d).astype(BF16)
    w['g_post_mix1'] = row(p['norm_post_mix'][1])
    eye2 = jnp.eye(2, dtype=F32)
    half = L_CMP // CMP_STRIDE
    w['cmp'] = dict(
        pe=jnp.broadcast_to(p['cmp_pe'].reshape(2, half, CMP_STRIDE, 1, HEAD_DIM),
                            (2, half, CMP_STRIDE, 2, HEAD_DIM)).reshape(2, half, 1, CMP_STRIDE * CMP_GROUP_PAIR),
        w1=jnp.einsum('sftdc,gh->sftgdhc', p['cmp_w1'].reshape(2, half, CMP_STRIDE, HEAD_DIM, -1), eye2)
        .reshape(2, half, CMP_STRIDE * CMP_GROUP_PAIR, -1).astype(BF16),
        b1=jnp.tile(p['cmp_b1'], (1, 2)).reshape(2, 1, -1),
        w2=jnp.einsum('scd,gh->sgchd', p['cmp_w2'], eye2).reshape(2, -1, CMP_GROUP_PAIR).astype(BF16))
    w['table'] = p['rel_bias_table']
    w['tiles'] = _toeplitz_tiles(p['rel_bias_table'])
    w['tab'] = jnp.pad(p['rel_bias_table'].T.astype(F32), ((0, 0), (0, LANES - NUM_BUCKETS)))
    w['thr'] = _bucket_thresholds()
    h_i = jnp.arange(N_HEADS)
    src = (h_i[:, None] * 3 + jnp.arange(3)[None, :]).reshape(-1)
    dst0 = (((h_i % GROUP_SIZE)[:, None] * 3 + jnp.arange(3)[None, :]) * gw
            + (h_i // GROUP_SIZE)[:, None] * HEAD_DIM).reshape(-1)
    lanes = jnp.arange(3 * GROUP_SIZE * gw)
    w['eg'] = jnp.zeros((GATE_LANES, lanes.shape[0]), F32).at[src].set(
        ((lanes[None, :] >= dst0[:, None]) & (lanes[None, :] < dst0[:, None] + HEAD_DIM)).astype(F32))
    return w


def _selmap_t(n_blocks, n_cmp_rows):
    ratio = SEL_BLOCK // CMP_STRIDE
    h = jnp.arange(n_cmp_rows)[None, :]
    j = jnp.arange(n_blocks)[:, None]
    m = ((h - 1) // ratio == j).astype(F32) + (h // ratio == j).astype(F32)
    return jnp.where(h >= 1, m, 0.0).astype(BF16)


def _trunk_prompt(x, w):
    n, t, d = x.shape
    gw = N_KV_GROUPS * HEAD_DIM
    zeros = lambda r: jnp.zeros((n, r, d), F32)
    x1, h_last, c_last = _rglru_layer(x, zeros(1), zeros(CONV_W - 1), w['rg'], stride=1, tm=256)
    m0 = w['mlp'][0]
    x2 = _mlp_layer(x1.reshape(n * t, d), m0['g1'], m0['wup'], m0['wdn'], m0['g2'], tm=512)
    pj = w['proj']
    kv4, _, selwin, q, gates, kst, kwt, vsa, kv_t, win_t = _proj_layer(
        x2, pj['gkv'], pj['wkv'], pj['gq'], pj['wq'], pj['wg'], pj['bg'], tm=512, seq_len=t)
    nh = t // CMP_STRIDE
    n_blocks = t // SEL_BLOCK
    kc, vc = _compress(kv_t, w['cmp'], n_seq=n, seq_pages=t // KEY_TILE, page=KEY_TILE)
    selexp = (jnp.arange(t)[None, :] // SEL_BLOCK == jnp.arange(n_blocks)[:, None]).astype(BF16)
    attn = _nsa_prompt(q.reshape(n, t, -1), gates.reshape(n, t, -1), kc.transpose(0, 2, 1), vc, kst, vsa.reshape(n, t, -1),
                       kwt, selwin.reshape(n, t, -1), w['tiles'], w['tab'], _selmap_t(n_blocks, nh), selexp, w['eg'], w['thr'])
    m1 = w['mlp'][1]
    y = _mlp_layer(x2, m1['g1'], m1['wup'], m1['wdn'], m1['g2'], tm=512,
                   mix=(attn.reshape(n * t, -1), w['wo'], w['g_post_mix1']))
    wlen = min(WINDOW, t)
    rows_of = lambda a, sets: a.reshape(n, sets, N_KV_GROUPS, HEAD_DIM, -1).transpose(0, 4, 1, 2, 3)
    return (y.reshape(n, t, d), h_last.reshape(1, n, d), c_last.reshape(1, n, CONV_W - 1, d),
            rows_of(kv_t, 4), rows_of(win_t[:, :, t - wlen:], 2))


def _trunk_sample(x, h0, c0, cache_kv, cache_win, page_table, w):
    n, t, d = x.shape
    n_phys, page = cache_kv.shape[:2]
    past = page_table.shape[1] * page
    win_len = cache_win.shape[1]
    assert t < CMP_STRIDE and past % CMP_STRIDE == 0 and t <= DEC_T_PAD and page % CMP_STRIDE == 0
    assert page_table.shape[1] % PAGES_PER_STEP == 0 and page == KEY_TILE
    to_tn = lambda a: a.transpose(1, 0, 2).reshape(a.shape[0] * a.shape[1], -1)
    to_nt = lambda a, k: a.reshape(k, n, -1).transpose(1, 0, 2)
    tail = CONV_W - 1
    x1, h_last, c_last = _rglru_layer(to_tn(x)[None], h0[None], to_tn(c0)[None], w['rg'], stride=n, tm=t * n)
    m0 = w['mlp'][0]
    x2 = _mlp_layer(x1[0], m0['g1'], m0['wup'], m0['wdn'], m0['g2'], tm=t * n)
    pj = w['proj']
    kv4, win, selwin, q, gates = _proj_layer(x2, pj['gkv'], pj['wkv'], pj['gq'], pj['wq'], pj['wg'], pj['bg'], tm=t * n)

    row_w = cache_kv.shape[2] * cache_kv.shape[3] * cache_kv.shape[4]
    nh = past // CMP_STRIDE
    cache_t = cache_kv.reshape(n_phys, page, row_w).transpose(0, 2, 1)
    cache_win_t = cache_win.reshape(n, win_len, -1).transpose(0, 2, 1)
    kc, vc = _compress(cache_t, w['cmp'], n_seq=n, seq_pages=page_table.shape[1], page=page, page_table=page_table)
    pad_t = lambda a, k: jnp.pad(a, ((0, 0), (0, k - a.shape[1]), (0, 0)))
    n_blocks = -(-(past + t) // SEL_BLOCK)
    attn = _nsa_sample(
        page_table, pad_t(to_nt(q, t).astype(F32), DEC_T_PAD), pad_t(to_nt(gates, t), DEC_T_PAD), kc.transpose(0, 2, 1), vc,
        cache_t, _sample_bias(w['table'], past, t, win_len, nh),
        pad_t(to_nt(selwin, t), NEW_KEYS_PAD), cache_win_t, _selmap_t(SAMPLE_BLOCKS_PAD, nh), w['eg'], n_blocks)
    m1 = w['mlp'][1]
    y = _mlp_layer(x2, m1['g1'], m1['wup'], m1['wdn'], m1['g2'], tm=t * n,
                   mix=(to_tn(attn[:, :t]).astype(BF16), w['wo'], w['g_post_mix1']))
    win_rows = to_nt(win, t).reshape(n, t, 2, N_KV_GROUPS, HEAD_DIM)
    win_new = jnp.concatenate([cache_win, win_rows], axis=1)[:, t:]
    return (to_nt(y, t), h_last, to_nt(c_last[0], tail)[None],
            to_nt(kv4, t).reshape(n, t, 4, N_KV_GROUPS, HEAD_DIM), win_new)


def kernel(x_prompt, x_sample, state_rglru_h, state_rglru_conv, cache_kv, cache_win, page_table, norm_pre_mix,
           norm_post_mix, norm_pre_mlp, norm_post_mlp, w_mlp_up, w_mlp_down, rg_w_in, rg_conv_w, rg_conv_b, rg_w_a, rg_b_a,
           rg_w_x, rg_b_x, rg_lambda, rg_w_out, kv_norm, w_kv, cmp_pe, cmp_w1, cmp_b1, cmp_w2, nsa_w_qg, nsa_b_g, nsa_w_o,
           rel_bias_table):
    w = _prepare(dict(
        norm_pre_mix=norm_pre_mix, norm_post_mix=norm_post_mix, norm_pre_mlp=norm_pre_mlp, norm_post_mlp=norm_post_mlp,
        w_mlp_up=w_mlp_up, w_mlp_down=w_mlp_down, rg_w_in=rg_w_in, rg_conv_w=rg_conv_w, rg_conv_b=rg_conv_b,
        rg_w_a=rg_w_a, rg_b_a=rg_b_a, rg_w_x=rg_w_x, rg_b_x=rg_b_x, rg_lambda=rg_lambda, rg_w_out=rg_w_out,
        kv_norm=kv_norm, w_kv=w_kv, cmp_pe=cmp_pe, cmp_w1=cmp_w1, cmp_b1=cmp_b1, cmp_w2=cmp_w2, nsa_w_qg=nsa_w_qg,
        nsa_b_g=nsa_b_g, nsa_w_o=nsa_w_o, rel_bias_table=rel_bias_table))
    y_p, p_h, p_conv, p_kv, p_win = _trunk_prompt(x_prompt, w)
    y_s, s_h, s_conv, s_kv, s_win = _trunk_sample(x_sample, state_rglru_h[0], state_rglru_conv[0], cache_kv, cache_win,
                                                  page_table, w)
    return (y_p, y_s, p_h, p_conv, p_kv, p_win, s_h, s_conv, s_kv, s_win)
```

```python
import functools
import math

import jax
import jax.numpy as jnp
from jax import lax
from jax.experimental import pallas as pl
from jax.experimental.pallas import tpu as pltpu

F32 = jnp.float32
BF16 = jnp.bfloat16

N_HEADS = 16
N_KV_GROUPS = 4
GROUP_SIZE = N_HEADS // N_KV_GROUPS
HEAD_DIM = 64
N_RNN_BLOCKS = 4
CONV_W = 4
LRU_C = 8.0
L_CMP = 32
CMP_STRIDE = 16
SEL_BLOCK = 64
N_SELECT = 16
WINDOW = 512
Q_BLOCK = 128
NUM_BUCKETS = 32
REL_MAX_DIST = 1024
EPS = 1e-6
NEG_INF = -1e30

SUBLANES = 8
LANES = 128
VMEM_LIMIT_BYTES = 56 * 1024 * 1024


def _cparams(*sem):
    return pltpu.CompilerParams(dimension_semantics=sem, vmem_limit_bytes=VMEM_LIMIT_BYTES)


def _const_spec(shape):
    nd = len(shape)
    return pl.BlockSpec(shape, lambda *_: (0,) * nd, pipeline_mode=pl.Buffered(1))


def _rms(x, g):
    return x * lax.rsqrt(jnp.mean(x * x, axis=-1, keepdims=True) + EPS) * g


def _dot(a, b):
    return jnp.dot(a, b, preferred_element_type=F32)


def _dot_split_rows(a, b):
    half = a.shape[0] // 2
    return jnp.concatenate([_dot(a[:half], b), _dot(a[half:], b)], axis=0)


def _rglru_kernel(x_ref, h0_ref, c0_ref, gpre_ref, win_ref, cw_ref, cb_ref, wax_ref, bax_ref, lam_ref,
                  wout_ref, gpost_ref, x1_ref, hlast_ref, clast_ref, xcat_sc, hprev_sc, *, stride, tm, pad):
    d = x_ref.shape[-1]
    tail = (CONV_W - 1) * stride
    j = pl.program_id(1)

    @pl.when(j == 0)
    def _():
        xcat_sc[pad - tail:pad, :] = c0_ref[0]
        hprev_sc[...] = jnp.zeros_like(hprev_sc)
        hprev_sc[tm - stride:tm, :] = h0_ref[0]

    x = x_ref[0]
    u = _rms(x, gpre_ref[...]).astype(BF16)
    proj = _dot(u, win_ref[...])
    gate = jax.nn.gelu(proj[:, :d])
    xcat_sc[pad:pad + tm, :] = proj[:, d:]
    xc = cb_ref[...] + cw_ref[CONV_W - 1:CONV_W, :] * xcat_sc[pad:pad + tm, :]
    for lag in range(1, CONV_W):
        xc = xc + cw_ref[CONV_W - 1 - lag:CONV_W - lag, :] * xcat_sc[pad - lag * stride:pad - lag * stride + tm, :]
    xcat_sc[pad - tail:pad, :] = xcat_sc[pad + tm - tail:pad + tm, :]
    clast_ref[0] = xcat_sc[pad - tail:pad, :]

    xcb = xc.astype(BF16)
    bw = d // N_RNN_BLOCKS
    ra, rx = [], []
    for blk in range(N_RNN_BLOCKS):
        pre = _dot(xcb[:, blk * bw:(blk + 1) * bw], wax_ref[blk])
        ra.append(pre[:, :bw])
        rx.append(pre[:, bw:])
    r = jax.nn.sigmoid(jnp.concatenate(ra, axis=1) + bax_ref[0:1, :])
    i = jax.nn.sigmoid(jnp.concatenate(rx, axis=1) + bax_ref[1:2, :])
    log_a = r * (-LRU_C * jax.nn.softplus(-lam_ref[...]))
    a = jnp.exp(log_a)
    b = jnp.sqrt(-jnp.tanh(log_a) * (a * a + 1.0)) * i * xc

    row = lax.broadcasted_iota(jnp.int32, (tm, d), 0)
    b = b + jnp.where(row < stride, a * pltpu.roll(hprev_sc[...], stride, 0), 0.0)
    s = stride
    while s < tm:
        keep = row >= s
        b = a * jnp.where(keep, pltpu.roll(b, s, 0), 0.0) + b
        a = a * jnp.where(keep, pltpu.roll(a, s, 0), 1.0)
        s *= 2
    h = b
    hprev_sc[...] = h
    hlast_ref[0] = h[tm - stride:tm, :]

    y = _dot((h * gate).astype(BF16), wout_ref[...])
    x1_ref[0] = x + _rms(y, gpost_ref[...])


def _rglru_layer(x, h0, c0, w, *, stride, tm):
    n, rows, d = x.shape
    tail = (CONV_W - 1) * stride
    pad = -(-tail // SUBLANES) * SUBLANES
    kern = functools.partial(_rglru_kernel, stride=stride, tm=tm, pad=pad)
    seq = lambda shape: pl.BlockSpec(shape, lambda b, j: (b, j, 0))
    per_n = lambda shape: pl.BlockSpec(shape, lambda b, j: (b, 0, 0))
    return pl.pallas_call(
        kern,
        grid=(n, rows // tm),
        in_specs=[seq((1, tm, d)), per_n((1, stride, d)), per_n((1, tail, d)),
                  _const_spec((1, d)), _const_spec(w['w_in'].shape), _const_spec((CONV_W, d)), _const_spec((1, d)),
                  _const_spec(w['w_ax'].shape), _const_spec((2, d)), _const_spec((1, d)),
                  _const_spec(w['w_out'].shape), _const_spec((1, d))],
        out_specs=[seq((1, tm, d)), per_n((1, stride, d)), per_n((1, tail, d))],
        out_shape=[jax.ShapeDtypeStruct((n, rows, d), F32), jax.ShapeDtypeStruct((n, stride, d), F32),
                   jax.ShapeDtypeStruct((n, tail, d), F32)],
        scratch_shapes=[pltpu.VMEM((pad + tm, d), F32), pltpu.VMEM((tm, d), F32)],
        compiler_params=_cparams("arbitrary", "arbitrary"),
        name="rglru_layer",
    )(x, h0, c0, w['g_pre'], w['w_in'], w['conv_w'], w['conv_b'], w['w_ax'], w['b_ax'], w['lam'], w['w_out'], w['g_post'])


MLP_HIDDEN_CHUNK = 1024


def _mlp_body(x, g1_ref, wup_ref, wdn_ref, g2_ref):
    u = _rms(x, g1_ref[...]).astype(BF16)
    f = wup_ref.shape[1]
    acc = jnp.zeros(x.shape, F32)
    for c in range(f // MLP_HIDDEN_CHUNK):
        cols = slice(c * MLP_HIDDEN_CHUNK, (c + 1) * MLP_HIDDEN_CHUNK)
        hid = jnp.maximum(_dot(u, wup_ref[:, cols]), 0.0)
        acc = acc + _dot((hid * hid).astype(BF16), wdn_ref[cols, :])
    return x + _rms(acc, g2_ref[...])


def _mlp_kernel(x_ref, g1_ref, wup_ref, wdn_ref, g2_ref, o_ref):
    o_ref[...] = _mlp_body(x_ref[...], g1_ref, wup_ref, wdn_ref, g2_ref)


def _mix_mlp_kernel(x_ref, a_ref, wo_ref, gmix_ref, g1_ref, wup_ref, wdn_ref, g2_ref, o_ref):
    x = x_ref[...] + _rms(_dot(a_ref[...], wo_ref[...]), gmix_ref[...])
    o_ref[...] = _mlp_body(x, g1_ref, wup_ref, wdn_ref, g2_ref)


def _mlp_layer(x, g1, wup, wdn, g2, *, tm, mix=None):
    rows, d = x.shape
    f = wup.shape[1]
    tok = pl.BlockSpec((tm, d), lambda i: (i, 0))
    mlp_specs = [_const_spec((1, d)), _const_spec((d, f)), _const_spec((f, d)), _const_spec((1, d))]
    if mix is None:
        kern, specs, args = _mlp_kernel, [tok] + mlp_specs, (x, g1, wup, wdn, g2)
    else:
        a, wo, gmix = mix
        kern = _mix_mlp_kernel
        specs = [tok, pl.BlockSpec((tm, a.shape[1]), lambda i: (i, 0)), _const_spec(wo.shape), _const_spec((1, d))] + mlp_specs
        args = (x, a, wo, gmix, g1, wup, wdn, g2)
    return pl.pallas_call(
        kern,
        grid=(rows // tm,),
        in_specs=specs,
        out_specs=tok,
        out_shape=jax.ShapeDtypeStruct((rows, d), F32),
        compiler_params=_cparams("arbitrary"),
        name="mlp_layer",
    )(*args)


def _proj_kernel(x_ref, gkv_ref, wkv_ref, gq_ref, wq_ref, wg_ref, bg_ref, kv_ref, win_ref, sw_ref, q_ref, gate_ref,
                 *kt_refs):
    x = x_ref[...]
    rows = _dot(_rms(x, gkv_ref[...]).astype(BF16), wkv_ref[...])
    n_kv = kv_ref.shape[1]
    gw = N_KV_GROUPS * HEAD_DIM
    kv_ref[...] = rows[:, :n_kv]
    win_ref[...] = rows[:, n_kv:]
    sw_ref[...] = rows[:, n_kv // 2:].astype(BF16)
    if kt_refs:
        kst_ref, kwt_ref, vsa_ref, kvt_ref, wint_ref = kt_refs
        rows_t = rows.T
        kvt_ref[0] = rows_t[:n_kv]
        wint_ref[0] = rows_t[n_kv:]
        kst_ref[0] = rows_t[2 * gw:3 * gw].astype(BF16)
        kwt_ref[0] = rows_t[4 * gw:5 * gw].astype(BF16)
        v_sel = rows[:, 3 * gw:4 * gw]
        lane_grp = lax.broadcasted_iota(jnp.int32, v_sel.shape, 1) // HEAD_DIM
        vsa_ref[...] = jnp.concatenate([jnp.where(lane_grp == ONES_GROUP[0], 1.0, v_sel),
                                        jnp.where(lane_grp == ONES_GROUP[1], 1.0, v_sel)], axis=1).astype(BF16)
    u = _rms(x, gq_ref[...]).astype(BF16)
    q_ref[...] = (_dot(u, wq_ref[...]) * HEAD_DIM ** -0.5).astype(BF16)
    gate_ref[...] = jax.nn.sigmoid(_dot(u, wg_ref[...]) + bg_ref[...])


def _proj_layer(x, gkv, wkv, gq, wq, wg, bg, *, tm, seq_len=None):
    rows, d = x.shape
    n_all = wkv.shape[1]
    n_kv = 4 * N_KV_GROUPS * HEAD_DIM
    gw = N_KV_GROUPS * HEAD_DIM
    tok = lambda w: pl.BlockSpec((tm, w), lambda i: (i, 0))
    out_specs = [tok(n_kv), tok(n_all - n_kv), tok(n_all - n_kv // 2), tok(wq.shape[1]), tok(wg.shape[1])]
    out_shape = [jax.ShapeDtypeStruct((rows, n_kv), F32), jax.ShapeDtypeStruct((rows, n_all - n_kv), F32),
                 jax.ShapeDtypeStruct((rows, n_all - n_kv // 2), BF16), jax.ShapeDtypeStruct((rows, wq.shape[1]), BF16),
                 jax.ShapeDtypeStruct((rows, wg.shape[1]), F32)]
    if seq_len is not None:
        tiles = seq_len // tm
        kt = pl.BlockSpec((1, gw, tm), lambda i: (i // tiles, 0, i % tiles))
        ktw = lambda w: pl.BlockSpec((1, w, tm), lambda i: (i // tiles, 0, i % tiles))
        out_specs += [kt, kt, tok(2 * gw), ktw(n_kv), ktw(n_all - n_kv)]
        out_shape += [jax.ShapeDtypeStruct((rows // seq_len, gw, seq_len), BF16)] * 2
        out_shape += [jax.ShapeDtypeStruct((rows, 2 * gw), BF16),
                      jax.ShapeDtypeStruct((rows // seq_len, n_kv, seq_len), F32),
                      jax.ShapeDtypeStruct((rows // seq_len, n_all - n_kv, seq_len), F32)]
    return pl.pallas_call(
        _proj_kernel,
        grid=(rows // tm,),
        in_specs=[tok(d), _const_spec((1, d)), _const_spec(wkv.shape), _const_spec((1, d)), _const_spec(wq.shape),
                  _const_spec(wg.shape), _const_spec(bg.shape)],
        out_specs=out_specs,
        out_shape=out_shape,
        compiler_params=_cparams("arbitrary"),
        name="kv_q_proj",
    )(x, gkv, wkv, gq, wq, wg, bg)


CMP_GROUP_PAIR = 2 * HEAD_DIM


def _compress_kernel(*refs, n_pref, n_in):
    x_refs = refs[n_pref:n_pref + n_in]
    pe_ref, w1_ref, b1_ref, w2_ref, kc_ref, vc_ref, carry_sc, xs_sc = refs[n_pref + n_in:]
    step = pl.program_id(1)

    @pl.when(step == 0)
    def _():
        carry_sc[...] = jnp.zeros_like(carry_sc)

    set_w = N_KV_GROUPS * HEAD_DIM
    n_chunks = 2 * set_w // CMP_GROUP_PAIR
    page = x_refs[0].shape[2]
    halves = page // CMP_STRIDE
    for p, x_ref in enumerate(x_refs):
        x = x_ref[0].T
        for cc in range(n_chunks):
            xs_sc[p, cc] = x[:, cc * CMP_GROUP_PAIR:(cc + 1) * CMP_GROUP_PAIR]
    m = n_in * halves
    row = lax.broadcasted_iota(jnp.int32, (m, 2 * LANES), 0)
    for s, out_ref in enumerate((kc_ref, vc_ref)):
        for jp in range(set_w // CMP_GROUP_PAIR):
            cc = s * (set_w // CMP_GROUP_PAIR) + jp
            xj = jnp.concatenate(
                [jnp.concatenate([xs_sc[p, cc, pl.ds(tt, halves, stride=CMP_STRIDE), :] for tt in range(CMP_STRIDE)], axis=1)
                 for p in range(n_in)], axis=0)
            first = _dot((xj + pe_ref[s, 0]).astype(BF16), w1_ref[s, 0])
            second = _dot((xj + pe_ref[s, 1]).astype(BF16), w1_ref[s, 1])
            slot = s * 2 + jp
            prev = jnp.where(row == 0, carry_sc[slot, 0:1, :], pltpu.roll(first, 1, 0))
            carry_sc[slot, 0:1, :] = first[m - 1:m, :]
            hid = jax.nn.gelu(prev + second + b1_ref[s])
            out_ref[0, :, jp * CMP_GROUP_PAIR:(jp + 1) * CMP_GROUP_PAIR] = _dot(hid.astype(BF16), w2_ref[s]).astype(BF16)


CMP_PAGES_PER_STEP = 16


def _compress(src, cw, *, n_seq, seq_pages, page, page_table=None):
    n_in = CMP_PAGES_PER_STEP
    set_w = N_KV_GROUPS * HEAD_DIM
    halves = page // CMP_STRIDE
    m = n_in * halves
    steps = seq_pages // n_in
    block = (1, 2 * set_w, page)
    if page_table is None:
        x_specs = [pl.BlockSpec(block, functools.partial(lambda b, s, p: (b, 0, s * n_in + p), p=p)) for p in range(n_in)]
        n_pref = 0
    else:
        x_specs = [pl.BlockSpec(block, functools.partial(lambda b, s, pt, p: (pt[b, s * n_in + p], 0, 0), p=p))
                   for p in range(n_in)]
        n_pref = 1
    cidx = lambda nd: (lambda *_: (0,) * nd)
    const = lambda a: pl.BlockSpec(a.shape, cidx(a.ndim), pipeline_mode=pl.Buffered(1))
    out_spec = pl.BlockSpec((1, m, set_w), lambda b, s, *_: (b, s, 0))
    grid_spec = pltpu.PrefetchScalarGridSpec(
        num_scalar_prefetch=n_pref,
        grid=(n_seq, steps),
        in_specs=x_specs + [const(cw['pe']), const(cw['w1']), const(cw['b1']), const(cw['w2'])],
        out_specs=[out_spec, out_spec],
        scratch_shapes=[pltpu.VMEM((4, SUBLANES, 2 * LANES), F32),
                        pltpu.VMEM((n_in, 2 * set_w // CMP_GROUP_PAIR, page, CMP_GROUP_PAIR), F32)],
    )
    call = pl.pallas_call(
        functools.partial(_compress_kernel, n_pref=n_pref, n_in=n_in),
        grid_spec=grid_spec,
        out_shape=[jax.ShapeDtypeStruct((n_seq, seq_pages * halves, set_w), BF16)] * 2,
        compiler_params=_cparams("arbitrary", "arbitrary"),
        name="compress",
    )
    args = ([page_table] if page_table is not None else []) + [src] * n_in + [cw['pe'], cw['w1'], cw['b1'], cw['w2']]
    return call(*args)


KEY_TILE = 128
N_NEAR_TILES = REL_MAX_DIST // KEY_TILE + 1
TILE_FUTURE, TILE_FAR, TILE_WINDOW = 0, N_NEAR_TILES + 1, N_NEAR_TILES + 2


def _rel_bucket(dist):
    n = jnp.maximum(dist, 0)
    max_exact = NUM_BUCKETS // 2
    nf = jnp.maximum(n, 1).astype(F32)
    large = max_exact + (jnp.log(nf / max_exact) / math.log(REL_MAX_DIST / max_exact)
                         * (NUM_BUCKETS - max_exact)).astype(jnp.int32)
    return jnp.where(n < max_exact, n, jnp.minimum(large, NUM_BUCKETS - 1))


def _bias_lookup(table, dist):
    onehot = (_rel_bucket(dist)[..., None] == jnp.arange(NUM_BUCKETS)).astype(F32)
    out = jnp.einsum('...b,bh->...h', onehot, table.astype(F32), precision=lax.Precision.HIGHEST)
    return jnp.moveaxis(out, -1, 0)


CMP_BAND_W = 2 * LANES
CMP_BAND_E0 = Q_BLOCK // CMP_STRIDE - CMP_BAND_W


def _compressed_band(table):
    i = jnp.arange(Q_BLOCK)[:, None]
    e = CMP_BAND_E0 + jnp.arange(CMP_BAND_W)[None, :]
    dist = i - CMP_STRIDE * e - (CMP_STRIDE - 1)
    assert -CMP_STRIDE * (CMP_BAND_E0 - 1) - (CMP_STRIDE - 1) >= REL_MAX_DIST
    far = _bias_lookup(table, jnp.array(REL_MAX_DIST))
    return jnp.where(dist >= 0, _bias_lookup(table, dist) - far[:, None, None], 0.0)


def _toeplitz_tiles(table):
    i = jnp.arange(Q_BLOCK)[:, None]
    j = jnp.arange(KEY_TILE)[None, :]
    offs = jnp.array([-1] + list(range(N_NEAR_TILES)) + [N_NEAR_TILES + REL_MAX_DIST // KEY_TILE, WINDOW // KEY_TILE])
    dist = KEY_TILE * offs[:, None, None] + i - j
    ok = (dist >= 0) & ((jnp.arange(offs.shape[0]) != TILE_WINDOW)[:, None, None] | (dist < WINDOW))
    far = _bias_lookup(table, jnp.array(REL_MAX_DIST))
    return jnp.where(ok, _bias_lookup(table, dist) - far[:, None, None, None], NEG_INF).transpose(1, 0, 2, 3)


SEL_CHUNK = 512
ONES_GROUP = (1, 0)
WIN_KEYS = WINDOW + Q_BLOCK


def _masked_softmax(s, mask):
    s = jnp.where(mask, s, NEG_INF)
    e = jnp.exp(s - jnp.max(s, axis=-1, keepdims=True))
    p = e / jnp.sum(e, axis=-1, keepdims=True)
    return jnp.where(mask, p, 0.0)


def _topk_masks_t(vals, key_sc, corr_sc, n_rows):
    shape = vals[0].shape
    for i, v in enumerate(vals):
        key_sc[i] = jnp.where(v > 0.0, pltpu.bitcast(v, jnp.int32), jnp.where(v < 0.0, -1, 0))
    corr_sc[...] = jnp.zeros_like(corr_sc)
    jg = lax.broadcasted_iota(jnp.int32, shape, 0) // SUBLANES
    jj = lax.broadcasted_iota(jnp.int32, (SUBLANES, shape[1]), 0)

    def group_body(kg, cnts):
        base = pl.multiple_of(kg * SUBLANES, SUBLANES)
        out = []
        for i, cnt in enumerate(cnts):
            key = key_sc[i]
            thr = jnp.where(jg > kg, key - 1, key)
            diag = key_sc[i, pl.ds(base, SUBLANES), :]
            corr = jnp.zeros(diag.shape, jnp.int32)
            for kk in range(SUBLANES):
                k_row = key_sc[i, pl.ds(base + kk, 1), :]
                cnt = cnt + jnp.where(jnp.broadcast_to(k_row, shape) > thr, 1, 0)
                corr = corr + jnp.where(jnp.broadcast_to(k_row, diag.shape) == diag, jnp.where(jj > kk, 1, 0), 0)
            corr_sc[i, pl.ds(base, SUBLANES), :] = corr
            out.append(cnt)
        return tuple(out)

    cnts = lax.fori_loop(0, (n_rows + SUBLANES - 1) // SUBLANES, group_body,
                         tuple(jnp.zeros(shape, jnp.int32) for _ in vals))
    return [jnp.where(cnt + corr_sc[i] < N_SELECT, 1.0, 0.0) for i, cnt in enumerate(cnts)]


def _nsa_prompt_kernel(q_ref, gate_ref, kct_ref, vc_ref, kst_ref, vsa_ref, kwt_ref, vw_ref, tb_ref, cb_ref,
                       selmap_ref, selneg_ref, eg_ref, o_ref, key_sc, corr_sc, unsel_sc, ocmp_sc, osel_sc, m_sc,
                       acc_sc, s0_sc, s1_sc):
    qb = pl.program_id(1)
    t0 = qb * Q_BLOCK
    gw = N_KV_GROUPS * HEAD_DIM
    n_cmp = kct_ref.shape[2]
    rows = GROUP_SIZE * Q_BLOCK

    qi = lax.broadcasted_iota(jnp.int32, (Q_BLOCK, n_cmp), 0)
    hc = lax.broadcasted_iota(jnp.int32, (Q_BLOCK, n_cmp), 1)
    mask_c = jnp.where(hc >= 1, t0 + qi - (CMP_STRIDE * hc + CMP_STRIDE - 1), -1) >= 0
    first_row = qb * (Q_BLOCK // CMP_STRIDE)
    band_tile = (first_row + Q_BLOCK // CMP_STRIDE - 1) // LANES
    band_shift = (first_row % LANES + LANES + CMP_BAND_E0 + CMP_BAND_W) % CMP_BAND_W

    def compressed_bias(head):
        band = pltpu.roll(cb_ref[head], band_shift, 1)
        lo, hi = band[:, :LANES], band[:, LANES:]
        return jnp.concatenate([jnp.where(band_tile == c, hi, jnp.where(band_tile - 1 == c, lo, 0.0))
                                for c in range(n_cmp // LANES)], axis=1)

    lane_g = lax.broadcasted_iota(jnp.int32, (Q_BLOCK, gw), 1) // HEAD_DIM

    def group_queries(g):
        return jnp.concatenate(
            [jnp.where(lane_g == g, q_ref[0, :, r * gw:(r + 1) * gw], 0.0).astype(BF16) for r in range(GROUP_SIZE)], axis=0)

    nb = selmap_ref.shape[0]
    jb = lax.broadcasted_iota(jnp.int32, (nb, Q_BLOCK), 0)
    qpos_t = t0 + lax.broadcasted_iota(jnp.int32, (nb, Q_BLOCK), 1)
    cur = qpos_t // SEL_BLOCK
    forced = jnp.where(jb == 0, 1, 0) + jnp.where(jb == cur, 1, 0) + jnp.where(jb == cur - 1, 1, 0)
    valid_t = jb * SEL_BLOCK <= qpos_t
    n_valid_blocks = (t0 + Q_BLOCK - 1) // SEL_BLOCK + 1
    n_chunks = (t0 + Q_BLOCK - 1) // SEL_CHUNK + 1

    importance = []
    for g in range(N_KV_GROUPS):
        s_c = _dot(group_queries(g), kct_ref[0])
        p_rows = []
        for r in range(GROUP_SIZE):
            bias = compressed_bias(g * GROUP_SIZE + r)
            p_rows.append(_masked_softmax(s_c[r * Q_BLOCK:(r + 1) * Q_BLOCK] + bias, mask_c).astype(BF16))
        ocmp_sc[g] = _dot(jnp.concatenate(p_rows, axis=0), vc_ref[0])
        imp = jnp.zeros((nb, Q_BLOCK), F32)
        for r in range(GROUP_SIZE):
            imp = imp + lax.dot_general(selmap_ref[...], p_rows[r], (((1,), (1,)), ((), ())), preferred_element_type=F32)
        importance.append(jnp.where(forced > 0, 1e9, jnp.where(valid_t, imp, -1e9)))

    for g, mask_t in enumerate(_topk_masks_t(importance, key_sc, corr_sc, n_valid_blocks)):
        unsel_sc[g] = (1.0 - mask_t.T).astype(BF16)

    def group_body(g, carry):
        qm = group_queries(g)
        unsel = unsel_sc[g]

        m_sc[...] = jnp.full_like(m_sc, NEG_INF)
        acc_sc[...] = jnp.zeros_like(acc_sc)
        ones_at = jnp.where(g == ONES_GROUP[0], 1, 0)
        v_col = pl.multiple_of(ones_at * gw, gw)

        def scores(c):
            return _dot_split_rows(qm, kst_ref[0, :, pl.ds(pl.multiple_of(c * SEL_CHUNK, SEL_CHUNK), SEL_CHUNK)])

        def sweep_step(c, s_cur, s_next):
            off = pl.multiple_of(c * SEL_CHUNK, SEL_CHUNK)
            masked = _dot(unsel, selneg_ref[:, pl.ds(off, SEL_CHUNK)])
            s_next[...] = scores(jnp.minimum(c + 1, n_chunks - 1))
            parts = []
            for r in range(GROUP_SIZE):
                bias = jnp.concatenate(
                    [tb_ref[jnp.clip(qb - (c * (SEL_CHUNK // KEY_TILE) + h), -1, N_NEAR_TILES) + 1, g * GROUP_SIZE + r]
                     for h in range(SEL_CHUNK // KEY_TILE)], axis=1)
                parts.append(s_cur[r * Q_BLOCK:(r + 1) * Q_BLOCK, :] + (bias + masked))
            s = jnp.concatenate(parts, axis=0)
            m_old = m_sc[...]
            m_new = jnp.maximum(m_old, jnp.max(s, axis=-1, keepdims=True))
            alpha = jnp.exp(m_old - m_new)
            p = jnp.exp(s - jnp.concatenate([m_new] * (SEL_CHUNK // LANES), axis=1))
            pv = _dot_split_rows(p.astype(BF16), vsa_ref[0, pl.ds(off, SEL_CHUNK), pl.ds(v_col, gw)])
            acc_sc[...] = jnp.concatenate([alpha] * (gw // LANES), axis=1) * acc_sc[...] + pv
            m_sc[...] = m_new

        s0_sc[...] = scores(0)

        def pair_body(pair, carry2):
            sweep_step(2 * pair, s0_sc, s1_sc)

            @pl.when(2 * pair + 1 < n_chunks)
            def _():
                sweep_step(2 * pair + 1, s1_sc, s0_sc)

            return carry2

        lax.fori_loop(0, (n_chunks + 1) // 2, pair_body, 0)
        acc = acc_sc[...]
        denom = jnp.where(ones_at == 1, acc[:, ONES_GROUP[1] * HEAD_DIM:ONES_GROUP[1] * HEAD_DIM + 1],
                          acc[:, ONES_GROUP[0] * HEAD_DIM:ONES_GROUP[0] * HEAD_DIM + 1])
        osel_sc[g] = acc / denom
        return carry

    lax.fori_loop(0, N_KV_GROUPS, group_body, 0)

    n_wt = WIN_KEYS // KEY_TILE
    w_off = [pl.multiple_of(jnp.maximum(qb - (n_wt - 1) + w, 0) * KEY_TILE, KEY_TILE) for w in range(n_wt)]
    w_tile = [jnp.where(qb - (n_wt - 1) + w >= 0, TILE_WINDOW if w == 0 else n_wt - w, TILE_FUTURE) for w in range(n_wt)]
    k_win = jnp.concatenate([kwt_ref[0, :, pl.ds(o, KEY_TILE)] for o in w_off], axis=1)
    v_win = jnp.concatenate([vw_ref[0, pl.ds(o, KEY_TILE), :] for o in w_off], axis=0)
    gexp = jnp.dot(gate_ref[0], eg_ref[...], preferred_element_type=F32, precision=lax.Precision.HIGHEST)
    out = [jnp.zeros((Q_BLOCK, gw), F32) for _ in range(GROUP_SIZE)]
    for g in range(N_KV_GROUPS):
        s_w = _dot(group_queries(g), k_win)
        pw_rows = []
        for r in range(GROUP_SIZE):
            bias = jnp.concatenate([tb_ref[ti, g * GROUP_SIZE + r] for ti in w_tile], axis=1)
            s_r = s_w[r * Q_BLOCK:(r + 1) * Q_BLOCK] + bias
            e = jnp.exp(s_r - jnp.max(s_r, axis=-1, keepdims=True))
            pw_rows.append((e / jnp.sum(e, axis=-1, keepdims=True)).astype(BF16))
        o_win = _dot(jnp.concatenate(pw_rows, axis=0), v_win)
        for r in range(GROUP_SIZE):
            rs = slice(r * Q_BLOCK, (r + 1) * Q_BLOCK)
            mix = (gexp[:, (3 * r) * gw:(3 * r + 1) * gw] * ocmp_sc[g, rs, :]
                   + gexp[:, (3 * r + 1) * gw:(3 * r + 2) * gw] * osel_sc[g, rs, :]
                   + gexp[:, (3 * r + 2) * gw:(3 * r + 3) * gw] * o_win[rs])
            out[r] = out[r] + jnp.where(lane_g == g, mix, 0.0)
    for r in range(GROUP_SIZE):
        o_ref[0, :, r * gw:(r + 1) * gw] = out[r].astype(BF16)


def _nsa_prompt(q, gates, kct, vc, kst, vsa, kwt, selwin, tiles, band, selmap, selneg, eg):
    n, t, hw = q.shape
    gw = N_KV_GROUPS * HEAD_DIM
    n_cmp = kct.shape[2]
    rows = GROUP_SIZE * Q_BLOCK
    per_n = lambda shape, col=0: pl.BlockSpec(shape, lambda b, j: (b, 0, col), pipeline_mode=pl.Buffered(1))
    blk = lambda w: pl.BlockSpec((1, Q_BLOCK, w), lambda b, j: (b, j, 0))
    return pl.pallas_call(
        _nsa_prompt_kernel,
        grid=(n, t // Q_BLOCK),
        in_specs=[blk(hw), blk(gates.shape[2]),
                  per_n((1, gw, n_cmp)), per_n((1, n_cmp, gw)), per_n((1, gw, t)), per_n((1, t, 2 * gw)),
                  per_n((1, gw, t)), per_n((1, t, gw), 3),
                  _const_spec(tiles.shape), _const_spec(band.shape), _const_spec(selmap.shape), _const_spec(selneg.shape),
                  _const_spec(eg.shape)],
        out_specs=blk(hw),
        out_shape=jax.ShapeDtypeStruct((n, t, hw), BF16),
        scratch_shapes=[pltpu.VMEM((N_KV_GROUPS, selmap.shape[0], Q_BLOCK), jnp.int32),
                        pltpu.VMEM((N_KV_GROUPS, selmap.shape[0], Q_BLOCK), jnp.int32),
                        pltpu.VMEM((N_KV_GROUPS, Q_BLOCK, selmap.shape[0]), BF16),
                        pltpu.VMEM((N_KV_GROUPS, rows, gw), F32), pltpu.VMEM((N_KV_GROUPS, rows, gw), F32),
                        pltpu.VMEM((rows, LANES), F32), pltpu.VMEM((rows, gw), F32),
                        pltpu.VMEM((rows, SEL_CHUNK), F32), pltpu.VMEM((rows, SEL_CHUNK), F32)],
        compiler_params=_cparams("arbitrary", "arbitrary"),
        name="nsa_prompt",
    )(q, gates, kct, vc, kst, vsa, kwt, selwin, tiles, band, selmap, selneg, eg)


DEC_T_PAD = SUBLANES
SAMPLE_BLOCKS_PAD = 256
NEW_KEYS_PAD = LANES


def _nsa_sample_kernel(*refs, n_pages, n_blocks):
    pt_ref, q_ref, gate_ref, kct_ref, vc_ref = refs[:5]
    page_refs = refs[5:5 + n_pages]
    (bsel_ref, bcmp_ref, bwin_ref, bnew_ref, new_ref, cwin_ref, selmap_ref, eg_ref, o_ref,
     key_sc, corr_sc, sel_sc, m_sc, l_sc, acc_sc, oc_sc) = refs[5 + n_pages:]
    c = pl.program_id(1)
    gw = N_KV_GROUPS * HEAD_DIM
    rows = N_HEADS * DEC_T_PAD
    page = page_refs[0].shape[2]
    lane_g = lax.broadcasted_iota(jnp.int32, (DEC_T_PAD, gw), 1) // HEAD_DIM
    qm = jnp.concatenate(
        [jnp.where(lane_g == g, q_ref[0, :, r * gw:(r + 1) * gw], 0.0)
         for r in range(GROUP_SIZE) for g in range(N_KV_GROUPS)], axis=0).astype(BF16)
    nt = (((1,), (1,)), ((), ()))

    @pl.when(c == 0)
    def _():
        p_c = _masked_softmax(_dot(qm, kct_ref[0]) + bcmp_ref[...], bcmp_ref[...] > 0.5 * NEG_INF).astype(BF16)
        oc_sc[...] = _dot(p_c, vc_ref[0])
        imp_all = lax.dot_general(selmap_ref[...], p_c, nt, preferred_element_type=F32)
        per_r = rows // GROUP_SIZE
        imp = imp_all
        for r in range(1, GROUP_SIZE):
            imp = imp + pltpu.roll(imp_all, r * per_r, 1)
        jb = lax.broadcasted_iota(jnp.int32, imp.shape, 0)
        last = n_blocks - 1
        forced = jnp.where(jb == 0, 1, 0) + jnp.where(jb == last, 1, 0) + jnp.where(jb == last - 1, 1, 0)
        imp = jnp.where(forced > 0, 1e9, jnp.where(jb <= last, imp, -1e9))
        sel_sc[...] = _topk_masks_t([imp], key_sc, corr_sc, n_blocks)[0].T.astype(BF16)
        m_sc[...] = jnp.full_like(m_sc, NEG_INF)
        l_sc[...] = jnp.zeros_like(l_sc)
        acc_sc[...] = jnp.zeros_like(acc_sc)

    def online_update(s, pv_of):
        m_old = m_sc[...]
        m_new = jnp.maximum(m_old, jnp.max(s, axis=-1, keepdims=True))
        alpha = jnp.exp(m_old - m_new)
        p = jnp.exp(s - m_new)
        l_sc[...] = alpha * l_sc[...] + jnp.sum(p, axis=-1, keepdims=True)
        acc_sc[...] = alpha * acc_sc[...] + pv_of(p.astype(BF16))
        m_sc[...] = m_new

    keys = n_pages * page
    s = jnp.concatenate([_dot(qm, pr[0, :gw, :].astype(BF16)) for pr in page_refs], axis=1)
    eb = lax.broadcasted_iota(jnp.int32, (SAMPLE_BLOCKS_PAD, keys), 0)
    ek = lax.broadcasted_iota(jnp.int32, (SAMPLE_BLOCKS_PAD, keys), 1)
    expand = jnp.where(eb == c * (keys // SEL_BLOCK) + ek // SEL_BLOCK, 1.0, 0.0).astype(BF16)
    picked = _dot(sel_sc[...], expand) > 0.5
    v_t = jnp.concatenate([pr[0, gw:, :].astype(BF16) for pr in page_refs], axis=1)
    online_update(jnp.where(picked, s + bsel_ref[...], NEG_INF),
                  lambda p: lax.dot_general(p, v_t, nt, preferred_element_type=F32))

    @pl.when(c == pl.num_programs(1) - 1)
    def _():
        s_new = lax.dot_general(qm, new_ref[0, :, :gw], nt, preferred_element_type=F32) + bnew_ref[0]
        online_update(s_new, lambda p: _dot(p, new_ref[0, :, gw:2 * gw]))
        o_sel = acc_sc[...] / l_sc[...]

        s_w = _dot(qm, cwin_ref[0, :gw, :].astype(BF16)) + bwin_ref[...]
        s_wn = lax.dot_general(qm, new_ref[0, :, 2 * gw:3 * gw], nt, preferred_element_type=F32) + bnew_ref[1]
        m_w = jnp.maximum(jnp.max(s_w, axis=-1, keepdims=True), jnp.max(s_wn, axis=-1, keepdims=True))
        e_w = jnp.exp(s_w - m_w)
        e_wn = jnp.exp(s_wn - m_w)
        l_w = jnp.sum(e_w, axis=-1, keepdims=True) + jnp.sum(e_wn, axis=-1, keepdims=True)
        o_win = lax.dot_general((e_w / l_w).astype(BF16), cwin_ref[0, gw:, :].astype(BF16), nt,
                                preferred_element_type=F32) + _dot((e_wn / l_w).astype(BF16), new_ref[0, :, 3 * gw:])

        gexp = jnp.dot(gate_ref[0], eg_ref[...], preferred_element_type=F32, precision=lax.Precision.HIGHEST)
        o_cmp = oc_sc[...]
        for r in range(GROUP_SIZE):
            out_r = jnp.zeros((DEC_T_PAD, gw), F32)
            for g in range(N_KV_GROUPS):
                rs = slice((r * N_KV_GROUPS + g) * DEC_T_PAD, (r * N_KV_GROUPS + g + 1) * DEC_T_PAD)
                mix = (gexp[:, (3 * r) * gw:(3 * r + 1) * gw] * o_cmp[rs]
                       + gexp[:, (3 * r + 1) * gw:(3 * r + 2) * gw] * o_sel[rs]
                       + gexp[:, (3 * r + 2) * gw:(3 * r + 3) * gw] * o_win[rs])
                out_r = out_r + jnp.where(lane_g == g, mix, 0.0)
            o_ref[0, :, r * gw:(r + 1) * gw] = out_r


PAGES_PER_STEP = 8


def _nsa_sample(page_table, q, gates, kct, vc, cache, bias, new_rows, cache_win, selmap, eg, n_blocks):
    n, _, hw = q.shape
    gw = N_KV_GROUPS * HEAD_DIM
    n_cmp = kct.shape[2]
    rows = N_HEADS * DEC_T_PAD
    page = cache.shape[2]
    steps = page_table.shape[1] // PAGES_PER_STEP
    keys = PAGES_PER_STEP * page
    per_n = lambda a: pl.BlockSpec((1,) + a.shape[1:], lambda b, c, pt: (b, 0, 0))
    cidx = lambda nd: (lambda *_: (0,) * nd)
    const = lambda a: pl.BlockSpec(a.shape, cidx(a.ndim), pipeline_mode=pl.Buffered(1))
    page_specs = [pl.BlockSpec((1, 2 * gw, page), functools.partial(
        lambda b, c, pt, p: (pt[b, c * PAGES_PER_STEP + p], 1, 0), p=p)) for p in range(PAGES_PER_STEP)]
    grid_spec = pltpu.PrefetchScalarGridSpec(
        num_scalar_prefetch=1,
        grid=(n, steps),
        in_specs=[per_n(q), per_n(gates), per_n(kct), per_n(vc)] + page_specs + [
            pl.BlockSpec((rows, keys), lambda b, c, pt: (0, c)), const(bias['cmp']), const(bias['win']), const(bias['new']),
            per_n(new_rows), per_n(cache_win), const(selmap), const(eg)],
        out_specs=pl.BlockSpec((1, DEC_T_PAD, hw), lambda b, c, pt: (b, 0, 0)),
        scratch_shapes=[pltpu.VMEM((1, SAMPLE_BLOCKS_PAD, rows), jnp.int32),
                        pltpu.VMEM((1, SAMPLE_BLOCKS_PAD, rows), jnp.int32),
                        pltpu.VMEM((rows, SAMPLE_BLOCKS_PAD), BF16),
                        pltpu.VMEM((rows, 1), F32), pltpu.VMEM((rows, 1), F32), pltpu.VMEM((rows, gw), F32),
                        pltpu.VMEM((rows, gw), F32)],
    )
    return pl.pallas_call(
        functools.partial(_nsa_sample_kernel, n_pages=PAGES_PER_STEP, n_blocks=n_blocks),
        grid_spec=grid_spec,
        out_shape=jax.ShapeDtypeStruct((n, DEC_T_PAD, hw), F32),
        compiler_params=_cparams("arbitrary", "arbitrary"),
        name="nsa_sample",
    )(page_table, q, gates, kct, vc, *([cache] * PAGES_PER_STEP), bias['sel'], bias['cmp'], bias['win'], bias['new'],
      new_rows, cache_win, selmap, eg)


def _sample_bias(table, past, dec_t, win_len, n_cmp_rows):
    t = jnp.arange(DEC_T_PAD)[:, None]

    def rows(dist, ok):
        b = jnp.where(ok, _bias_lookup(table, dist), NEG_INF)
        b = b.reshape(N_KV_GROUPS, GROUP_SIZE, DEC_T_PAD, -1).transpose(1, 0, 2, 3)
        return b.reshape(N_HEADS * DEC_T_PAD, -1)

    dist_s = past + t - jnp.arange(past)[None, :]
    h = jnp.arange(n_cmp_rows)[None, :]
    dist_c = past + t - (CMP_STRIDE * h + CMP_STRIDE - 1)
    j = jnp.arange(win_len)[None, :]
    dist_w = win_len + t - j
    tn = jnp.arange(NEW_KEYS_PAD)[None, :]
    new = rows(t - tn, (tn <= t) & (tn < dec_t))
    return dict(sel=rows(dist_s, dist_s >= 0), cmp=rows(dist_c, (h >= 1) & (dist_c >= 0)),
                win=rows(dist_w, (dist_w < WINDOW) & (past - win_len + j >= 0)), new=jnp.stack([new, new]))


GATE_LANES = LANES


def _prepare(p):
    d = p['norm_pre_mix'].shape[1]
    hw = N_HEADS * HEAD_DIM
    gw = N_KV_GROUPS * HEAD_DIM
    row = lambda v: v.reshape(1, -1)
    w = {}
    w['rg'] = dict(
        g_pre=row(p['norm_pre_mix'][0]), w_in=p['rg_w_in'][0].astype(BF16), conv_w=p['rg_conv_w'][0],
        conv_b=row(p['rg_conv_b'][0]),
        w_ax=jnp.concatenate([p['rg_w_a'][0], p['rg_w_x'][0]], axis=2).astype(BF16),
        b_ax=jnp.stack([p['rg_b_a'][0], p['rg_b_x'][0]]), lam=row(p['rg_lambda'][0]),
        w_out=p['rg_w_out'][0].astype(BF16), g_post=row(p['norm_post_mix'][0]))
    w['mlp'] = [dict(g1=row(p['norm_pre_mlp'][l]), wup=p['w_mlp_up'][l].astype(BF16),
                     wdn=p['w_mlp_down'][l].astype(BF16), g2=row(p['norm_post_mlp'][l])) for l in range(2)]
    wqg = p['nsa_w_qg'][0]
    n_gate = wqg.shape[1] - hw
    w['proj'] = dict(
        gkv=row(p['kv_norm']), wkv=p['w_kv'].astype(BF16), gq=row(p['norm_pre_mix'][1]),
        wq=wqg[:, :hw].reshape(d, N_KV_GROUPS, GROUP_SIZE, HEAD_DIM).transpose(0, 2, 1, 3).reshape(d, hw).astype(BF16),
        wg=jnp.pad(wqg[:, hw:], ((0, 0), (0, GATE_LANES - n_gate))).astype(BF16),
        bg=jnp.pad(p['nsa_b_g'][0], (0, GATE_LANES - n_gate)).reshape(1, -1))
    w['wo'] = p['nsa_w_o'][0].reshape(N_KV_GROUPS, GROUP_SIZE, HEAD_DIM, d).transpose(1, 0, 2, 3).reshape(hw, d).astype(BF16)
    w['g_post_mix1'] = row(p['norm_post_mix'][1])
    eye2 = jnp.eye(2, dtype=F32)
    half = L_CMP // CMP_STRIDE
    w['cmp'] = dict(
        pe=jnp.broadcast_to(p['cmp_pe'].reshape(2, half, CMP_STRIDE, 1, HEAD_DIM),
                            (2, half, CMP_STRIDE, 2, HEAD_DIM)).reshape(2, half, 1, CMP_STRIDE * CMP_GROUP_PAIR),
        w1=jnp.einsum('sftdc,gh->sftgdhc', p['cmp_w1'].reshape(2, half, CMP_STRIDE, HEAD_DIM, -1), eye2)
        .reshape(2, half, CMP_STRIDE * CMP_GROUP_PAIR, -1).astype(BF16),
        b1=jnp.tile(p['cmp_b1'], (1, 2)).reshape(2, 1, -1),
        w2=jnp.einsum('scd,gh->sgchd', p['cmp_w2'], eye2).reshape(2, -1, CMP_GROUP_PAIR).astype(BF16))
    w['table'] = p['rel_bias_table']
    w['tiles'] = _toeplitz_tiles(p['rel_bias_table'])
    w['band'] = _compressed_band(p['rel_bias_table'])
    h_i = jnp.arange(N_HEADS)
    src = (h_i[:, None] * 3 + jnp.arange(3)[None, :]).reshape(-1)
    dst0 = (((h_i % GROUP_SIZE)[:, None] * 3 + jnp.arange(3)[None, :]) * gw
            + (h_i // GROUP_SIZE)[:, None] * HEAD_DIM).reshape(-1)
    lanes = jnp.arange(3 * GROUP_SIZE * gw)
    w['eg'] = jnp.zeros((GATE_LANES, lanes.shape[0]), F32).at[src].set(
        ((lanes[None, :] >= dst0[:, None]) & (lanes[None, :] < dst0[:, None] + HEAD_DIM)).astype(F32))
    return w


def _selmap_t(n_blocks, n_cmp_rows):
    ratio = SEL_BLOCK // CMP_STRIDE
    h = jnp.arange(n_cmp_rows)[None, :]
    j = jnp.arange(n_blocks)[:, None]
    m = ((h - 1) // ratio == j).astype(F32) + (h // ratio == j).astype(F32)
    return jnp.where(h >= 1, m, 0.0).astype(BF16)


def _trunk_prompt(x, w):
    n, t, d = x.shape
    gw = N_KV_GROUPS * HEAD_DIM
    zeros = lambda r: jnp.zeros((n, r, d), F32)
    x1, h_last, c_last = _rglru_layer(x, zeros(1), zeros(CONV_W - 1), w['rg'], stride=1, tm=256)
    m0 = w['mlp'][0]
    x2 = _mlp_layer(x1.reshape(n * t, d), m0['g1'], m0['wup'], m0['wdn'], m0['g2'], tm=512)
    pj = w['proj']
    kv4, _, selwin, q, gates, kst, kwt, vsa, kv_t, win_t = _proj_layer(
        x2, pj['gkv'], pj['wkv'], pj['gq'], pj['wq'], pj['wg'], pj['bg'], tm=512, seq_len=t)
    nh = t // CMP_STRIDE
    n_blocks = t // SEL_BLOCK
    kc, vc = _compress(kv_t, w['cmp'], n_seq=n, seq_pages=t // KEY_TILE, page=KEY_TILE)
    selneg = jnp.where(jnp.arange(t)[None, :] // SEL_BLOCK == jnp.arange(n_blocks)[:, None], NEG_INF, 0.0).astype(BF16)
    attn = _nsa_prompt(q.reshape(n, t, -1), gates.reshape(n, t, -1), kc.transpose(0, 2, 1), vc, kst, vsa.reshape(n, t, -1),
                       kwt, selwin.reshape(n, t, -1), w['tiles'], w['band'], _selmap_t(n_blocks, nh), selneg, w['eg'])
    m1 = w['mlp'][1]
    y = _mlp_layer(x2, m1['g1'], m1['wup'], m1['wdn'], m1['g2'], tm=512,
                   mix=(attn.reshape(n * t, -1), w['wo'], w['g_post_mix1']))
    wlen = min(WINDOW, t)
    rows_of = lambda a, sets: a.reshape(n, sets, N_KV_GROUPS, HEAD_DIM, -1).transpose(0, 4, 1, 2, 3)
    return (y.reshape(n, t, d), h_last.reshape(1, n, d), c_last.reshape(1, n, CONV_W - 1, d),
            rows_of(kv_t, 4), rows_of(win_t[:, :, t - wlen:], 2))


def _trunk_sample(x, h0, c0, cache_kv, cache_win, page_table, w):
    n, t, d = x.shape
    n_phys, page = cache_kv.shape[:2]
    past = page_table.shape[1] * page
    win_len = cache_win.shape[1]
    assert t < CMP_STRIDE and past % CMP_STRIDE == 0 and t <= DEC_T_PAD and page % CMP_STRIDE == 0
    assert page_table.shape[1] % PAGES_PER_STEP == 0 and page == KEY_TILE
    to_tn = lambda a: a.transpose(1, 0, 2).reshape(a.shape[0] * a.shape[1], -1)
    to_nt = lambda a, k: a.reshape(k, n, -1).transpose(1, 0, 2)
    tail = CONV_W - 1
    x1, h_last, c_last = _rglru_layer(to_tn(x)[None], h0[None], to_tn(c0)[None], w['rg'], stride=n, tm=t * n)
    m0 = w['mlp'][0]
    x2 = _mlp_layer(x1[0], m0['g1'], m0['wup'], m0['wdn'], m0['g2'], tm=t * n)
    pj = w['proj']
    kv4, win, selwin, q, gates = _proj_layer(x2, pj['gkv'], pj['wkv'], pj['gq'], pj['wq'], pj['wg'], pj['bg'], tm=t * n)

    row_w = cache_kv.shape[2] * cache_kv.shape[3] * cache_kv.shape[4]
    nh = past // CMP_STRIDE
    cache_t = cache_kv.reshape(n_phys, page, row_w).transpose(0, 2, 1)
    cache_win_t = cache_win.reshape(n, win_len, -1).transpose(0, 2, 1)
    kc, vc = _compress(cache_t, w['cmp'], n_seq=n, seq_pages=page_table.shape[1], page=page, page_table=page_table)
    pad_t = lambda a, k: jnp.pad(a, ((0, 0), (0, k - a.shape[1]), (0, 0)))
    n_blocks = -(-(past + t) // SEL_BLOCK)
    attn = _nsa_sample(
        page_table, pad_t(to_nt(q, t).astype(F32), DEC_T_PAD), pad_t(to_nt(gates, t), DEC_T_PAD), kc.transpose(0, 2, 1), vc,
        cache_t, _sample_bias(w['table'], past, t, win_len, nh),
        pad_t(to_nt(selwin, t), NEW_KEYS_PAD), cache_win_t, _selmap_t(SAMPLE_BLOCKS_PAD, nh), w['eg'], n_blocks)
    m1 = w['mlp'][1]
    y = _mlp_layer(x2, m1['g1'], m1['wup'], m1['wdn'], m1['g2'], tm=t * n,
                   mix=(to_tn(attn[:, :t]).astype(BF16), w['wo'], w['g_post_mix1']))
    win_rows = to_nt(win, t).reshape(n, t, 2, N_KV_GROUPS, HEAD_DIM)
    win_new = jnp.concatenate([cache_win, win_rows], axis=1)[:, t:]
    return (to_nt(y, t), h_last, to_nt(c_last[0], tail)[None],
            to_nt(kv4, t).reshape(n, t, 4, N_KV_GROUPS, HEAD_DIM), win_new)


def kernel(x_prompt, x_sample, state_rglru_h, state_rglru_conv, cache_kv, cache_win, page_table, norm_pre_mix,
           norm_post_mix, norm_pre_mlp, norm_post_mlp, w_mlp_up, w_mlp_down, rg_w_in, rg_conv_w, rg_conv_b, rg_w_a, rg_b_a,
           rg_w_x, rg_b_x, rg_lambda, rg_w_out, kv_norm, w_kv, cmp_pe, cmp_w1, cmp_b1, cmp_w2, nsa_w_qg, nsa_b_g, nsa_w_o,
           rel_bias_table):
    w = _prepare(dict(
        norm_pre_mix=norm_pre_mix, norm_post_mix=norm_post_mix, norm_pre_mlp=norm_pre_mlp, norm_post_mlp=norm_post_mlp,
        w_mlp_up=w_mlp_up, w_mlp_down=w_mlp_down, rg_w_in=rg_w_in, rg_conv_w=rg_conv_w, rg_conv_b=rg_conv_b,
        rg_w_a=rg_w_a, rg_b_a=rg_b_a, rg_w_x=rg_w_x, rg_b_x=rg_b_x, rg_lambda=rg_lambda, rg_w_out=rg_w_out,
        kv_norm=kv_norm, w_kv=w_kv, cmp_pe=cmp_pe, cmp_w1=cmp_w1, cmp_b1=cmp_b1, cmp_w2=cmp_w2, nsa_w_qg=nsa_w_qg,
        nsa_b_g=nsa_b_g, nsa_w_o=nsa_w_o, rel_bias_table=rel_bias_table))
    y_p, p_h, p_conv, p_kv, p_win = _trunk_prompt(x_prompt, w)
    y_s, s_h, s_conv, s_kv, s_win = _trunk_sample(x_sample, state_rglru_h[0], state_rglru_conv[0], cache_kv, cache_win,
                                                  page_table, w)
    return (y_p, y_s, p_h, p_conv, p_kv, p_win, s_h, s_conv, s_kv, s_win)
```

```python
import functools
import math

import jax
import jax.numpy as jnp
import numpy as np
from jax import lax
from jax.experimental import pallas as pl
from jax.experimental.pallas import tpu as pltpu

F32 = jnp.float32
BF16 = jnp.bfloat16

N_HEADS = 16
N_KV_GROUPS = 4
GROUP_SIZE = N_HEADS // N_KV_GROUPS
HEAD_DIM = 64
N_RNN_BLOCKS = 4
CONV_W = 4
LRU_C = 8.0
L_CMP = 32
CMP_STRIDE = 16
SEL_BLOCK = 64
N_SELECT = 16
WINDOW = 512
Q_BLOCK = 128
NUM_BUCKETS = 32
REL_MAX_DIST = 1024
EPS = 1e-6
NEG_INF = -1e30

SUBLANES = 8
LANES = 128
VMEM_LIMIT_BYTES = 56 * 1024 * 1024


def _cparams(*sem):
    return pltpu.CompilerParams(dimension_semantics=sem, vmem_limit_bytes=VMEM_LIMIT_BYTES)


def _const_spec(shape):
    nd = len(shape)
    return pl.BlockSpec(shape, lambda *_: (0,) * nd, pipeline_mode=pl.Buffered(1))


def _rms(x, g):
    return x * lax.rsqrt(jnp.mean(x * x, axis=-1, keepdims=True) + EPS) * g


def _dot(a, b):
    return jnp.dot(a, b, preferred_element_type=F32)


def _dot_split_rows(a, b):
    half = a.shape[0] // 2
    return jnp.concatenate([_dot(a[:half], b), _dot(a[half:], b)], axis=0)


def _rglru_kernel(x_ref, h0_ref, c0_ref, gpre_ref, win_ref, cw_ref, cb_ref, wax_ref, bax_ref, lam_ref,
                  wout_ref, gpost_ref, x1_ref, hlast_ref, clast_ref, xcat_sc, hprev_sc, *, stride, tm, pad):
    d = x_ref.shape[-1]
    tail = (CONV_W - 1) * stride
    j = pl.program_id(1)

    @pl.when(j == 0)
    def _():
        xcat_sc[pad - tail:pad, :] = c0_ref[0]
        hprev_sc[...] = jnp.zeros_like(hprev_sc)
        hprev_sc[tm - stride:tm, :] = h0_ref[0]

    x = x_ref[0]
    u = _rms(x, gpre_ref[...]).astype(BF16)
    proj = _dot(u, win_ref[...])
    gate = jax.nn.gelu(proj[:, :d])
    xcat_sc[pad:pad + tm, :] = proj[:, d:]
    xc = cb_ref[...] + cw_ref[CONV_W - 1:CONV_W, :] * xcat_sc[pad:pad + tm, :]
    for lag in range(1, CONV_W):
        xc = xc + cw_ref[CONV_W - 1 - lag:CONV_W - lag, :] * xcat_sc[pad - lag * stride:pad - lag * stride + tm, :]
    xcat_sc[pad - tail:pad, :] = xcat_sc[pad + tm - tail:pad + tm, :]
    clast_ref[0] = xcat_sc[pad - tail:pad, :]

    xcb = xc.astype(BF16)
    bw = d // N_RNN_BLOCKS
    ra, rx = [], []
    for blk in range(N_RNN_BLOCKS):
        pre = _dot(xcb[:, blk * bw:(blk + 1) * bw], wax_ref[blk])
        ra.append(pre[:, :bw])
        rx.append(pre[:, bw:])
    r = jax.nn.sigmoid(jnp.concatenate(ra, axis=1) + bax_ref[0:1, :])
    i = jax.nn.sigmoid(jnp.concatenate(rx, axis=1) + bax_ref[1:2, :])
    log_a = r * (-LRU_C * jax.nn.softplus(-lam_ref[...]))
    a = jnp.exp(log_a)
    b = jnp.sqrt(-jnp.tanh(log_a) * (a * a + 1.0)) * i * xc

    row = lax.broadcasted_iota(jnp.int32, (tm, d), 0)
    b = b + jnp.where(row < stride, a * pltpu.roll(hprev_sc[...], stride, 0), 0.0)
    s = stride
    while s < tm:
        keep = row >= s
        b = a * jnp.where(keep, pltpu.roll(b, s, 0), 0.0) + b
        a = a * jnp.where(keep, pltpu.roll(a, s, 0), 1.0)
        s *= 2
    h = b
    hprev_sc[...] = h
    hlast_ref[0] = h[tm - stride:tm, :]

    y = _dot((h * gate).astype(BF16), wout_ref[...])
    x1_ref[0] = x + _rms(y, gpost_ref[...])


def _rglru_layer(x, h0, c0, w, *, stride, tm):
    n, rows, d = x.shape
    tail = (CONV_W - 1) * stride
    pad = -(-tail // SUBLANES) * SUBLANES
    kern = functools.partial(_rglru_kernel, stride=stride, tm=tm, pad=pad)
    seq = lambda shape: pl.BlockSpec(shape, lambda b, j: (b, j, 0))
    per_n = lambda shape: pl.BlockSpec(shape, lambda b, j: (b, 0, 0))
    return pl.pallas_call(
        kern,
        grid=(n, rows // tm),
        in_specs=[seq((1, tm, d)), per_n((1, stride, d)), per_n((1, tail, d)),
                  _const_spec((1, d)), _const_spec(w['w_in'].shape), _const_spec((CONV_W, d)), _const_spec((1, d)),
                  _const_spec(w['w_ax'].shape), _const_spec((2, d)), _const_spec((1, d)),
                  _const_spec(w['w_out'].shape), _const_spec((1, d))],
        out_specs=[seq((1, tm, d)), per_n((1, stride, d)), per_n((1, tail, d))],
        out_shape=[jax.ShapeDtypeStruct((n, rows, d), F32), jax.ShapeDtypeStruct((n, stride, d), F32),
                   jax.ShapeDtypeStruct((n, tail, d), F32)],
        scratch_shapes=[pltpu.VMEM((pad + tm, d), F32), pltpu.VMEM((tm, d), F32)],
        compiler_params=_cparams("arbitrary", "arbitrary"),
        name="rglru_layer",
    )(x, h0, c0, w['g_pre'], w['w_in'], w['conv_w'], w['conv_b'], w['w_ax'], w['b_ax'], w['lam'], w['w_out'], w['g_post'])


MLP_HIDDEN_CHUNK = 1024


def _mlp_body(x, g1_ref, wup_ref, wdn_ref, g2_ref):
    u = _rms(x, g1_ref[...]).astype(BF16)
    f = wup_ref.shape[1]
    acc = jnp.zeros(x.shape, F32)
    for c in range(f // MLP_HIDDEN_CHUNK):
        cols = slice(c * MLP_HIDDEN_CHUNK, (c + 1) * MLP_HIDDEN_CHUNK)
        hid = jnp.maximum(_dot(u, wup_ref[:, cols]), 0.0)
        acc = acc + _dot((hid * hid).astype(BF16), wdn_ref[cols, :])
    return x + _rms(acc, g2_ref[...])


def _mlp_kernel(x_ref, g1_ref, wup_ref, wdn_ref, g2_ref, o_ref):
    o_ref[...] = _mlp_body(x_ref[...], g1_ref, wup_ref, wdn_ref, g2_ref)


def _mix_mlp_kernel(x_ref, a_ref, wo_ref, gmix_ref, g1_ref, wup_ref, wdn_ref, g2_ref, o_ref):
    x = x_ref[...] + _rms(_dot(a_ref[...], wo_ref[...]), gmix_ref[...])
    o_ref[...] = _mlp_body(x, g1_ref, wup_ref, wdn_ref, g2_ref)


def _mlp_layer(x, g1, wup, wdn, g2, *, tm, mix=None):
    rows, d = x.shape
    f = wup.shape[1]
    tok = pl.BlockSpec((tm, d), lambda i: (i, 0))
    mlp_specs = [_const_spec((1, d)), _const_spec((d, f)), _const_spec((f, d)), _const_spec((1, d))]
    if mix is None:
        kern, specs, args = _mlp_kernel, [tok] + mlp_specs, (x, g1, wup, wdn, g2)
    else:
        a, wo, gmix = mix
        kern = _mix_mlp_kernel
        specs = [tok, pl.BlockSpec((tm, a.shape[1]), lambda i: (i, 0)), _const_spec(wo.shape), _const_spec((1, d))] + mlp_specs
        args = (x, a, wo, gmix, g1, wup, wdn, g2)
    return pl.pallas_call(
        kern,
        grid=(rows // tm,),
        in_specs=specs,
        out_specs=tok,
        out_shape=jax.ShapeDtypeStruct((rows, d), F32),
        compiler_params=_cparams("arbitrary"),
        name="mlp_layer",
    )(*args)


def _proj_kernel(x_ref, gkv_ref, wkv_ref, gq_ref, wq_ref, wg_ref, bg_ref, kv_ref, win_ref, sw_ref, q_ref, gate_ref,
                 *kt_refs):
    x = x_ref[...]
    rows = _dot(_rms(x, gkv_ref[...]).astype(BF16), wkv_ref[...])
    n_kv = kv_ref.shape[1]
    gw = N_KV_GROUPS * HEAD_DIM
    kv_ref[...] = rows[:, :n_kv]
    win_ref[...] = rows[:, n_kv:]
    sw_ref[...] = rows[:, n_kv // 2:].astype(BF16)
    if kt_refs:
        kst_ref, kwt_ref, vsa_ref, kvt_ref, wint_ref = kt_refs
        rows_t = rows.T
        kvt_ref[0] = rows_t[:n_kv]
        wint_ref[0] = rows_t[n_kv:]
        kst_ref[0] = rows_t[2 * gw:3 * gw].astype(BF16)
        kwt_ref[0] = rows_t[4 * gw:5 * gw].astype(BF16)
        v_sel = rows[:, 3 * gw:4 * gw]
        lane_grp = lax.broadcasted_iota(jnp.int32, v_sel.shape, 1) // HEAD_DIM
        vsa_ref[...] = jnp.concatenate([jnp.where(lane_grp == ONES_GROUP[0], 1.0, v_sel),
                                        jnp.where(lane_grp == ONES_GROUP[1], 1.0, v_sel)], axis=1).astype(BF16)
    u = _rms(x, gq_ref[...]).astype(BF16)
    q_ref[...] = (_dot(u, wq_ref[...]) * HEAD_DIM ** -0.5).astype(BF16)
    gate_ref[...] = jax.nn.sigmoid(_dot(u, wg_ref[...]) + bg_ref[...])


def _proj_layer(x, gkv, wkv, gq, wq, wg, bg, *, tm, seq_len=None):
    rows, d = x.shape
    n_all = wkv.shape[1]
    n_kv = 4 * N_KV_GROUPS * HEAD_DIM
    gw = N_KV_GROUPS * HEAD_DIM
    tok = lambda w: pl.BlockSpec((tm, w), lambda i: (i, 0))
    out_specs = [tok(n_kv), tok(n_all - n_kv), tok(n_all - n_kv // 2), tok(wq.shape[1]), tok(wg.shape[1])]
    out_shape = [jax.ShapeDtypeStruct((rows, n_kv), F32), jax.ShapeDtypeStruct((rows, n_all - n_kv), F32),
                 jax.ShapeDtypeStruct((rows, n_all - n_kv // 2), BF16), jax.ShapeDtypeStruct((rows, wq.shape[1]), BF16),
                 jax.ShapeDtypeStruct((rows, wg.shape[1]), F32)]
    if seq_len is not None:
        tiles = seq_len // tm
        kt = pl.BlockSpec((1, gw, tm), lambda i: (i // tiles, 0, i % tiles))
        ktw = lambda w: pl.BlockSpec((1, w, tm), lambda i: (i // tiles, 0, i % tiles))
        out_specs += [kt, kt, tok(2 * gw), ktw(n_kv), ktw(n_all - n_kv)]
        out_shape += [jax.ShapeDtypeStruct((rows // seq_len, gw, seq_len), BF16)] * 2
        out_shape += [jax.ShapeDtypeStruct((rows, 2 * gw), BF16),
                      jax.ShapeDtypeStruct((rows // seq_len, n_kv, seq_len), F32),
                      jax.ShapeDtypeStruct((rows // seq_len, n_all - n_kv, seq_len), F32)]
    return pl.pallas_call(
        _proj_kernel,
        grid=(rows // tm,),
        in_specs=[tok(d), _const_spec((1, d)), _const_spec(wkv.shape), _const_spec((1, d)), _const_spec(wq.shape),
                  _const_spec(wg.shape), _const_spec(bg.shape)],
        out_specs=out_specs,
        out_shape=out_shape,
        compiler_params=_cparams("arbitrary"),
        name="kv_q_proj",
    )(x, gkv, wkv, gq, wq, wg, bg)


CMP_GROUP_PAIR = 2 * HEAD_DIM


def _compress_kernel(*refs, n_pref, n_in):
    x_refs = refs[n_pref:n_pref + n_in]
    pe_ref, w1_ref, b1_ref, w2_ref, kc_ref, vc_ref, carry_sc, xs_sc = refs[n_pref + n_in:]
    step = pl.program_id(1)

    @pl.when(step == 0)
    def _():
        carry_sc[...] = jnp.zeros_like(carry_sc)

    set_w = N_KV_GROUPS * HEAD_DIM
    n_chunks = 2 * set_w // CMP_GROUP_PAIR
    page = x_refs[0].shape[2]
    halves = page // CMP_STRIDE
    for p, x_ref in enumerate(x_refs):
        x = x_ref[0].T
        for cc in range(n_chunks):
            xs_sc[p, cc] = x[:, cc * CMP_GROUP_PAIR:(cc + 1) * CMP_GROUP_PAIR]
    m = n_in * halves
    row = lax.broadcasted_iota(jnp.int32, (m, 2 * LANES), 0)
    for s, out_ref in enumerate((kc_ref, vc_ref)):
        for jp in range(set_w // CMP_GROUP_PAIR):
            cc = s * (set_w // CMP_GROUP_PAIR) + jp
            xj = jnp.concatenate(
                [jnp.concatenate([xs_sc[p, cc, pl.ds(tt, halves, stride=CMP_STRIDE), :] for tt in range(CMP_STRIDE)], axis=1)
                 for p in range(n_in)], axis=0)
            first = _dot((xj + pe_ref[s, 0]).astype(BF16), w1_ref[s, 0])
            second = _dot((xj + pe_ref[s, 1]).astype(BF16), w1_ref[s, 1])
            slot = s * 2 + jp
            prev = jnp.where(row == 0, carry_sc[slot, 0:1, :], pltpu.roll(first, 1, 0))
            carry_sc[slot, 0:1, :] = first[m - 1:m, :]
            hid = jax.nn.gelu(prev + second + b1_ref[s])
            out_ref[0, :, jp * CMP_GROUP_PAIR:(jp + 1) * CMP_GROUP_PAIR] = _dot(hid.astype(BF16), w2_ref[s]).astype(BF16)


CMP_PAGES_PER_STEP = 32


def _compress(src, cw, *, n_seq, seq_pages, page, page_table=None):
    n_in = CMP_PAGES_PER_STEP
    assert seq_pages % n_in == 0
    set_w = N_KV_GROUPS * HEAD_DIM
    halves = page // CMP_STRIDE
    m = n_in * halves
    steps = seq_pages // n_in
    block = (1, 2 * set_w, page)
    if page_table is None:
        x_specs = [pl.BlockSpec(block, functools.partial(lambda b, s, p: (b, 0, s * n_in + p), p=p)) for p in range(n_in)]
        n_pref = 0
    else:
        x_specs = [pl.BlockSpec(block, functools.partial(lambda b, s, pt, p: (pt[b, s * n_in + p], 0, 0), p=p))
                   for p in range(n_in)]
        n_pref = 1
    cidx = lambda nd: (lambda *_: (0,) * nd)
    const = lambda a: pl.BlockSpec(a.shape, cidx(a.ndim), pipeline_mode=pl.Buffered(1))
    out_spec = pl.BlockSpec((1, m, set_w), lambda b, s, *_: (b, s, 0))
    grid_spec = pltpu.PrefetchScalarGridSpec(
        num_scalar_prefetch=n_pref,
        grid=(n_seq, steps),
        in_specs=x_specs + [const(cw['pe']), const(cw['w1']), const(cw['b1']), const(cw['w2'])],
        out_specs=[out_spec, out_spec],
        scratch_shapes=[pltpu.VMEM((4, SUBLANES, 2 * LANES), F32),
                        pltpu.VMEM((n_in, 2 * set_w // CMP_GROUP_PAIR, page, CMP_GROUP_PAIR), F32)],
    )
    call = pl.pallas_call(
        functools.partial(_compress_kernel, n_pref=n_pref, n_in=n_in),
        grid_spec=grid_spec,
        out_shape=[jax.ShapeDtypeStruct((n_seq, seq_pages * halves, set_w), BF16)] * 2,
        compiler_params=_cparams("arbitrary", "arbitrary"),
        name="compress",
    )
    args = ([page_table] if page_table is not None else []) + [src] * n_in + [cw['pe'], cw['w1'], cw['b1'], cw['w2']]
    return call(*args)


KEY_TILE = 128
N_NEAR_TILES = REL_MAX_DIST // KEY_TILE + 1
TILE_FUTURE, TILE_FAR, TILE_WINDOW = 0, N_NEAR_TILES + 1, N_NEAR_TILES + 2


def _rel_bucket(dist):
    n = jnp.maximum(dist, 0)
    max_exact = NUM_BUCKETS // 2
    nf = jnp.maximum(n, 1).astype(F32)
    large = max_exact + (jnp.log(nf / max_exact) / math.log(REL_MAX_DIST / max_exact)
                         * (NUM_BUCKETS - max_exact)).astype(jnp.int32)
    return jnp.where(n < max_exact, n, jnp.minimum(large, NUM_BUCKETS - 1))


BIAS_LEFT = 2 * LANES


def _bias_by_distance(table, reach):
    bd = table[_rel_bucket(jnp.arange(REL_MAX_DIST))].T.astype(F32)
    heads = bd.shape[0]
    return jnp.concatenate([jnp.zeros((heads, BIAS_LEFT), F32), bd,
                            jnp.broadcast_to(bd[:, -1:], (heads, reach - REL_MAX_DIST))], axis=1)


def _bias_toeplitz(v, c, n_rows, n_cols, col_step=1):
    heads = v.shape[0]
    c += BIAS_LEFT
    width = (n_cols - 1) * col_step + 1
    period = n_rows + width
    assert c - width + 1 >= 0 and c + n_rows <= v.shape[1]
    b = jnp.concatenate([v[:, c - width + 1:c + 1][:, ::-1], jnp.zeros((heads, 1), F32), v[:, c + 1:c + n_rows][:, ::-1]],
                        axis=1)
    m = jnp.tile(b, (1, n_rows))[:, :n_rows * (period - 1)].reshape(heads, n_rows, period - 1)
    return m[:, :, :width:col_step]


CMP_BAND_W = 2 * LANES
CMP_BAND_E0 = Q_BLOCK // CMP_STRIDE - CMP_BAND_W


def _compressed_band(table):
    i = np.arange(Q_BLOCK)[:, None]
    e = CMP_BAND_E0 + np.arange(CMP_BAND_W)[None, :]
    dist = i - CMP_STRIDE * e - (CMP_STRIDE - 1)
    assert -CMP_STRIDE * (CMP_BAND_E0 - 1) - (CMP_STRIDE - 1) >= REL_MAX_DIST
    c0 = -CMP_STRIDE * CMP_BAND_E0 - (CMP_STRIDE - 1)
    v = _bias_by_distance(table, c0 + Q_BLOCK)
    far = v[:, -1][:, None, None]
    return jnp.where(dist >= 0, _bias_toeplitz(v, c0, Q_BLOCK, CMP_BAND_W, CMP_STRIDE) - far, 0.0)


def _toeplitz_tiles(table):
    i = np.arange(Q_BLOCK)[:, None]
    j = np.arange(KEY_TILE)[None, :]
    offs = [-1] + list(range(N_NEAR_TILES)) + [N_NEAR_TILES + REL_MAX_DIST // KEY_TILE, WINDOW // KEY_TILE]
    v = _bias_by_distance(table, KEY_TILE * (max(offs) + 1))
    far = v[:, -1][:, None, None]
    tiles = []
    for idx, off in enumerate(offs):
        dist = KEY_TILE * off + i - j
        ok = (dist >= 0) & ((idx != TILE_WINDOW) | (dist < WINDOW))
        tiles.append(jnp.where(ok, _bias_toeplitz(v, KEY_TILE * off, Q_BLOCK, KEY_TILE) - far, NEG_INF))
    return jnp.stack(tiles)


SEL_CHUNK = 1024
ONES_GROUP = (1, 0)
WIN_KEYS = WINDOW + Q_BLOCK


def _masked_softmax(s, mask):
    s = jnp.where(mask, s, NEG_INF)
    e = jnp.exp(s - jnp.max(s, axis=-1, keepdims=True))
    p = e / jnp.sum(e, axis=-1, keepdims=True)
    return jnp.where(mask, p, 0.0)


def _topk_masks_t(vals, key_sc, corr_sc, n_rows):
    shape = vals[0].shape
    for i, v in enumerate(vals):
        key_sc[i] = v
    corr_sc[...] = jnp.zeros_like(corr_sc)
    jg = lax.broadcasted_iota(jnp.int32, shape, 0) // SUBLANES
    jj = lax.broadcasted_iota(jnp.int32, (SUBLANES, shape[1]), 0)

    def group_body(kg, cnts):
        base = pl.multiple_of(kg * SUBLANES, SUBLANES)
        later = jnp.where(jg > kg, 1, 0)
        out = []
        for i, cnt in enumerate(cnts):
            key = key_sc[i]
            diag = key_sc[i, pl.ds(base, SUBLANES), :]
            corr = jnp.zeros(diag.shape, jnp.int32)
            for kk in range(SUBLANES):
                k_row = key_sc[i, pl.ds(base + kk, 1), :]
                k_all = jnp.broadcast_to(k_row, shape)
                cnt = cnt + jnp.where(k_all > key, 1, jnp.where(k_all == key, later, 0))
                corr = corr + jnp.where(jnp.broadcast_to(k_row, diag.shape) == diag, jnp.where(jj > kk, 1, 0), 0)
            corr_sc[i, pl.ds(base, SUBLANES), :] = corr
            out.append(cnt)
        return tuple(out)

    cnts = lax.fori_loop(0, (n_rows + SUBLANES - 1) // SUBLANES, group_body,
                         tuple(jnp.zeros(shape, jnp.int32) for _ in vals))
    return [jnp.where(cnt + corr_sc[i] < N_SELECT, 1.0, 0.0) for i, cnt in enumerate(cnts)]


def _nsa_prompt_kernel(q_ref, gate_ref, kct_ref, vc_ref, ksa_ref, vsa_ref, kwt_ref, vw_ref, tb_ref, cb_ref,
                       selmap_ref, eg_ref, o_ref, key_sc, corr_sc, unsel_sc, ocmp_sc, osel_sc, m_sc,
                       acc_sc, s0_sc, s1_sc):
    qb = pl.program_id(1)
    t0 = qb * Q_BLOCK
    gw = N_KV_GROUPS * HEAD_DIM
    n_cmp = kct_ref.shape[2]
    rows = GROUP_SIZE * Q_BLOCK

    qi = lax.broadcasted_iota(jnp.int32, (Q_BLOCK, n_cmp), 0)
    hc = lax.broadcasted_iota(jnp.int32, (Q_BLOCK, n_cmp), 1)
    mask_c = jnp.where(hc >= 1, t0 + qi - (CMP_STRIDE * hc + CMP_STRIDE - 1), -1) >= 0
    first_row = qb * (Q_BLOCK // CMP_STRIDE)
    band_tile = (first_row + Q_BLOCK // CMP_STRIDE - 1) // LANES
    band_shift = (first_row % LANES + LANES + CMP_BAND_E0 + CMP_BAND_W) % CMP_BAND_W

    def compressed_bias(head):
        band = pltpu.roll(cb_ref[head], band_shift, 1)
        lo, hi = band[:, :LANES], band[:, LANES:]
        return jnp.concatenate([jnp.where(band_tile == c, hi, jnp.where(band_tile - 1 == c, lo, 0.0))
                                for c in range(n_cmp // LANES)], axis=1)

    lane_g = lax.broadcasted_iota(jnp.int32, (Q_BLOCK, gw), 1) // HEAD_DIM

    def group_queries(g):
        return jnp.concatenate(
            [jnp.where(lane_g == g, q_ref[0, :, r * gw:(r + 1) * gw], 0.0).astype(BF16) for r in range(GROUP_SIZE)], axis=0)

    nb = selmap_ref.shape[0]
    jb = lax.broadcasted_iota(jnp.int32, (nb, Q_BLOCK), 0)
    qpos_t = t0 + lax.broadcasted_iota(jnp.int32, (nb, Q_BLOCK), 1)
    cur = qpos_t // SEL_BLOCK
    forced = jnp.where(jb == 0, 1, 0) + jnp.where(jb == cur, 1, 0) + jnp.where(jb == cur - 1, 1, 0)
    valid_t = jb * SEL_BLOCK <= qpos_t
    n_valid_blocks = (t0 + Q_BLOCK - 1) // SEL_BLOCK + 1
    n_chunks = (t0 + Q_BLOCK - 1) // SEL_CHUNK + 1

    importance = []
    for g in range(N_KV_GROUPS):
        s_c = _dot(group_queries(g), kct_ref[0])
        p_rows = []
        for r in range(GROUP_SIZE):
            bias = compressed_bias(g * GROUP_SIZE + r)
            p_rows.append(_masked_softmax(s_c[r * Q_BLOCK:(r + 1) * Q_BLOCK] + bias, mask_c).astype(BF16))
        ocmp_sc[g] = _dot(jnp.concatenate(p_rows, axis=0), vc_ref[0])
        imp = jnp.zeros((nb, Q_BLOCK), F32)
        for r in range(GROUP_SIZE):
            imp = imp + lax.dot_general(selmap_ref[...], p_rows[r], (((1,), (1,)), ((), ())), preferred_element_type=F32)
        importance.append(jnp.where(forced > 0, 1e9, jnp.where(valid_t, imp, -1e9)))

    for g, mask_t in enumerate(_topk_masks_t(importance, key_sc, corr_sc, n_valid_blocks)):
        unsel_sc[g] = (1.0 - mask_t.T).astype(BF16)

    def group_body(g, carry):
        qm = group_queries(g)
        unsel = unsel_sc[g]

        m_sc[...] = jnp.full_like(m_sc, NEG_INF)
        acc_sc[...] = jnp.zeros_like(acc_sc)
        ones_at = jnp.where(g == ONES_GROUP[0], 1, 0)
        v_col = pl.multiple_of(ones_at * gw, gw)
        tiles_per_chunk = SEL_CHUNK // KEY_TILE

        def scores(c):
            return _dot_split_rows(qm, ksa_ref[0, :gw, pl.ds(pl.multiple_of(c * SEL_CHUNK, SEL_CHUNK), SEL_CHUNK)])

        def sweep_step(c, s_cur, s_next, near):
            off = pl.multiple_of(c * SEL_CHUNK, SEL_CHUNK)
            masked = _dot(unsel, ksa_ref[0, gw:, pl.ds(off, SEL_CHUNK)])
            s_next[...] = scores(jnp.minimum(c + 1, n_chunks - 1))
            parts = []
            for r in range(GROUP_SIZE):
                extra = masked
                if near:
                    extra = masked + jnp.concatenate(
                        [tb_ref[jnp.clip(qb - (c * tiles_per_chunk + h), -1, N_NEAR_TILES) + 1, g * GROUP_SIZE + r]
                         for h in range(tiles_per_chunk)], axis=1)
                parts.append(s_cur[r * Q_BLOCK:(r + 1) * Q_BLOCK, :] + extra)
            s = jnp.concatenate(parts, axis=0)
            m_old = m_sc[...]
            m_new = jnp.maximum(m_old, jnp.max(s, axis=-1, keepdims=True))
            alpha = jnp.exp(m_old - m_new)
            p = jnp.exp(s - jnp.concatenate([m_new] * (SEL_CHUNK // LANES), axis=1))
            pv = _dot_split_rows(p.astype(BF16), vsa_ref[0, pl.ds(off, SEL_CHUNK), pl.ds(v_col, gw)])
            acc_sc[...] = jnp.concatenate([alpha] * (gw // LANES), axis=1) * acc_sc[...] + pv
            m_sc[...] = m_new

        s0_sc[...] = scores(0)

        n_far = jnp.maximum((qb - N_NEAR_TILES + 1) // tiles_per_chunk, 0)

        def step(c, s_cur, s_next):
            @pl.when(c < n_far)
            def _():
                sweep_step(c, s_cur, s_next, False)

            @pl.when((c >= n_far) & (c < n_chunks))
            def _():
                sweep_step(c, s_cur, s_next, True)

        def pair_body(pair, carry2):
            step(2 * pair, s0_sc, s1_sc)
            step(2 * pair + 1, s1_sc, s0_sc)
            return carry2

        lax.fori_loop(0, (n_chunks + 1) // 2, pair_body, 0)
        acc = acc_sc[...]
        denom = jnp.where(ones_at == 1, acc[:, ONES_GROUP[1] * HEAD_DIM:ONES_GROUP[1] * HEAD_DIM + 1],
                          acc[:, ONES_GROUP[0] * HEAD_DIM:ONES_GROUP[0] * HEAD_DIM + 1])
        osel_sc[g] = acc / denom
        return carry

    lax.fori_loop(0, N_KV_GROUPS, group_body, 0)

    n_wt = WIN_KEYS // KEY_TILE
    w_off = [pl.multiple_of(jnp.maximum(qb - (n_wt - 1) + w, 0) * KEY_TILE, KEY_TILE) for w in range(n_wt)]
    w_tile = [jnp.where(qb - (n_wt - 1) + w >= 0, TILE_WINDOW if w == 0 else n_wt - w, TILE_FUTURE) for w in range(n_wt)]
    k_win = jnp.concatenate([kwt_ref[0, :, pl.ds(o, KEY_TILE)] for o in w_off], axis=1)
    v_win = jnp.concatenate([vw_ref[0, pl.ds(o, KEY_TILE), :] for o in w_off], axis=0)
    gexp = jnp.dot(gate_ref[0], eg_ref[...], preferred_element_type=F32, precision=lax.Precision.HIGHEST)
    out = [jnp.zeros((Q_BLOCK, gw), F32) for _ in range(GROUP_SIZE)]
    for g in range(N_KV_GROUPS):
        s_w = _dot(group_queries(g), k_win)
        pw_rows = []
        for r in range(GROUP_SIZE):
            bias = jnp.concatenate([tb_ref[ti, g * GROUP_SIZE + r] for ti in w_tile], axis=1)
            s_r = s_w[r * Q_BLOCK:(r + 1) * Q_BLOCK] + bias
            e = jnp.exp(s_r - jnp.max(s_r, axis=-1, keepdims=True))
            pw_rows.append((e / jnp.sum(e, axis=-1, keepdims=True)).astype(BF16))
        o_win = _dot(jnp.concatenate(pw_rows, axis=0), v_win)
        for r in range(GROUP_SIZE):
            rs = slice(r * Q_BLOCK, (r + 1) * Q_BLOCK)
            mix = (gexp[:, (3 * r) * gw:(3 * r + 1) * gw] * ocmp_sc[g, rs, :]
                   + gexp[:, (3 * r + 1) * gw:(3 * r + 2) * gw] * osel_sc[g, rs, :]
                   + gexp[:, (3 * r + 2) * gw:(3 * r + 3) * gw] * o_win[rs])
            out[r] = out[r] + jnp.where(lane_g == g, mix, 0.0)
    for r in range(GROUP_SIZE):
        o_ref[0, :, r * gw:(r + 1) * gw] = out[r].astype(BF16)


def _nsa_prompt(q, gates, kct, vc, ksa, vsa, kwt, selwin, tiles, band, selmap, eg):
    n, t, hw = q.shape
    gw = N_KV_GROUPS * HEAD_DIM
    n_cmp = kct.shape[2]
    rows = GROUP_SIZE * Q_BLOCK
    per_n = lambda shape, col=0: pl.BlockSpec(shape, lambda b, j: (b, 0, col), pipeline_mode=pl.Buffered(1))
    blk = lambda w: pl.BlockSpec((1, Q_BLOCK, w), lambda b, j: (b, j, 0))
    return pl.pallas_call(
        _nsa_prompt_kernel,
        grid=(n, t // Q_BLOCK),
        in_specs=[blk(hw), blk(gates.shape[2]),
                  per_n((1, gw, n_cmp)), per_n((1, n_cmp, gw)), per_n((1, ksa.shape[1], t)), per_n((1, t, 2 * gw)),
                  per_n((1, gw, t)), per_n((1, t, gw), 3),
                  _const_spec(tiles.shape), _const_spec(band.shape), _const_spec(selmap.shape), _const_spec(eg.shape)],
        out_specs=blk(hw),
        out_shape=jax.ShapeDtypeStruct((n, t, hw), BF16),
        scratch_shapes=[pltpu.VMEM((N_KV_GROUPS, selmap.shape[0], Q_BLOCK), F32),
                        pltpu.VMEM((N_KV_GROUPS, selmap.shape[0], Q_BLOCK), jnp.int32),
                        pltpu.VMEM((N_KV_GROUPS, Q_BLOCK, selmap.shape[0]), BF16),
                        pltpu.VMEM((N_KV_GROUPS, rows, gw), F32), pltpu.VMEM((N_KV_GROUPS, rows, gw), F32),
                        pltpu.VMEM((rows, LANES), F32), pltpu.VMEM((rows, gw), F32),
                        pltpu.VMEM((rows, SEL_CHUNK), F32), pltpu.VMEM((rows, SEL_CHUNK), F32)],
        compiler_params=_cparams("arbitrary", "arbitrary"),
        name="nsa_prompt",
    )(q, gates, kct, vc, ksa, vsa, kwt, selwin, tiles, band, selmap, eg)


DEC_T_PAD = SUBLANES
SAMPLE_BLOCKS_PAD = 256
NEW_KEYS_PAD = LANES


def _nsa_sample_kernel(*refs, n_pages, n_blocks):
    pt_ref, q_ref, gate_ref, kct_ref, vc_ref = refs[:5]
    page_refs = refs[5:5 + n_pages]
    (bsel_ref, bcmp_ref, bwin_ref, bnew_ref, new_ref, cwin_ref, selmap_ref, eg_ref, o_ref,
     key_sc, corr_sc, sel_sc, m_sc, l_sc, acc_sc, oc_sc) = refs[5 + n_pages:]
    c = pl.program_id(1)
    gw = N_KV_GROUPS * HEAD_DIM
    rows = N_HEADS * DEC_T_PAD
    page = page_refs[0].shape[2]
    lane_g = lax.broadcasted_iota(jnp.int32, (DEC_T_PAD, gw), 1) // HEAD_DIM
    qm = jnp.concatenate(
        [jnp.where(lane_g == g, q_ref[0, :, r * gw:(r + 1) * gw], 0.0)
         for r in range(GROUP_SIZE) for g in range(N_KV_GROUPS)], axis=0).astype(BF16)
    nt = (((1,), (1,)), ((), ()))

    @pl.when(c == 0)
    def _():
        p_c = _masked_softmax(_dot(qm, kct_ref[0]) + bcmp_ref[...], bcmp_ref[...] > 0.5 * NEG_INF).astype(BF16)
        oc_sc[...] = _dot(p_c, vc_ref[0])
        imp_all = lax.dot_general(selmap_ref[...], p_c, nt, preferred_element_type=F32)
        per_r = rows // GROUP_SIZE
        imp = imp_all
        for r in range(1, GROUP_SIZE):
            imp = imp + pltpu.roll(imp_all, r * per_r, 1)
        jb = lax.broadcasted_iota(jnp.int32, imp.shape, 0)
        last = n_blocks - 1
        forced = jnp.where(jb == 0, 1, 0) + jnp.where(jb == last, 1, 0) + jnp.where(jb == last - 1, 1, 0)
        imp = jnp.where(forced > 0, 1e9, jnp.where(jb <= last, imp, -1e9))
        sel_sc[...] = _topk_masks_t([imp], key_sc, corr_sc, n_blocks)[0].T.astype(BF16)
        m_sc[...] = jnp.full_like(m_sc, NEG_INF)
        l_sc[...] = jnp.zeros_like(l_sc)
        acc_sc[...] = jnp.zeros_like(acc_sc)

    def online_update(s, pv_of):
        m_old = m_sc[...]
        m_new = jnp.maximum(m_old, jnp.max(s, axis=-1, keepdims=True))
        alpha = jnp.exp(m_old - m_new)
        p = jnp.exp(s - m_new)
        l_sc[...] = alpha * l_sc[...] + jnp.sum(p, axis=-1, keepdims=True)
        acc_sc[...] = alpha * acc_sc[...] + pv_of(p.astype(BF16))
        m_sc[...] = m_new

    keys = n_pages * page
    s = jnp.concatenate([_dot(qm, pr[0, :gw, :].astype(BF16)) for pr in page_refs], axis=1)
    eb = lax.broadcasted_iota(jnp.int32, (SAMPLE_BLOCKS_PAD, keys), 0)
    ek = lax.broadcasted_iota(jnp.int32, (SAMPLE_BLOCKS_PAD, keys), 1)
    expand = jnp.where(eb == c * (keys // SEL_BLOCK) + ek // SEL_BLOCK, 1.0, 0.0).astype(BF16)
    picked = _dot(sel_sc[...], expand) > 0.5
    v_t = jnp.concatenate([pr[0, gw:, :].astype(BF16) for pr in page_refs], axis=1)
    online_update(jnp.where(picked, s + bsel_ref[...], NEG_INF),
                  lambda p: lax.dot_general(p, v_t, nt, preferred_element_type=F32))

    @pl.when(c == pl.num_programs(1) - 1)
    def _():
        s_new = lax.dot_general(qm, new_ref[0, :, :gw], nt, preferred_element_type=F32) + bnew_ref[0]
        online_update(s_new, lambda p: _dot(p, new_ref[0, :, gw:2 * gw]))
        o_sel = acc_sc[...] / l_sc[...]

        s_w = _dot(qm, cwin_ref[0, :gw, :].astype(BF16)) + bwin_ref[...]
        s_wn = lax.dot_general(qm, new_ref[0, :, 2 * gw:3 * gw], nt, preferred_element_type=F32) + bnew_ref[1]
        m_w = jnp.maximum(jnp.max(s_w, axis=-1, keepdims=True), jnp.max(s_wn, axis=-1, keepdims=True))
        e_w = jnp.exp(s_w - m_w)
        e_wn = jnp.exp(s_wn - m_w)
        l_w = jnp.sum(e_w, axis=-1, keepdims=True) + jnp.sum(e_wn, axis=-1, keepdims=True)
        o_win = lax.dot_general((e_w / l_w).astype(BF16), cwin_ref[0, gw:, :].astype(BF16), nt,
                                preferred_element_type=F32) + _dot((e_wn / l_w).astype(BF16), new_ref[0, :, 3 * gw:])

        gexp = jnp.dot(gate_ref[0], eg_ref[...], preferred_element_type=F32, precision=lax.Precision.HIGHEST)
        o_cmp = oc_sc[...]
        for r in range(GROUP_SIZE):
            out_r = jnp.zeros((DEC_T_PAD, gw), F32)
            for g in range(N_KV_GROUPS):
                rs = slice((r * N_KV_GROUPS + g) * DEC_T_PAD, (r * N_KV_GROUPS + g + 1) * DEC_T_PAD)
                mix = (gexp[:, (3 * r) * gw:(3 * r + 1) * gw] * o_cmp[rs]
                       + gexp[:, (3 * r + 1) * gw:(3 * r + 2) * gw] * o_sel[rs]
                       + gexp[:, (3 * r + 2) * gw:(3 * r + 3) * gw] * o_win[rs])
                out_r = out_r + jnp.where(lane_g == g, mix, 0.0)
            o_ref[0, :, r * gw:(r + 1) * gw] = out_r


PAGES_PER_STEP = 8


def _nsa_sample(page_table, q, gates, kct, vc, cache, bias, new_rows, cache_win, selmap, eg, n_blocks):
    n, _, hw = q.shape
    gw = N_KV_GROUPS * HEAD_DIM
    n_cmp = kct.shape[2]
    rows = N_HEADS * DEC_T_PAD
    page = cache.shape[2]
    steps = page_table.shape[1] // PAGES_PER_STEP
    keys = PAGES_PER_STEP * page
    per_n = lambda a: pl.BlockSpec((1,) + a.shape[1:], lambda b, c, pt: (b, 0, 0))
    cidx = lambda nd: (lambda *_: (0,) * nd)
    const = lambda a: pl.BlockSpec(a.shape, cidx(a.ndim), pipeline_mode=pl.Buffered(1))
    page_specs = [pl.BlockSpec((1, 2 * gw, page), functools.partial(
        lambda b, c, pt, p: (pt[b, c * PAGES_PER_STEP + p], 1, 0), p=p)) for p in range(PAGES_PER_STEP)]
    grid_spec = pltpu.PrefetchScalarGridSpec(
        num_scalar_prefetch=1,
        grid=(n, steps),
        in_specs=[per_n(q), per_n(gates), per_n(kct), per_n(vc)] + page_specs + [
            pl.BlockSpec((rows, keys), lambda b, c, pt: (0, c)), const(bias['cmp']), const(bias['win']), const(bias['new']),
            per_n(new_rows), per_n(cache_win), const(selmap), const(eg)],
        out_specs=pl.BlockSpec((1, DEC_T_PAD, hw), lambda b, c, pt: (b, 0, 0)),
        scratch_shapes=[pltpu.VMEM((1, SAMPLE_BLOCKS_PAD, rows), F32),
                        pltpu.VMEM((1, SAMPLE_BLOCKS_PAD, rows), jnp.int32),
                        pltpu.VMEM((rows, SAMPLE_BLOCKS_PAD), BF16),
                        pltpu.VMEM((rows, 1), F32), pltpu.VMEM((rows, 1), F32), pltpu.VMEM((rows, gw), F32),
                        pltpu.VMEM((rows, gw), F32)],
    )
    return pl.pallas_call(
        functools.partial(_nsa_sample_kernel, n_pages=PAGES_PER_STEP, n_blocks=n_blocks),
        grid_spec=grid_spec,
        out_shape=jax.ShapeDtypeStruct((n, DEC_T_PAD, hw), F32),
        compiler_params=_cparams("arbitrary", "arbitrary"),
        name="nsa_sample",
    )(page_table, q, gates, kct, vc, *([cache] * PAGES_PER_STEP), bias['sel'], bias['cmp'], bias['win'], bias['new'],
      new_rows, cache_win, selmap, eg)


def _sample_bias(table, past, dec_t, win_len, n_cmp_rows):
    t = np.arange(DEC_T_PAD)[:, None]
    v = _bias_by_distance(table, max(past + DEC_T_PAD, REL_MAX_DIST + 1))

    def rows(c, n_cols, col_step, ok):
        b = jnp.where(ok, _bias_toeplitz(v, c, DEC_T_PAD, n_cols, col_step), NEG_INF)
        b = b.reshape(N_KV_GROUPS, GROUP_SIZE, DEC_T_PAD, -1).transpose(1, 0, 2, 3)
        return b.reshape(N_HEADS * DEC_T_PAD, -1)

    h = np.arange(n_cmp_rows)[None, :]
    dist_c = past + t - (CMP_STRIDE * h + CMP_STRIDE - 1)
    j = np.arange(win_len)[None, :]
    dist_w = win_len + t - j
    tn = np.arange(NEW_KEYS_PAD)[None, :]
    new = rows(0, NEW_KEYS_PAD, 1, (tn <= t) & (tn < dec_t))
    return dict(sel=rows(past, past, 1, np.ones((DEC_T_PAD, past), bool)),
                cmp=rows(past - (CMP_STRIDE - 1), n_cmp_rows, CMP_STRIDE, (h >= 1) & (dist_c >= 0)),
                win=rows(win_len, win_len, 1, (dist_w < WINDOW) & (past - win_len + j >= 0)), new=jnp.stack([new, new]))


GATE_LANES = LANES


def _prepare(p):
    d = p['norm_pre_mix'].shape[1]
    hw = N_HEADS * HEAD_DIM
    gw = N_KV_GROUPS * HEAD_DIM
    row = lambda v: v.reshape(1, -1)
    w = {}
    w['rg'] = dict(
        g_pre=row(p['norm_pre_mix'][0]), w_in=p['rg_w_in'][0].astype(BF16), conv_w=p['rg_conv_w'][0],
        conv_b=row(p['rg_conv_b'][0]),
        w_ax=jnp.concatenate([p['rg_w_a'][0], p['rg_w_x'][0]], axis=2).astype(BF16),
        b_ax=jnp.stack([p['rg_b_a'][0], p['rg_b_x'][0]]), lam=row(p['rg_lambda'][0]),
        w_out=p['rg_w_out'][0].astype(BF16), g_post=row(p['norm_post_mix'][0]))
    w['mlp'] = [dict(g1=row(p['norm_pre_mlp'][l]), wup=p['w_mlp_up'][l].astype(BF16),
                     wdn=p['w_mlp_down'][l].astype(BF16), g2=row(p['norm_post_mlp'][l])) for l in range(2)]
    wqg = p['nsa_w_qg'][0]
    n_gate = wqg.shape[1] - hw
    w['proj'] = dict(
        gkv=row(p['kv_norm']), wkv=p['w_kv'].astype(BF16), gq=row(p['norm_pre_mix'][1]),
        wq=wqg[:, :hw].reshape(d, N_KV_GROUPS, GROUP_SIZE, HEAD_DIM).transpose(0, 2, 1, 3).reshape(d, hw).astype(BF16),
        wg=jnp.pad(wqg[:, hw:], ((0, 0), (0, GATE_LANES - n_gate))).astype(BF16),
        bg=jnp.pad(p['nsa_b_g'][0], (0, GATE_LANES - n_gate)).reshape(1, -1))
    w['wo'] = p['nsa_w_o'][0].reshape(N_KV_GROUPS, GROUP_SIZE, HEAD_DIM, d).transpose(1, 0, 2, 3).reshape(hw, d).astype(BF16)
    w['g_post_mix1'] = row(p['norm_post_mix'][1])
    eye2 = jnp.eye(2, dtype=F32)
    half = L_CMP // CMP_STRIDE
    w['cmp'] = dict(
        pe=jnp.broadcast_to(p['cmp_pe'].reshape(2, half, CMP_STRIDE, 1, HEAD_DIM),
                            (2, half, CMP_STRIDE, 2, HEAD_DIM)).reshape(2, half, 1, CMP_STRIDE * CMP_GROUP_PAIR),
        w1=jnp.einsum('sftdc,gh->sftgdhc', p['cmp_w1'].reshape(2, half, CMP_STRIDE, HEAD_DIM, -1), eye2)
        .reshape(2, half, CMP_STRIDE * CMP_GROUP_PAIR, -1).astype(BF16),
        b1=jnp.tile(p['cmp_b1'], (1, 2)).reshape(2, 1, -1),
        w2=jnp.einsum('scd,gh->sgchd', p['cmp_w2'], eye2).reshape(2, -1, CMP_GROUP_PAIR).astype(BF16))
    w['table'] = p['rel_bias_table']
    w['tiles'] = _toeplitz_tiles(p['rel_bias_table'])
    w['band'] = _compressed_band(p['rel_bias_table'])
    h_i = jnp.arange(N_HEADS)
    src = (h_i[:, None] * 3 + jnp.arange(3)[None, :]).reshape(-1)
    dst0 = (((h_i % GROUP_SIZE)[:, None] * 3 + jnp.arange(3)[None, :]) * gw
            + (h_i // GROUP_SIZE)[:, None] * HEAD_DIM).reshape(-1)
    lanes = jnp.arange(3 * GROUP_SIZE * gw)
    w['eg'] = jnp.zeros((GATE_LANES, lanes.shape[0]), F32).at[src].set(
        ((lanes[None, :] >= dst0[:, None]) & (lanes[None, :] < dst0[:, None] + HEAD_DIM)).astype(F32))
    return w


def _selmap_t(n_blocks, n_cmp_rows):
    ratio = SEL_BLOCK // CMP_STRIDE
    h = jnp.arange(n_cmp_rows)[None, :]
    j = jnp.arange(n_blocks)[:, None]
    m = ((h - 1) // ratio == j).astype(F32) + (h // ratio == j).astype(F32)
    return jnp.where(h >= 1, m, 0.0).astype(BF16)


def _trunk_prompt(x, w):
    n, t, d = x.shape
    gw = N_KV_GROUPS * HEAD_DIM
    zeros = lambda r: jnp.zeros((n, r, d), F32)
    x1, h_last, c_last = _rglru_layer(x, zeros(1), zeros(CONV_W - 1), w['rg'], stride=1, tm=256)
    m0 = w['mlp'][0]
    x2 = _mlp_layer(x1.reshape(n * t, d), m0['g1'], m0['wup'], m0['wdn'], m0['g2'], tm=512)
    pj = w['proj']
    kv4, _, selwin, q, gates, kst, kwt, vsa, kv_t, win_t = _proj_layer(
        x2, pj['gkv'], pj['wkv'], pj['gq'], pj['wq'], pj['wg'], pj['bg'], tm=512, seq_len=t)
    nh = t // CMP_STRIDE
    n_blocks = t // SEL_BLOCK
    kc, vc = _compress(kv_t, w['cmp'], n_seq=n, seq_pages=t // KEY_TILE, page=KEY_TILE)
    selneg = jnp.where(jnp.arange(t)[None, :] // SEL_BLOCK == jnp.arange(n_blocks)[:, None], NEG_INF, 0.0).astype(BF16)
    ksa = jnp.concatenate([kst, jnp.broadcast_to(selneg, (n,) + selneg.shape)], axis=1)
    attn = _nsa_prompt(q.reshape(n, t, -1), gates.reshape(n, t, -1), kc.transpose(0, 2, 1), vc, ksa, vsa.reshape(n, t, -1),
                       kwt, selwin.reshape(n, t, -1), w['tiles'], w['band'], _selmap_t(n_blocks, nh), w['eg'])
    m1 = w['mlp'][1]
    y = _mlp_layer(x2, m1['g1'], m1['wup'], m1['wdn'], m1['g2'], tm=512,
                   mix=(attn.reshape(n * t, -1), w['wo'], w['g_post_mix1']))
    wlen = min(WINDOW, t)
    rows_of = lambda a, sets: a.reshape(n, sets, N_KV_GROUPS, HEAD_DIM, -1).transpose(0, 4, 1, 2, 3)
    return (y.reshape(n, t, d), h_last.reshape(1, n, d), c_last.reshape(1, n, CONV_W - 1, d),
            rows_of(kv_t, 4), rows_of(win_t[:, :, t - wlen:], 2))


def _trunk_sample(x, h0, c0, cache_kv, cache_win, page_table, w):
    n, t, d = x.shape
    n_phys, page = cache_kv.shape[:2]
    past = page_table.shape[1] * page
    win_len = cache_win.shape[1]
    assert t < CMP_STRIDE and past % CMP_STRIDE == 0 and t <= DEC_T_PAD and page % CMP_STRIDE == 0
    assert page_table.shape[1] % PAGES_PER_STEP == 0 and page == KEY_TILE
    to_tn = lambda a: a.transpose(1, 0, 2).reshape(a.shape[0] * a.shape[1], -1)
    to_nt = lambda a, k: a.reshape(k, n, -1).transpose(1, 0, 2)
    tail = CONV_W - 1
    x1, h_last, c_last = _rglru_layer(to_tn(x)[None], h0[None], to_tn(c0)[None], w['rg'], stride=n, tm=t * n)
    m0 = w['mlp'][0]
    x2 = _mlp_layer(x1[0], m0['g1'], m0['wup'], m0['wdn'], m0['g2'], tm=t * n)
    pj = w['proj']
    kv4, win, selwin, q, gates = _proj_layer(x2, pj['gkv'], pj['wkv'], pj['gq'], pj['wq'], pj['wg'], pj['bg'], tm=t * n)

    row_w = cache_kv.shape[2] * cache_kv.shape[3] * cache_kv.shape[4]
    nh = past // CMP_STRIDE
    cache_t = cache_kv.reshape(n_phys, page, row_w).transpose(0, 2, 1)
    cache_win_t = cache_win.reshape(n, win_len, -1).transpose(0, 2, 1)
    kc, vc = _compress(cache_t, w['cmp'], n_seq=n, seq_pages=page_table.shape[1], page=page, page_table=page_table)
    pad_t = lambda a, k: jnp.pad(a, ((0, 0), (0, k - a.shape[1]), (0, 0)))
    n_blocks = -(-(past + t) // SEL_BLOCK)
    attn = _nsa_sample(
        page_table, pad_t(to_nt(q, t).astype(F32), DEC_T_PAD), pad_t(to_nt(gates, t), DEC_T_PAD), kc.transpose(0, 2, 1), vc,
        cache_t, _sample_bias(w['table'], past, t, win_len, nh),
        pad_t(to_nt(selwin, t), NEW_KEYS_PAD), cache_win_t, _selmap_t(SAMPLE_BLOCKS_PAD, nh), w['eg'], n_blocks)
    m1 = w['mlp'][1]
    y = _mlp_layer(x2, m1['g1'], m1['wup'], m1['wdn'], m1['g2'], tm=t * n,
                   mix=(to_tn(attn[:, :t]).astype(BF16), w['wo'], w['g_post_mix1']))
    win_rows = to_nt(win, t).reshape(n, t, 2, N_KV_GROUPS, HEAD_DIM)
    win_new = jnp.concatenate([cache_win, win_rows], axis=1)[:, t:]
    return (to_nt(y, t), h_last, to_nt(c_last[0], tail)[None],
            to_nt(kv4, t).reshape(n, t, 4, N_KV_GROUPS, HEAD_DIM), win_new)


def kernel(x_prompt, x_sample, state_rglru_h, state_rglru_conv, cache_kv, cache_win, page_table, norm_pre_mix,
           norm_post_mix, norm_pre_mlp, norm_post_mlp, w_mlp_up, w_mlp_down, rg_w_in, rg_conv_w, rg_conv_b, rg_w_a, rg_b_a,
           rg_w_x, rg_b_x, rg_lambda, rg_w_out, kv_norm, w_kv, cmp_pe, cmp_w1, cmp_b1, cmp_w2, nsa_w_qg, nsa_b_g, nsa_w_o,
           rel_bias_table):
    w = _prepare(dict(
        norm_pre_mix=norm_pre_mix, norm_post_mix=norm_post_mix, norm_pre_mlp=norm_pre_mlp, norm_post_mlp=norm_post_mlp,
        w_mlp_up=w_mlp_up, w_mlp_down=w_mlp_down, rg_w_in=rg_w_in, rg_conv_w=rg_conv_w, rg_conv_b=rg_conv_b,
        rg_w_a=rg_w_a, rg_b_a=rg_b_a, rg_w_x=rg_w_x, rg_b_x=rg_b_x, rg_lambda=rg_lambda, rg_w_out=rg_w_out,
        kv_norm=kv_norm, w_kv=w_kv, cmp_pe=cmp_pe, cmp_w1=cmp_w1, cmp_b1=cmp_b1, cmp_w2=cmp_w2, nsa_w_qg=nsa_w_qg,
        nsa_b_g=nsa_b_g, nsa_w_o=nsa_w_o, rel_bias_table=rel_bias_table))
    y_p, p_h, p_conv, p_kv, p_win = _trunk_prompt(x_prompt, w)
    y_s, s_h, s_conv, s_kv, s_win = _trunk_sample(x_sample, state_rglru_h[0], state_rglru_conv[0], cache_kv, cache_win,
                                                  page_table, w)
    return (y_p, y_s, p_h, p_conv, p_kv, p_win, s_h, s_conv, s_kv, s_win)
```

```python
import functools
import math

import jax
import jax.numpy as jnp
import numpy as np
from jax import lax
from jax.experimental import pallas as pl
from jax.experimental.pallas import tpu as pltpu

F32 = jnp.float32
BF16 = jnp.bfloat16

N_HEADS = 16
N_KV_GROUPS = 4
GROUP_SIZE = N_HEADS // N_KV_GROUPS
HEAD_DIM = 64
N_RNN_BLOCKS = 4
CONV_W = 4
LRU_C = 8.0
L_CMP = 32
CMP_STRIDE = 16
SEL_BLOCK = 64
N_SELECT = 16
WINDOW = 512
Q_BLOCK = 128
NUM_BUCKETS = 32
REL_MAX_DIST = 1024
EPS = 1e-6
NEG_INF = -1e30

SUBLANES = 8
LANES = 128
VMEM_LIMIT_BYTES = 56 * 1024 * 1024


def _cparams(*sem):
    return pltpu.CompilerParams(dimension_semantics=sem, vmem_limit_bytes=VMEM_LIMIT_BYTES)


def _const_spec(shape):
    nd = len(shape)
    return pl.BlockSpec(shape, lambda *_: (0,) * nd, pipeline_mode=pl.Buffered(1))


def _rms(x, g):
    return x * lax.rsqrt(jnp.mean(x * x, axis=-1, keepdims=True) + EPS) * g


def _dot(a, b):
    return jnp.dot(a, b, preferred_element_type=F32)


def _dot_split_rows(a, b):
    half = a.shape[0] // 2
    return jnp.concatenate([_dot(a[:half], b), _dot(a[half:], b)], axis=0)


def _rglru_kernel(x_ref, h0_ref, c0_ref, gpre_ref, win_ref, cw_ref, cb_ref, wax_ref, bax_ref, lam_ref,
                  wout_ref, gpost_ref, x1_ref, hlast_ref, clast_ref, xcat_sc, hprev_sc, *, stride, tm, pad):
    d = x_ref.shape[-1]
    tail = (CONV_W - 1) * stride
    j = pl.program_id(1)

    @pl.when(j == 0)
    def _():
        xcat_sc[pad - tail:pad, :] = c0_ref[0]
        hprev_sc[...] = jnp.zeros_like(hprev_sc)
        hprev_sc[tm - stride:tm, :] = h0_ref[0]

    x = x_ref[0]
    u = _rms(x, gpre_ref[...]).astype(BF16)
    proj = _dot(u, win_ref[...])
    gate = jax.nn.gelu(proj[:, :d])
    xcat_sc[pad:pad + tm, :] = proj[:, d:]
    xc = cb_ref[...] + cw_ref[CONV_W - 1:CONV_W, :] * xcat_sc[pad:pad + tm, :]
    for lag in range(1, CONV_W):
        xc = xc + cw_ref[CONV_W - 1 - lag:CONV_W - lag, :] * xcat_sc[pad - lag * stride:pad - lag * stride + tm, :]
    xcat_sc[pad - tail:pad, :] = xcat_sc[pad + tm - tail:pad + tm, :]
    clast_ref[0] = xcat_sc[pad - tail:pad, :]

    xcb = xc.astype(BF16)
    bw = d // N_RNN_BLOCKS
    ra, rx = [], []
    for blk in range(N_RNN_BLOCKS):
        pre = _dot(xcb[:, blk * bw:(blk + 1) * bw], wax_ref[blk])
        ra.append(pre[:, :bw])
        rx.append(pre[:, bw:])
    r = jax.nn.sigmoid(jnp.concatenate(ra, axis=1) + bax_ref[0:1, :])
    i = jax.nn.sigmoid(jnp.concatenate(rx, axis=1) + bax_ref[1:2, :])
    log_a = r * (-LRU_C * jax.nn.softplus(-lam_ref[...]))
    a = jnp.exp(log_a)
    b = jnp.sqrt(-jnp.tanh(log_a) * (a * a + 1.0)) * i * xc

    row = lax.broadcasted_iota(jnp.int32, (tm, d), 0)
    b = b + jnp.where(row < stride, a * pltpu.roll(hprev_sc[...], stride, 0), 0.0)
    s = stride
    while s < tm:
        keep = row >= s
        b = a * jnp.where(keep, pltpu.roll(b, s, 0), 0.0) + b
        a = a * jnp.where(keep, pltpu.roll(a, s, 0), 1.0)
        s *= 2
    h = b
    hprev_sc[...] = h
    hlast_ref[0] = h[tm - stride:tm, :]

    y = _dot((h * gate).astype(BF16), wout_ref[...])
    x1_ref[0] = x + _rms(y, gpost_ref[...])


def _rglru_layer(x, h0, c0, w, *, stride, tm):
    n, rows, d = x.shape
    tail = (CONV_W - 1) * stride
    pad = -(-tail // SUBLANES) * SUBLANES
    kern = functools.partial(_rglru_kernel, stride=stride, tm=tm, pad=pad)
    seq = lambda shape: pl.BlockSpec(shape, lambda b, j: (b, j, 0))
    per_n = lambda shape: pl.BlockSpec(shape, lambda b, j: (b, 0, 0))
    return pl.pallas_call(
        kern,
        grid=(n, rows // tm),
        in_specs=[seq((1, tm, d)), per_n((1, stride, d)), per_n((1, tail, d)),
                  _const_spec((1, d)), _const_spec(w['w_in'].shape), _const_spec((CONV_W, d)), _const_spec((1, d)),
                  _const_spec(w['w_ax'].shape), _const_spec((2, d)), _const_spec((1, d)),
                  _const_spec(w['w_out'].shape), _const_spec((1, d))],
        out_specs=[seq((1, tm, d)), per_n((1, stride, d)), per_n((1, tail, d))],
        out_shape=[jax.ShapeDtypeStruct((n, rows, d), F32), jax.ShapeDtypeStruct((n, stride, d), F32),
                   jax.ShapeDtypeStruct((n, tail, d), F32)],
        scratch_shapes=[pltpu.VMEM((pad + tm, d), F32), pltpu.VMEM((tm, d), F32)],
        compiler_params=_cparams("arbitrary", "arbitrary"),
        name="rglru_layer",
    )(x, h0, c0, w['g_pre'], w['w_in'], w['conv_w'], w['conv_b'], w['w_ax'], w['b_ax'], w['lam'], w['w_out'], w['g_post'])


MLP_HIDDEN_CHUNK = 1024


def _mlp_body(x, g1_ref, wup_ref, wdn_ref, g2_ref):
    u = _rms(x, g1_ref[...]).astype(BF16)
    f = wup_ref.shape[1]
    acc = jnp.zeros(x.shape, F32)
    for c in range(f // MLP_HIDDEN_CHUNK):
        cols = slice(c * MLP_HIDDEN_CHUNK, (c + 1) * MLP_HIDDEN_CHUNK)
        hid = jnp.maximum(_dot(u, wup_ref[:, cols]), 0.0)
        acc = acc + _dot((hid * hid).astype(BF16), wdn_ref[cols, :])
    return x + _rms(acc, g2_ref[...])


def _mlp_kernel(x_ref, g1_ref, wup_ref, wdn_ref, g2_ref, o_ref):
    o_ref[...] = _mlp_body(x_ref[...], g1_ref, wup_ref, wdn_ref, g2_ref)


def _mix_mlp_kernel(x_ref, a_ref, wo_ref, gmix_ref, g1_ref, wup_ref, wdn_ref, g2_ref, o_ref):
    x = x_ref[...] + _rms(_dot(a_ref[...], wo_ref[...]), gmix_ref[...])
    o_ref[...] = _mlp_body(x, g1_ref, wup_ref, wdn_ref, g2_ref)


def _mlp_layer(x, g1, wup, wdn, g2, *, tm, mix=None):
    rows, d = x.shape
    f = wup.shape[1]
    tok = pl.BlockSpec((tm, d), lambda i: (i, 0))
    mlp_specs = [_const_spec((1, d)), _const_spec((d, f)), _const_spec((f, d)), _const_spec((1, d))]
    if mix is None:
        kern, specs, args = _mlp_kernel, [tok] + mlp_specs, (x, g1, wup, wdn, g2)
    else:
        a, wo, gmix = mix
        kern = _mix_mlp_kernel
        specs = [tok, pl.BlockSpec((tm, a.shape[1]), lambda i: (i, 0)), _const_spec(wo.shape), _const_spec((1, d))] + mlp_specs
        args = (x, a, wo, gmix, g1, wup, wdn, g2)
    return pl.pallas_call(
        kern,
        grid=(rows // tm,),
        in_specs=specs,
        out_specs=tok,
        out_shape=jax.ShapeDtypeStruct((rows, d), F32),
        compiler_params=_cparams("arbitrary"),
        name="mlp_layer",
    )(*args)


def _proj_kernel(x_ref, gkv_ref, wkv_ref, gq_ref, wq_ref, wg_ref, bg_ref, kv_ref, win_ref, sw_ref, q_ref, gate_ref,
                 *kt_refs):
    x = x_ref[...]
    rows = _dot(_rms(x, gkv_ref[...]).astype(BF16), wkv_ref[...])
    n_kv = kv_ref.shape[1]
    gw = N_KV_GROUPS * HEAD_DIM
    kv_ref[...] = rows[:, :n_kv]
    win_ref[...] = rows[:, n_kv:]
    sw_ref[...] = rows[:, n_kv // 2:].astype(BF16)
    if kt_refs:
        kst_ref, kwt_ref, vsa_ref, kvt_ref, wint_ref = kt_refs
        rows_t = rows.T
        kvt_ref[0] = rows_t[:n_kv]
        wint_ref[0] = rows_t[n_kv:]
        kst_ref[0] = rows_t[2 * gw:3 * gw].astype(BF16)
        kwt_ref[0] = rows_t[4 * gw:5 * gw].astype(BF16)
        v_sel = rows[:, 3 * gw:4 * gw]
        lane_grp = lax.broadcasted_iota(jnp.int32, v_sel.shape, 1) // HEAD_DIM
        vsa_ref[...] = jnp.concatenate([jnp.where(lane_grp == ONES_GROUP[0], 1.0, v_sel),
                                        jnp.where(lane_grp == ONES_GROUP[1], 1.0, v_sel)], axis=1).astype(BF16)
    u = _rms(x, gq_ref[...]).astype(BF16)
    q_ref[...] = (_dot(u, wq_ref[...]) * HEAD_DIM ** -0.5).astype(BF16)
    gate_ref[...] = jax.nn.sigmoid(_dot(u, wg_ref[...]) + bg_ref[...])


def _proj_layer(x, gkv, wkv, gq, wq, wg, bg, *, tm, seq_len=None):
    rows, d = x.shape
    n_all = wkv.shape[1]
    n_kv = 4 * N_KV_GROUPS * HEAD_DIM
    gw = N_KV_GROUPS * HEAD_DIM
    tok = lambda w: pl.BlockSpec((tm, w), lambda i: (i, 0))
    out_specs = [tok(n_kv), tok(n_all - n_kv), tok(n_all - n_kv // 2), tok(wq.shape[1]), tok(wg.shape[1])]
    out_shape = [jax.ShapeDtypeStruct((rows, n_kv), F32), jax.ShapeDtypeStruct((rows, n_all - n_kv), F32),
                 jax.ShapeDtypeStruct((rows, n_all - n_kv // 2), BF16), jax.ShapeDtypeStruct((rows, wq.shape[1]), BF16),
                 jax.ShapeDtypeStruct((rows, wg.shape[1]), F32)]
    if seq_len is not None:
        tiles = seq_len // tm
        kt = pl.BlockSpec((1, gw, tm), lambda i: (i // tiles, 0, i % tiles))
        ktw = lambda w: pl.BlockSpec((1, w, tm), lambda i: (i // tiles, 0, i % tiles))
        out_specs += [kt, kt, tok(2 * gw), ktw(n_kv), ktw(n_all - n_kv)]
        out_shape += [jax.ShapeDtypeStruct((rows // seq_len, gw, seq_len), BF16)] * 2
        out_shape += [jax.ShapeDtypeStruct((rows, 2 * gw), BF16),
                      jax.ShapeDtypeStruct((rows // seq_len, n_kv, seq_len), F32),
                      jax.ShapeDtypeStruct((rows // seq_len, n_all - n_kv, seq_len), F32)]
    return pl.pallas_call(
        _proj_kernel,
        grid=(rows // tm,),
        in_specs=[tok(d), _const_spec((1, d)), _const_spec(wkv.shape), _const_spec((1, d)), _const_spec(wq.shape),
                  _const_spec(wg.shape), _const_spec(bg.shape)],
        out_specs=out_specs,
        out_shape=out_shape,
        compiler_params=_cparams("arbitrary"),
        name="kv_q_proj",
    )(x, gkv, wkv, gq, wq, wg, bg)


CMP_GROUP_PAIR = 2 * HEAD_DIM


def _compress_kernel(*refs, n_pref, n_in):
    x_refs = refs[n_pref:n_pref + n_in]
    pe_ref, w1_ref, b1_ref, w2_ref, kc_ref, vc_ref, carry_sc, xs_sc = refs[n_pref + n_in:]
    step = pl.program_id(1)

    @pl.when(step == 0)
    def _():
        carry_sc[...] = jnp.zeros_like(carry_sc)

    set_w = N_KV_GROUPS * HEAD_DIM
    n_chunks = 2 * set_w // CMP_GROUP_PAIR
    page = x_refs[0].shape[2]
    halves = page // CMP_STRIDE
    for p, x_ref in enumerate(x_refs):
        x = x_ref[0].T
        for cc in range(n_chunks):
            xs_sc[p, cc] = x[:, cc * CMP_GROUP_PAIR:(cc + 1) * CMP_GROUP_PAIR]
    m = n_in * halves
    row = lax.broadcasted_iota(jnp.int32, (m, 2 * LANES), 0)
    for s, out_ref in enumerate((kc_ref, vc_ref)):
        for jp in range(set_w // CMP_GROUP_PAIR):
            cc = s * (set_w // CMP_GROUP_PAIR) + jp
            xj = jnp.concatenate(
                [jnp.concatenate([xs_sc[p, cc, pl.ds(tt, halves, stride=CMP_STRIDE), :] for tt in range(CMP_STRIDE)], axis=1)
                 for p in range(n_in)], axis=0)
            first = _dot((xj + pe_ref[s, 0]).astype(BF16), w1_ref[s, 0])
            second = _dot((xj + pe_ref[s, 1]).astype(BF16), w1_ref[s, 1])
            slot = s * 2 + jp
            prev = jnp.where(row == 0, carry_sc[slot, 0:1, :], pltpu.roll(first, 1, 0))
            carry_sc[slot, 0:1, :] = first[m - 1:m, :]
            hid = jax.nn.gelu(prev + second + b1_ref[s])
            out_ref[0, :, jp * CMP_GROUP_PAIR:(jp + 1) * CMP_GROUP_PAIR] = _dot(hid.astype(BF16), w2_ref[s]).astype(BF16)


CMP_PAGES_PER_STEP = 32


def _compress(src, cw, *, n_seq, seq_pages, page, page_table=None):
    n_in = CMP_PAGES_PER_STEP
    assert seq_pages % n_in == 0
    set_w = N_KV_GROUPS * HEAD_DIM
    halves = page // CMP_STRIDE
    m = n_in * halves
    steps = seq_pages // n_in
    block = (1, 2 * set_w, page)
    if page_table is None:
        x_specs = [pl.BlockSpec(block, functools.partial(lambda b, s, p: (b, 0, s * n_in + p), p=p)) for p in range(n_in)]
        n_pref = 0
    else:
        x_specs = [pl.BlockSpec(block, functools.partial(lambda b, s, pt, p: (pt[b, s * n_in + p], 0, 0), p=p))
                   for p in range(n_in)]
        n_pref = 1
    cidx = lambda nd: (lambda *_: (0,) * nd)
    const = lambda a: pl.BlockSpec(a.shape, cidx(a.ndim), pipeline_mode=pl.Buffered(1))
    out_spec = pl.BlockSpec((1, m, set_w), lambda b, s, *_: (b, s, 0))
    grid_spec = pltpu.PrefetchScalarGridSpec(
        num_scalar_prefetch=n_pref,
        grid=(n_seq, steps),
        in_specs=x_specs + [const(cw['pe']), const(cw['w1']), const(cw['b1']), const(cw['w2'])],
        out_specs=[out_spec, out_spec],
        scratch_shapes=[pltpu.VMEM((4, SUBLANES, 2 * LANES), F32),
                        pltpu.VMEM((n_in, 2 * set_w // CMP_GROUP_PAIR, page, CMP_GROUP_PAIR), F32)],
    )
    call = pl.pallas_call(
        functools.partial(_compress_kernel, n_pref=n_pref, n_in=n_in),
        grid_spec=grid_spec,
        out_shape=[jax.ShapeDtypeStruct((n_seq, seq_pages * halves, set_w), BF16)] * 2,
        compiler_params=_cparams("arbitrary", "arbitrary"),
        name="compress",
    )
    args = ([page_table] if page_table is not None else []) + [src] * n_in + [cw['pe'], cw['w1'], cw['b1'], cw['w2']]
    return call(*args)


KEY_TILE = 128
N_NEAR_TILES = REL_MAX_DIST // KEY_TILE + 1
TILE_FUTURE, TILE_FAR, TILE_WINDOW = 0, N_NEAR_TILES + 1, N_NEAR_TILES + 2


def _rel_bucket(dist):
    n = jnp.maximum(dist, 0)
    max_exact = NUM_BUCKETS // 2
    nf = jnp.maximum(n, 1).astype(F32)
    large = max_exact + (jnp.log(nf / max_exact) / math.log(REL_MAX_DIST / max_exact)
                         * (NUM_BUCKETS - max_exact)).astype(jnp.int32)
    return jnp.where(n < max_exact, n, jnp.minimum(large, NUM_BUCKETS - 1))


BIAS_LEFT = 2 * LANES


def _bias_by_distance(table, reach):
    bd = table[_rel_bucket(jnp.arange(REL_MAX_DIST))].T.astype(F32)
    heads = bd.shape[0]
    return jnp.concatenate([jnp.zeros((heads, BIAS_LEFT), F32), bd,
                            jnp.broadcast_to(bd[:, -1:], (heads, reach - REL_MAX_DIST))], axis=1)


def _bias_toeplitz(v, c, n_rows, n_cols, col_step=1):
    heads = v.shape[0]
    c += BIAS_LEFT
    width = (n_cols - 1) * col_step + 1
    period = n_rows + width
    assert c - width + 1 >= 0 and c + n_rows <= v.shape[1]
    if col_step > 1:
        return jnp.stack([v[:, c + i - width + 1:c + i + 1:col_step][:, ::-1] for i in range(n_rows)], axis=1)
    b = jnp.concatenate([v[:, c - width + 1:c + 1][:, ::-1], jnp.zeros((heads, 1), F32), v[:, c + 1:c + n_rows][:, ::-1]],
                        axis=1)
    m = jnp.tile(b, (1, n_rows))[:, :n_rows * (period - 1)].reshape(heads, n_rows, period - 1)
    return m[:, :, :width:col_step]


CMP_BAND_W = 2 * LANES
CMP_BAND_E0 = Q_BLOCK // CMP_STRIDE - CMP_BAND_W


def _compressed_band(table):
    i = np.arange(Q_BLOCK)[:, None]
    e = CMP_BAND_E0 + np.arange(CMP_BAND_W)[None, :]
    dist = i - CMP_STRIDE * e - (CMP_STRIDE - 1)
    assert -CMP_STRIDE * (CMP_BAND_E0 - 1) - (CMP_STRIDE - 1) >= REL_MAX_DIST
    c0 = -CMP_STRIDE * CMP_BAND_E0 - (CMP_STRIDE - 1)
    v = _bias_by_distance(table, c0 + Q_BLOCK)
    far = v[:, -1][:, None, None]
    return jnp.where(dist >= 0, _bias_toeplitz(v, c0, Q_BLOCK, CMP_BAND_W, CMP_STRIDE) - far, 0.0)


def _toeplitz_tiles(table):
    i = np.arange(Q_BLOCK)[:, None]
    j = np.arange(KEY_TILE)[None, :]
    offs = [-1] + list(range(N_NEAR_TILES)) + [N_NEAR_TILES + REL_MAX_DIST // KEY_TILE, WINDOW // KEY_TILE]
    v = _bias_by_distance(table, KEY_TILE * (max(offs) + 1))
    far = v[:, -1][:, None, None]
    tiles = []
    for idx, off in enumerate(offs):
        dist = KEY_TILE * off + i - j
        ok = (dist >= 0) & ((idx != TILE_WINDOW) | (dist < WINDOW))
        tiles.append(jnp.where(ok, _bias_toeplitz(v, KEY_TILE * off, Q_BLOCK, KEY_TILE) - far, NEG_INF))
    return jnp.stack(tiles)


SEL_CHUNK = 1024
ONES_GROUP = (1, 0)
WIN_KEYS = WINDOW + Q_BLOCK


def _masked_softmax(s, mask):
    s = jnp.where(mask, s, NEG_INF)
    e = jnp.exp(s - jnp.max(s, axis=-1, keepdims=True))
    p = e / jnp.sum(e, axis=-1, keepdims=True)
    return jnp.where(mask, p, 0.0)


def _topk_masks_t(vals, key_sc, corr_sc, n_rows):
    shape = vals[0].shape
    for i, v in enumerate(vals):
        key_sc[i] = v
    corr_sc[...] = jnp.zeros_like(corr_sc)
    jg = lax.broadcasted_iota(jnp.int32, shape, 0) // SUBLANES
    jj = lax.broadcasted_iota(jnp.int32, (SUBLANES, shape[1]), 0)

    def group_body(kg, cnts):
        base = pl.multiple_of(kg * SUBLANES, SUBLANES)
        later = jnp.where(jg > kg, 1, 0)
        out = []
        for i, cnt in enumerate(cnts):
            key = key_sc[i]
            diag = key_sc[i, pl.ds(base, SUBLANES), :]
            corr = jnp.zeros(diag.shape, jnp.int32)
            for kk in range(SUBLANES):
                k_row = key_sc[i, pl.ds(base + kk, 1), :]
                k_all = jnp.broadcast_to(k_row, shape)
                cnt = cnt + jnp.where(k_all > key, 1, jnp.where(k_all == key, later, 0))
                corr = corr + jnp.where(jnp.broadcast_to(k_row, diag.shape) == diag, jnp.where(jj > kk, 1, 0), 0)
            corr_sc[i, pl.ds(base, SUBLANES), :] = corr
            out.append(cnt)
        return tuple(out)

    cnts = lax.fori_loop(0, (n_rows + SUBLANES - 1) // SUBLANES, group_body,
                         tuple(jnp.zeros(shape, jnp.int32) for _ in vals))
    return [jnp.where(cnt + corr_sc[i] < N_SELECT, 1.0, 0.0) for i, cnt in enumerate(cnts)]


def _nsa_prompt_kernel(q_ref, gate_ref, kct_ref, vc_ref, ksa_ref, vsa_ref, kwt_ref, vw_ref, tb_ref, cb_ref,
                       selmap_ref, eg_ref, o_ref, key_sc, corr_sc, unsel_sc, ocmp_sc, osel_sc, m_sc,
                       acc_sc, s0_sc, s1_sc):
    qb = pl.program_id(1)
    t0 = qb * Q_BLOCK
    gw = N_KV_GROUPS * HEAD_DIM
    n_cmp = kct_ref.shape[2]
    rows = GROUP_SIZE * Q_BLOCK

    qi = lax.broadcasted_iota(jnp.int32, (Q_BLOCK, n_cmp), 0)
    hc = lax.broadcasted_iota(jnp.int32, (Q_BLOCK, n_cmp), 1)
    mask_c = jnp.where(hc >= 1, t0 + qi - (CMP_STRIDE * hc + CMP_STRIDE - 1), -1) >= 0
    first_row = qb * (Q_BLOCK // CMP_STRIDE)
    band_tile = (first_row + Q_BLOCK // CMP_STRIDE - 1) // LANES
    band_shift = (first_row % LANES + LANES + CMP_BAND_E0 + CMP_BAND_W) % CMP_BAND_W

    def compressed_bias(head):
        band = pltpu.roll(cb_ref[head], band_shift, 1)
        lo, hi = band[:, :LANES], band[:, LANES:]
        return jnp.concatenate([jnp.where(band_tile == c, hi, jnp.where(band_tile - 1 == c, lo, 0.0))
                                for c in range(n_cmp // LANES)], axis=1)

    lane_g = lax.broadcasted_iota(jnp.int32, (Q_BLOCK, gw), 1) // HEAD_DIM

    def group_queries(g):
        return jnp.concatenate(
            [jnp.where(lane_g == g, q_ref[0, :, r * gw:(r + 1) * gw], 0.0).astype(BF16) for r in range(GROUP_SIZE)], axis=0)

    nb = selmap_ref.shape[0]
    jb = lax.broadcasted_iota(jnp.int32, (nb, Q_BLOCK), 0)
    qpos_t = t0 + lax.broadcasted_iota(jnp.int32, (nb, Q_BLOCK), 1)
    cur = qpos_t // SEL_BLOCK
    forced = jnp.where(jb == 0, 1, 0) + jnp.where(jb == cur, 1, 0) + jnp.where(jb == cur - 1, 1, 0)
    valid_t = jb * SEL_BLOCK <= qpos_t
    n_valid_blocks = (t0 + Q_BLOCK - 1) // SEL_BLOCK + 1
    n_chunks = (t0 + Q_BLOCK - 1) // SEL_CHUNK + 1

    importance = []
    for g in range(N_KV_GROUPS):
        s_c = _dot(group_queries(g), kct_ref[0])
        p_rows = []
        for r in range(GROUP_SIZE):
            bias = compressed_bias(g * GROUP_SIZE + r)
            p_rows.append(_masked_softmax(s_c[r * Q_BLOCK:(r + 1) * Q_BLOCK] + bias, mask_c).astype(BF16))
        ocmp_sc[g] = _dot(jnp.concatenate(p_rows, axis=0), vc_ref[0])
        imp = jnp.zeros((nb, Q_BLOCK), F32)
        for r in range(GROUP_SIZE):
            imp = imp + lax.dot_general(selmap_ref[...], p_rows[r], (((1,), (1,)), ((), ())), preferred_element_type=F32)
        importance.append(jnp.where(forced > 0, 1e9, jnp.where(valid_t, imp, -1e9)))

    for g, mask_t in enumerate(_topk_masks_t(importance, key_sc, corr_sc, n_valid_blocks)):
        unsel_sc[g] = (1.0 - mask_t.T).astype(BF16)

    def group_body(g, carry):
        qm = group_queries(g)
        unsel = unsel_sc[g]

        m_sc[...] = jnp.full_like(m_sc, NEG_INF)
        acc_sc[...] = jnp.zeros_like(acc_sc)
        ones_at = jnp.where(g == ONES_GROUP[0], 1, 0)
        v_col = pl.multiple_of(ones_at * gw, gw)
        tiles_per_chunk = SEL_CHUNK // KEY_TILE

        def scores(c):
            return _dot_split_rows(qm, ksa_ref[0, :gw, pl.ds(pl.multiple_of(c * SEL_CHUNK, SEL_CHUNK), SEL_CHUNK)])

        def sweep_step(c, s_cur, s_next, near):
            off = pl.multiple_of(c * SEL_CHUNK, SEL_CHUNK)
            masked = _dot(unsel, ksa_ref[0, gw:, pl.ds(off, SEL_CHUNK)])
            s_next[...] = scores(jnp.minimum(c + 1, n_chunks - 1))
            parts = []
            for r in range(GROUP_SIZE):
                extra = masked
                if near:
                    extra = masked + jnp.concatenate(
                        [tb_ref[jnp.clip(qb - (c * tiles_per_chunk + h), -1, N_NEAR_TILES) + 1, g * GROUP_SIZE + r]
                         for h in range(tiles_per_chunk)], axis=1)
                parts.append(s_cur[r * Q_BLOCK:(r + 1) * Q_BLOCK, :] + extra)
            s = jnp.concatenate(parts, axis=0)
            m_old = m_sc[...]
            m_new = jnp.maximum(m_old, jnp.max(s, axis=-1, keepdims=True))
            alpha = jnp.exp(m_old - m_new)
            p = jnp.exp(s - jnp.concatenate([m_new] * (SEL_CHUNK // LANES), axis=1))
            pv = _dot_split_rows(p.astype(BF16), vsa_ref[0, pl.ds(off, SEL_CHUNK), pl.ds(v_col, gw)])
            acc_sc[...] = jnp.concatenate([alpha] * (gw // LANES), axis=1) * acc_sc[...] + pv
            m_sc[...] = m_new

        s0_sc[...] = scores(0)

        n_far = jnp.maximum((qb - N_NEAR_TILES + 1) // tiles_per_chunk, 0)

        def step(c, s_cur, s_next):
            @pl.when(c < n_far)
            def _():
                sweep_step(c, s_cur, s_next, False)

            @pl.when((c >= n_far) & (c < n_chunks))
            def _():
                sweep_step(c, s_cur, s_next, True)

        def pair_body(pair, carry2):
            step(2 * pair, s0_sc, s1_sc)
            step(2 * pair + 1, s1_sc, s0_sc)
            return carry2

        lax.fori_loop(0, (n_chunks + 1) // 2, pair_body, 0)
        acc = acc_sc[...]
        denom = jnp.where(ones_at == 1, acc[:, ONES_GROUP[1] * HEAD_DIM:ONES_GROUP[1] * HEAD_DIM + 1],
                          acc[:, ONES_GROUP[0] * HEAD_DIM:ONES_GROUP[0] * HEAD_DIM + 1])
        osel_sc[g] = acc / denom
        return carry

    lax.fori_loop(0, N_KV_GROUPS, group_body, 0)

    n_wt = WIN_KEYS // KEY_TILE
    w_off = [pl.multiple_of(jnp.maximum(qb - (n_wt - 1) + w, 0) * KEY_TILE, KEY_TILE) for w in range(n_wt)]
    w_tile = [jnp.where(qb - (n_wt - 1) + w >= 0, TILE_WINDOW if w == 0 else n_wt - w, TILE_FUTURE) for w in range(n_wt)]
    k_win = jnp.concatenate([kwt_ref[0, :, pl.ds(o, KEY_TILE)] for o in w_off], axis=1)
    v_win = jnp.concatenate([vw_ref[0, pl.ds(o, KEY_TILE), :] for o in w_off], axis=0)
    gexp = jnp.dot(gate_ref[0], eg_ref[...], preferred_element_type=F32, precision=lax.Precision.HIGHEST)
    out = [jnp.zeros((Q_BLOCK, gw), F32) for _ in range(GROUP_SIZE)]
    for g in range(N_KV_GROUPS):
        s_w = _dot(group_queries(g), k_win)
        pw_rows = []
        for r in range(GROUP_SIZE):
            bias = jnp.concatenate([tb_ref[ti, g * GROUP_SIZE + r] for ti in w_tile], axis=1)
            s_r = s_w[r * Q_BLOCK:(r + 1) * Q_BLOCK] + bias
            e = jnp.exp(s_r - jnp.max(s_r, axis=-1, keepdims=True))
            pw_rows.append((e / jnp.sum(e, axis=-1, keepdims=True)).astype(BF16))
        o_win = _dot(jnp.concatenate(pw_rows, axis=0), v_win)
        for r in range(GROUP_SIZE):
            rs = slice(r * Q_BLOCK, (r + 1) * Q_BLOCK)
            mix = (gexp[:, (3 * r) * gw:(3 * r + 1) * gw] * ocmp_sc[g, rs, :]
                   + gexp[:, (3 * r + 1) * gw:(3 * r + 2) * gw] * osel_sc[g, rs, :]
                   + gexp[:, (3 * r + 2) * gw:(3 * r + 3) * gw] * o_win[rs])
            out[r] = out[r] + jnp.where(lane_g == g, mix, 0.0)
    for r in range(GROUP_SIZE):
        o_ref[0, :, r * gw:(r + 1) * gw] = out[r].astype(BF16)


def _nsa_prompt(q, gates, kct, vc, ksa, vsa, kwt, selwin, tiles, band, selmap, eg):
    n, t, hw = q.shape
    gw = N_KV_GROUPS * HEAD_DIM
    n_cmp = kct.shape[2]
    rows = GROUP_SIZE * Q_BLOCK
    per_n = lambda shape, col=0: pl.BlockSpec(shape, lambda b, j: (b, 0, col), pipeline_mode=pl.Buffered(1))
    blk = lambda w: pl.BlockSpec((1, Q_BLOCK, w), lambda b, j: (b, j, 0))
    return pl.pallas_call(
        _nsa_prompt_kernel,
        grid=(n, t // Q_BLOCK),
        in_specs=[blk(hw), blk(gates.shape[2]),
                  per_n((1, gw, n_cmp)), per_n((1, n_cmp, gw)), per_n((1, ksa.shape[1], t)), per_n((1, t, 2 * gw)),
                  per_n((1, gw, t)), per_n((1, t, gw), 3),
                  _const_spec(tiles.shape), _const_spec(band.shape), _const_spec(selmap.shape), _const_spec(eg.shape)],
        out_specs=blk(hw),
        out_shape=jax.ShapeDtypeStruct((n, t, hw), BF16),
        scratch_shapes=[pltpu.VMEM((N_KV_GROUPS, selmap.shape[0], Q_BLOCK), F32),
                        pltpu.VMEM((N_KV_GROUPS, selmap.shape[0], Q_BLOCK), jnp.int32),
                        pltpu.VMEM((N_KV_GROUPS, Q_BLOCK, selmap.shape[0]), BF16),
                        pltpu.VMEM((N_KV_GROUPS, rows, gw), F32), pltpu.VMEM((N_KV_GROUPS, rows, gw), F32),
                        pltpu.VMEM((rows, LANES), F32), pltpu.VMEM((rows, gw), F32),
                        pltpu.VMEM((rows, SEL_CHUNK), F32), pltpu.VMEM((rows, SEL_CHUNK), F32)],
        compiler_params=_cparams("arbitrary", "arbitrary"),
        name="nsa_prompt",
    )(q, gates, kct, vc, ksa, vsa, kwt, selwin, tiles, band, selmap, eg)


DEC_T_PAD = SUBLANES
SAMPLE_BLOCKS_PAD = 256
NEW_KEYS_PAD = LANES


def _nsa_sample_kernel(*refs, n_pages, n_blocks):
    pt_ref, q_ref, gate_ref, kct_ref, vc_ref = refs[:5]
    page_refs = refs[5:5 + n_pages]
    (bsel_ref, bcmp_ref, bwin_ref, bnew_ref, new_ref, cwin_ref, selmap_ref, eg_ref, o_ref,
     key_sc, corr_sc, sel_sc, m_sc, l_sc, acc_sc, oc_sc) = refs[5 + n_pages:]
    c = pl.program_id(1)
    gw = N_KV_GROUPS * HEAD_DIM
    rows = N_HEADS * DEC_T_PAD
    page = page_refs[0].shape[2]
    lane_g = lax.broadcasted_iota(jnp.int32, (DEC_T_PAD, gw), 1) // HEAD_DIM
    qm = jnp.concatenate(
        [jnp.where(lane_g == g, q_ref[0, :, r * gw:(r + 1) * gw], 0.0)
         for r in range(GROUP_SIZE) for g in range(N_KV_GROUPS)], axis=0).astype(BF16)
    nt = (((1,), (1,)), ((), ()))

    @pl.when(c == 0)
    def _():
        p_c = _masked_softmax(_dot(qm, kct_ref[0]) + bcmp_ref[...], bcmp_ref[...] > 0.5 * NEG_INF).astype(BF16)
        oc_sc[...] = _dot(p_c, vc_ref[0])
        imp_all = lax.dot_general(selmap_ref[...], p_c, nt, preferred_element_type=F32)
        per_r = rows // GROUP_SIZE
        imp = imp_all
        for r in range(1, GROUP_SIZE):
            imp = imp + pltpu.roll(imp_all, r * per_r, 1)
        jb = lax.broadcasted_iota(jnp.int32, imp.shape, 0)
        last = n_blocks - 1
        forced = jnp.where(jb == 0, 1, 0) + jnp.where(jb == last, 1, 0) + jnp.where(jb == last - 1, 1, 0)
        imp = jnp.where(forced > 0, 1e9, jnp.where(jb <= last, imp, -1e9))
        live = key_sc.shape[1]
        picked_t = _topk_masks_t([imp[:live]], key_sc, corr_sc, n_blocks)[0]
        sel_sc[...] = jnp.concatenate([picked_t, jnp.zeros((imp.shape[0] - live, rows), F32)], axis=0).T.astype(BF16)
        m_sc[...] = jnp.full_like(m_sc, NEG_INF)
        l_sc[...] = jnp.zeros_like(l_sc)
        acc_sc[...] = jnp.zeros_like(acc_sc)

    def online_update(s, pv_of):
        m_old = m_sc[...]
        m_new = jnp.maximum(m_old, jnp.max(s, axis=-1, keepdims=True))
        alpha = jnp.exp(m_old - m_new)
        p = jnp.exp(s - m_new)
        l_sc[...] = alpha * l_sc[...] + jnp.sum(p, axis=-1, keepdims=True)
        acc_sc[...] = alpha * acc_sc[...] + pv_of(p.astype(BF16))
        m_sc[...] = m_new

    keys = n_pages * page
    s = jnp.concatenate([_dot(qm, pr[0, :gw, :].astype(BF16)) for pr in page_refs], axis=1)
    eb = lax.broadcasted_iota(jnp.int32, (SAMPLE_BLOCKS_PAD, keys), 0)
    ek = lax.broadcasted_iota(jnp.int32, (SAMPLE_BLOCKS_PAD, keys), 1)
    expand = jnp.where(eb == c * (keys // SEL_BLOCK) + ek // SEL_BLOCK, 1.0, 0.0).astype(BF16)
    picked = _dot(sel_sc[...], expand) > 0.5
    v_t = jnp.concatenate([pr[0, gw:, :].astype(BF16) for pr in page_refs], axis=1)
    online_update(jnp.where(picked, s + bsel_ref[...], NEG_INF),
                  lambda p: lax.dot_general(p, v_t, nt, preferred_element_type=F32))

    @pl.when(c == pl.num_programs(1) - 1)
    def _():
        s_new = lax.dot_general(qm, new_ref[0, :, :gw], nt, preferred_element_type=F32) + bnew_ref[0]
        online_update(s_new, lambda p: _dot(p, new_ref[0, :, gw:2 * gw]))
        o_sel = acc_sc[...] / l_sc[...]

        s_w = _dot(qm, cwin_ref[0, :gw, :].astype(BF16)) + bwin_ref[...]
        s_wn = lax.dot_general(qm, new_ref[0, :, 2 * gw:3 * gw], nt, preferred_element_type=F32) + bnew_ref[1]
        m_w = jnp.maximum(jnp.max(s_w, axis=-1, keepdims=True), jnp.max(s_wn, axis=-1, keepdims=True))
        e_w = jnp.exp(s_w - m_w)
        e_wn = jnp.exp(s_wn - m_w)
        l_w = jnp.sum(e_w, axis=-1, keepdims=True) + jnp.sum(e_wn, axis=-1, keepdims=True)
        o_win = lax.dot_general((e_w / l_w).astype(BF16), cwin_ref[0, gw:, :].astype(BF16), nt,
                                preferred_element_type=F32) + _dot((e_wn / l_w).astype(BF16), new_ref[0, :, 3 * gw:])

        gexp = jnp.dot(gate_ref[0], eg_ref[...], preferred_element_type=F32, precision=lax.Precision.HIGHEST)
        o_cmp = oc_sc[...]
        for r in range(GROUP_SIZE):
            out_r = jnp.zeros((DEC_T_PAD, gw), F32)
            for g in range(N_KV_GROUPS):
                rs = slice((r * N_KV_GROUPS + g) * DEC_T_PAD, (r * N_KV_GROUPS + g + 1) * DEC_T_PAD)
                mix = (gexp[:, (3 * r) * gw:(3 * r + 1) * gw] * o_cmp[rs]
                       + gexp[:, (3 * r + 1) * gw:(3 * r + 2) * gw] * o_sel[rs]
                       + gexp[:, (3 * r + 2) * gw:(3 * r + 3) * gw] * o_win[rs])
                out_r = out_r + jnp.where(lane_g == g, mix, 0.0)
            o_ref[0, :, r * gw:(r + 1) * gw] = out_r


PAGES_PER_STEP = 16


def _nsa_sample(page_table, q, gates, kct, vc, cache, bias, new_rows, cache_win, selmap, eg, n_blocks):
    n, _, hw = q.shape
    gw = N_KV_GROUPS * HEAD_DIM
    n_cmp = kct.shape[2]
    rows = N_HEADS * DEC_T_PAD
    page = cache.shape[2]
    steps = page_table.shape[1] // PAGES_PER_STEP
    keys = PAGES_PER_STEP * page
    per_n = lambda a: pl.BlockSpec((1,) + a.shape[1:], lambda b, c, pt: (b, 0, 0))
    cidx = lambda nd: (lambda *_: (0,) * nd)
    const = lambda a: pl.BlockSpec(a.shape, cidx(a.ndim), pipeline_mode=pl.Buffered(1))
    page_specs = [pl.BlockSpec((1, 2 * gw, page), functools.partial(
        lambda b, c, pt, p: (pt[b, c * PAGES_PER_STEP + p], 1, 0), p=p)) for p in range(PAGES_PER_STEP)]
    grid_spec = pltpu.PrefetchScalarGridSpec(
        num_scalar_prefetch=1,
        grid=(n, steps),
        in_specs=[per_n(q), per_n(gates), per_n(kct), per_n(vc)] + page_specs + [
            pl.BlockSpec((rows, keys), lambda b, c, pt: (0, c)), const(bias['cmp']), const(bias['win']), const(bias['new']),
            per_n(new_rows), per_n(cache_win), const(selmap), const(eg)],
        out_specs=pl.BlockSpec((1, DEC_T_PAD, hw), lambda b, c, pt: (b, 0, 0)),
        scratch_shapes=[pltpu.VMEM((1, -(-n_blocks // SUBLANES) * SUBLANES, rows), F32),
                        pltpu.VMEM((1, -(-n_blocks // SUBLANES) * SUBLANES, rows), jnp.int32),
                        pltpu.VMEM((rows, SAMPLE_BLOCKS_PAD), BF16),
                        pltpu.VMEM((rows, 1), F32), pltpu.VMEM((rows, 1), F32), pltpu.VMEM((rows, gw), F32),
                        pltpu.VMEM((rows, gw), F32)],
    )
    return pl.pallas_call(
        functools.partial(_nsa_sample_kernel, n_pages=PAGES_PER_STEP, n_blocks=n_blocks),
        grid_spec=grid_spec,
        out_shape=jax.ShapeDtypeStruct((n, DEC_T_PAD, hw), F32),
        compiler_params=_cparams("arbitrary", "arbitrary"),
        name="nsa_sample",
    )(page_table, q, gates, kct, vc, *([cache] * PAGES_PER_STEP), bias['sel'], bias['cmp'], bias['win'], bias['new'],
      new_rows, cache_win, selmap, eg)


def _sample_bias(table, past, dec_t, win_len, n_cmp_rows):
    t = np.arange(DEC_T_PAD)[:, None]
    v = _bias_by_distance(table, max(past + DEC_T_PAD, REL_MAX_DIST + 1))

    def rows(c, n_cols, col_step, ok):
        b = jnp.where(ok, _bias_toeplitz(v, c, DEC_T_PAD, n_cols, col_step), NEG_INF)
        b = b.reshape(N_KV_GROUPS, GROUP_SIZE, DEC_T_PAD, -1).transpose(1, 0, 2, 3)
        return b.reshape(N_HEADS * DEC_T_PAD, -1)

    h = np.arange(n_cmp_rows)[None, :]
    dist_c = past + t - (CMP_STRIDE * h + CMP_STRIDE - 1)
    j = np.arange(win_len)[None, :]
    dist_w = win_len + t - j
    tn = np.arange(NEW_KEYS_PAD)[None, :]
    new = rows(0, NEW_KEYS_PAD, 1, (tn <= t) & (tn < dec_t))
    return dict(sel=rows(past, past, 1, np.ones((DEC_T_PAD, past), bool)),
                cmp=rows(past - (CMP_STRIDE - 1), n_cmp_rows, CMP_STRIDE, (h >= 1) & (dist_c >= 0)),
                win=rows(win_len, win_len, 1, (dist_w < WINDOW) & (past - win_len + j >= 0)), new=jnp.stack([new, new]))


GATE_LANES = LANES


def _prepare(p):
    d = p['norm_pre_mix'].shape[1]
    hw = N_HEADS * HEAD_DIM
    gw = N_KV_GROUPS * HEAD_DIM
    row = lambda v: v.reshape(1, -1)
    w = {}
    w['rg'] = dict(
        g_pre=row(p['norm_pre_mix'][0]), w_in=p['rg_w_in'][0].astype(BF16), conv_w=p['rg_conv_w'][0],
        conv_b=row(p['rg_conv_b'][0]),
        w_ax=jnp.concatenate([p['rg_w_a'][0], p['rg_w_x'][0]], axis=2).astype(BF16),
        b_ax=jnp.stack([p['rg_b_a'][0], p['rg_b_x'][0]]), lam=row(p['rg_lambda'][0]),
        w_out=p['rg_w_out'][0].astype(BF16), g_post=row(p['norm_post_mix'][0]))
    w['mlp'] = [dict(g1=row(p['norm_pre_mlp'][l]), wup=p['w_mlp_up'][l].astype(BF16),
                     wdn=p['w_mlp_down'][l].astype(BF16), g2=row(p['norm_post_mlp'][l])) for l in range(2)]
    wqg = p['nsa_w_qg'][0]
    n_gate = wqg.shape[1] - hw
    w['proj'] = dict(
        gkv=row(p['kv_norm']), wkv=p['w_kv'].astype(BF16), gq=row(p['norm_pre_mix'][1]),
        wq=wqg[:, :hw].reshape(d, N_KV_GROUPS, GROUP_SIZE, HEAD_DIM).transpose(0, 2, 1, 3).reshape(d, hw).astype(BF16),
        wg=jnp.pad(wqg[:, hw:], ((0, 0), (0, GATE_LANES - n_gate))).astype(BF16),
        bg=jnp.pad(p['nsa_b_g'][0], (0, GATE_LANES - n_gate)).reshape(1, -1))
    w['wo'] = p['nsa_w_o'][0].reshape(N_KV_GROUPS, GROUP_SIZE, HEAD_DIM, d).transpose(1, 0, 2, 3).reshape(hw, d).astype(BF16)
    w['g_post_mix1'] = row(p['norm_post_mix'][1])
    eye2 = jnp.eye(2, dtype=F32)
    half = L_CMP // CMP_STRIDE
    w['cmp'] = dict(
        pe=jnp.broadcast_to(p['cmp_pe'].reshape(2, half, CMP_STRIDE, 1, HEAD_DIM),
                            (2, half, CMP_STRIDE, 2, HEAD_DIM)).reshape(2, half, 1, CMP_STRIDE * CMP_GROUP_PAIR),
        w1=jnp.einsum('sftdc,gh->sftgdhc', p['cmp_w1'].reshape(2, half, CMP_STRIDE, HEAD_DIM, -1), eye2)
        .reshape(2, half, CMP_STRIDE * CMP_GROUP_PAIR, -1).astype(BF16),
        b1=jnp.tile(p['cmp_b1'], (1, 2)).reshape(2, 1, -1),
        w2=jnp.einsum('scd,gh->sgchd', p['cmp_w2'], eye2).reshape(2, -1, CMP_GROUP_PAIR).astype(BF16))
    w['table'] = p['rel_bias_table']
    w['tiles'] = _toeplitz_tiles(p['rel_bias_table'])
    w['band'] = _compressed_band(p['rel_bias_table'])
    h_i = jnp.arange(N_HEADS)
    src = (h_i[:, None] * 3 + jnp.arange(3)[None, :]).reshape(-1)
    dst0 = (((h_i % GROUP_SIZE)[:, None] * 3 + jnp.arange(3)[None, :]) * gw
            + (h_i // GROUP_SIZE)[:, None] * HEAD_DIM).reshape(-1)
    lanes = jnp.arange(3 * GROUP_SIZE * gw)
    w['eg'] = jnp.zeros((GATE_LANES, lanes.shape[0]), F32).at[src].set(
        ((lanes[None, :] >= dst0[:, None]) & (lanes[None, :] < dst0[:, None] + HEAD_DIM)).astype(F32))
    return w


def _selmap_t(n_blocks, n_cmp_rows):
    ratio = SEL_BLOCK // CMP_STRIDE
    h = jnp.arange(n_cmp_rows)[None, :]
    j = jnp.arange(n_blocks)[:, None]
    m = ((h - 1) // ratio == j).astype(F32) + (h // ratio == j).astype(F32)
    return jnp.where(h >= 1, m, 0.0).astype(BF16)


def _trunk_prompt(x, w):
    n, t, d = x.shape
    gw = N_KV_GROUPS * HEAD_DIM
    zeros = lambda r: jnp.zeros((n, r, d), F32)
    x1, h_last, c_last = _rglru_layer(x, zeros(1), zeros(CONV_W - 1), w['rg'], stride=1, tm=256)
    m0 = w['mlp'][0]
    x2 = _mlp_layer(x1.reshape(n * t, d), m0['g1'], m0['wup'], m0['wdn'], m0['g2'], tm=512)
    pj = w['proj']
    kv4, _, selwin, q, gates, kst, kwt, vsa, kv_t, win_t = _proj_layer(
        x2, pj['gkv'], pj['wkv'], pj['gq'], pj['wq'], pj['wg'], pj['bg'], tm=512, seq_len=t)
    nh = t // CMP_STRIDE
    n_blocks = t // SEL_BLOCK
    kc, vc = _compress(kv_t, w['cmp'], n_seq=n, seq_pages=t // KEY_TILE, page=KEY_TILE)
    selneg = jnp.where(jnp.arange(t)[None, :] // SEL_BLOCK == jnp.arange(n_blocks)[:, None], NEG_INF, 0.0).astype(BF16)
    ksa = jnp.concatenate([kst, jnp.broadcast_to(selneg, (n,) + selneg.shape)], axis=1)
    attn = _nsa_prompt(q.reshape(n, t, -1), gates.reshape(n, t, -1), kc.transpose(0, 2, 1), vc, ksa, vsa.reshape(n, t, -1),
                       kwt, selwin.reshape(n, t, -1), w['tiles'], w['band'], _selmap_t(n_blocks, nh), w['eg'])
    m1 = w['mlp'][1]
    y = _mlp_layer(x2, m1['g1'], m1['wup'], m1['wdn'], m1['g2'], tm=512,
                   mix=(attn.reshape(n * t, -1), w['wo'], w['g_post_mix1']))
    wlen = min(WINDOW, t)
    rows_of = lambda a, sets: a.reshape(n, sets, N_KV_GROUPS, HEAD_DIM, -1).transpose(0, 4, 1, 2, 3)
    return (y.reshape(n, t, d), h_last.reshape(1, n, d), c_last.reshape(1, n, CONV_W - 1, d),
            rows_of(kv_t, 4), rows_of(win_t[:, :, t - wlen:], 2))


def _trunk_sample(x, h0, c0, cache_kv, cache_win, page_table, w):
    n, t, d = x.shape
    n_phys, page = cache_kv.shape[:2]
    past = page_table.shape[1] * page
    win_len = cache_win.shape[1]
    assert t < CMP_STRIDE and past % CMP_STRIDE == 0 and t <= DEC_T_PAD and page % CMP_STRIDE == 0
    assert page_table.shape[1] % PAGES_PER_STEP == 0 and page == KEY_TILE
    to_tn = lambda a: a.transpose(1, 0, 2).reshape(a.shape[0] * a.shape[1], -1)
    to_nt = lambda a, k: a.reshape(k, n, -1).transpose(1, 0, 2)
    tail = CONV_W - 1
    x1, h_last, c_last = _rglru_layer(to_tn(x)[None], h0[None], to_tn(c0)[None], w['rg'], stride=n, tm=t * n)
    m0 = w['mlp'][0]
    x2 = _mlp_layer(x1[0], m0['g1'], m0['wup'], m0['wdn'], m0['g2'], tm=t * n)
    pj = w['proj']
    kv4, win, selwin, q, gates = _proj_layer(x2, pj['gkv'], pj['wkv'], pj['gq'], pj['wq'], pj['wg'], pj['bg'], tm=t * n)

    row_w = cache_kv.shape[2] * cache_kv.shape[3] * cache_kv.shape[4]
    nh = past // CMP_STRIDE
    cache_t = cache_kv.reshape(n_phys, page, row_w).transpose(0, 2, 1)
    cache_win_t = cache_win.reshape(n, win_len, -1).transpose(0, 2, 1)
    kc, vc = _compress(cache_t, w['cmp'], n_seq=n, seq_pages=page_table.shape[1], page=page, page_table=page_table)
    pad_t = lambda a, k: jnp.pad(a, ((0, 0), (0, k - a.shape[1]), (0, 0)))
    n_blocks = -(-(past + t) // SEL_BLOCK)
    attn = _nsa_sample(
        page_table, pad_t(to_nt(q, t).astype(F32), DEC_T_PAD), pad_t(to_nt(gates, t), DEC_T_PAD), kc.transpose(0, 2, 1), vc,
        cache_t, _sample_bias(w['table'], past, t, win_len, nh),
        pad_t(to_nt(selwin, t), NEW_KEYS_PAD), cache_win_t, _selmap_t(SAMPLE_BLOCKS_PAD, nh), w['eg'], n_blocks)
    m1 = w['mlp'][1]
    y = _mlp_layer(x2, m1['g1'], m1['wup'], m1['wdn'], m1['g2'], tm=t * n,
                   mix=(to_tn(attn[:, :t]).astype(BF16), w['wo'], w['g_post_mix1']))
    win_rows = to_nt(win, t).reshape(n, t, 2, N_KV_GROUPS, HEAD_DIM)
    win_new = jnp.concatenate([cache_win, win_rows], axis=1)[:, t:]
    return (to_nt(y, t), h_last, to_nt(c_last[0], tail)[None],
            to_nt(kv4, t).reshape(n, t, 4, N_KV_GROUPS, HEAD_DIM), win_new)


def kernel(x_prompt, x_sample, state_rglru_h, state_rglru_conv, cache_kv, cache_win, page_table, norm_pre_mix,
           norm_post_mix, norm_pre_mlp, norm_post_mlp, w_mlp_up, w_mlp_down, rg_w_in, rg_conv_w, rg_conv_b, rg_w_a, rg_b_a,
           rg_w_x, rg_b_x, rg_lambda, rg_w_out, kv_norm, w_kv, cmp_pe, cmp_w1, cmp_b1, cmp_w2, nsa_w_qg, nsa_b_g, nsa_w_o,
           rel_bias_table):
    w = _prepare(dict(
        norm_pre_mix=norm_pre_mix, norm_post_mix=norm_post_mix, norm_pre_mlp=norm_pre_mlp, norm_post_mlp=norm_post_mlp,
        w_mlp_up=w_mlp_up, w_mlp_down=w_mlp_down, rg_w_in=rg_w_in, rg_conv_w=rg_conv_w, rg_conv_b=rg_conv_b,
        rg_w_a=rg_w_a, rg_b_a=rg_b_a, rg_w_x=rg_w_x, rg_b_x=rg_b_x, rg_lambda=rg_lambda, rg_w_out=rg_w_out,
        kv_norm=kv_norm, w_kv=w_kv, cmp_pe=cmp_pe, cmp_w1=cmp_w1, cmp_b1=cmp_b1, cmp_w2=cmp_w2, nsa_w_qg=nsa_w_qg,
        nsa_b_g=nsa_b_g, nsa_w_o=nsa_w_o, rel_bias_table=rel_bias_table))
    y_p, p_h, p_conv, p_kv, p_win = _trunk_prompt(x_prompt, w)
    y_s, s_h, s_conv, s_kv, s_win = _trunk_sample(x_sample, state_rglru_h[0], state_rglru_conv[0], cache_kv, cache_win,
                                                  page_table, w)
    return (y_p, y_s, p_h, p_conv, p_kv, p_win, s_h, s_conv, s_kv, s_win)
```

```python
import functools
import math

import jax
import jax.numpy as jnp
import numpy as np
from jax import lax
from jax.experimental import pallas as pl
from jax.experimental.pallas import tpu as pltpu

F32 = jnp.float32
BF16 = jnp.bfloat16

N_HEADS = 16
N_KV_GROUPS = 4
GROUP_SIZE = N_HEADS // N_KV_GROUPS
HEAD_DIM = 64
N_RNN_BLOCKS = 4
CONV_W = 4
LRU_C = 8.0
L_CMP = 32
CMP_STRIDE = 16
SEL_BLOCK = 64
N_SELECT = 16
WINDOW = 512
Q_BLOCK = 128
NUM_BUCKETS = 32
REL_MAX_DIST = 1024
EPS = 1e-6
NEG_INF = -1e30

SUBLANES = 8
LANES = 128
VMEM_LIMIT_BYTES = 56 * 1024 * 1024


def _cparams(*sem):
    return pltpu.CompilerParams(dimension_semantics=sem, vmem_limit_bytes=VMEM_LIMIT_BYTES)


def _const_spec(shape):
    nd = len(shape)
    return pl.BlockSpec(shape, lambda *_: (0,) * nd, pipeline_mode=pl.Buffered(1))


def _rms(x, g):
    return x * lax.rsqrt(jnp.mean(x * x, axis=-1, keepdims=True) + EPS) * g


def _dot(a, b):
    return jnp.dot(a, b, preferred_element_type=F32)


def _dot_split_rows(a, b):
    half = a.shape[0] // 2
    return jnp.concatenate([_dot(a[:half], b), _dot(a[half:], b)], axis=0)


def _rglru_kernel(x_ref, h0_ref, c0_ref, gpre_ref, win_ref, cw_ref, cb_ref, wax_ref, bax_ref, lam_ref,
                  wout_ref, gpost_ref, x1_ref, hlast_ref, clast_ref, xcat_sc, hprev_sc, *, stride, tm, pad):
    d = x_ref.shape[-1]
    tail = (CONV_W - 1) * stride
    j = pl.program_id(1)

    @pl.when(j == 0)
    def _():
        xcat_sc[pad - tail:pad, :] = c0_ref[0]
        hprev_sc[...] = jnp.zeros_like(hprev_sc)
        hprev_sc[tm - stride:tm, :] = h0_ref[0]

    x = x_ref[0]
    u = _rms(x, gpre_ref[...]).astype(BF16)
    proj = _dot(u, win_ref[...])
    gate = jax.nn.gelu(proj[:, :d])
    xcat_sc[pad:pad + tm, :] = proj[:, d:]
    xc = cb_ref[...] + cw_ref[CONV_W - 1:CONV_W, :] * xcat_sc[pad:pad + tm, :]
    for lag in range(1, CONV_W):
        xc = xc + cw_ref[CONV_W - 1 - lag:CONV_W - lag, :] * xcat_sc[pad - lag * stride:pad - lag * stride + tm, :]
    xcat_sc[pad - tail:pad, :] = xcat_sc[pad + tm - tail:pad + tm, :]
    clast_ref[0] = xcat_sc[pad - tail:pad, :]

    xcb = xc.astype(BF16)
    bw = d // N_RNN_BLOCKS
    ra, rx = [], []
    for blk in range(N_RNN_BLOCKS):
        pre = _dot(xcb[:, blk * bw:(blk + 1) * bw], wax_ref[blk])
        ra.append(pre[:, :bw])
        rx.append(pre[:, bw:])
    r = jax.nn.sigmoid(jnp.concatenate(ra, axis=1) + bax_ref[0:1, :])
    i = jax.nn.sigmoid(jnp.concatenate(rx, axis=1) + bax_ref[1:2, :])
    log_a = r * (-LRU_C * jax.nn.softplus(-lam_ref[...]))
    a = jnp.exp(log_a)
    b = jnp.sqrt(-jnp.tanh(log_a) * (a * a + 1.0)) * i * xc

    row = lax.broadcasted_iota(jnp.int32, (tm, d), 0)
    b = b + jnp.where(row < stride, a * pltpu.roll(hprev_sc[...], stride, 0), 0.0)
    s = stride
    while s < tm:
        keep = row >= s
        b = a * jnp.where(keep, pltpu.roll(b, s, 0), 0.0) + b
        a = a * jnp.where(keep, pltpu.roll(a, s, 0), 1.0)
        s *= 2
    h = b
    hprev_sc[...] = h
    hlast_ref[0] = h[tm - stride:tm, :]

    y = _dot((h * gate).astype(BF16), wout_ref[...])
    x1_ref[0] = x + _rms(y, gpost_ref[...])


def _rglru_layer(x, h0, c0, w, *, stride, tm):
    n, rows, d = x.shape
    tail = (CONV_W - 1) * stride
    pad = -(-tail // SUBLANES) * SUBLANES
    kern = functools.partial(_rglru_kernel, stride=stride, tm=tm, pad=pad)
    seq = lambda shape: pl.BlockSpec(shape, lambda b, j: (b, j, 0))
    per_n = lambda shape: pl.BlockSpec(shape, lambda b, j: (b, 0, 0))
    return pl.pallas_call(
        kern,
        grid=(n, rows // tm),
        in_specs=[seq((1, tm, d)), per_n((1, stride, d)), per_n((1, tail, d)),
                  _const_spec((1, d)), _const_spec(w['w_in'].shape), _const_spec((CONV_W, d)), _const_spec((1, d)),
                  _const_spec(w['w_ax'].shape), _const_spec((2, d)), _const_spec((1, d)),
                  _const_spec(w['w_out'].shape), _const_spec((1, d))],
        out_specs=[seq((1, tm, d)), per_n((1, stride, d)), per_n((1, tail, d))],
        out_shape=[jax.ShapeDtypeStruct((n, rows, d), F32), jax.ShapeDtypeStruct((n, stride, d), F32),
                   jax.ShapeDtypeStruct((n, tail, d), F32)],
        scratch_shapes=[pltpu.VMEM((pad + tm, d), F32), pltpu.VMEM((tm, d), F32)],
        compiler_params=_cparams("arbitrary", "arbitrary"),
        name="rglru_layer",
    )(x, h0, c0, w['g_pre'], w['w_in'], w['conv_w'], w['conv_b'], w['w_ax'], w['b_ax'], w['lam'], w['w_out'], w['g_post'])


MLP_HIDDEN_CHUNK = 1024


def _mlp_body(x, g1_ref, wup_ref, wdn_ref, g2_ref):
    u = _rms(x, g1_ref[...]).astype(BF16)
    f = wup_ref.shape[1]
    acc = jnp.zeros(x.shape, F32)
    for c in range(f // MLP_HIDDEN_CHUNK):
        cols = slice(c * MLP_HIDDEN_CHUNK, (c + 1) * MLP_HIDDEN_CHUNK)
        hid = jnp.maximum(_dot(u, wup_ref[:, cols]), 0.0)
        acc = acc + _dot((hid * hid).astype(BF16), wdn_ref[cols, :])
    return x + _rms(acc, g2_ref[...])


def _mlp_kernel(x_ref, g1_ref, wup_ref, wdn_ref, g2_ref, o_ref):
    o_ref[...] = _mlp_body(x_ref[...], g1_ref, wup_ref, wdn_ref, g2_ref)


def _mix_mlp_kernel(x_ref, a_ref, wo_ref, gmix_ref, g1_ref, wup_ref, wdn_ref, g2_ref, o_ref):
    x = x_ref[...] + _rms(_dot(a_ref[...], wo_ref[...]), gmix_ref[...])
    o_ref[...] = _mlp_body(x, g1_ref, wup_ref, wdn_ref, g2_ref)


def _mlp_layer(x, g1, wup, wdn, g2, *, tm, mix=None):
    rows, d = x.shape
    f = wup.shape[1]
    tok = pl.BlockSpec((tm, d), lambda i: (i, 0))
    mlp_specs = [_const_spec((1, d)), _const_spec((d, f)), _const_spec((f, d)), _const_spec((1, d))]
    if mix is None:
        kern, specs, args = _mlp_kernel, [tok] + mlp_specs, (x, g1, wup, wdn, g2)
    else:
        a, wo, gmix = mix
        kern = _mix_mlp_kernel
        specs = [tok, pl.BlockSpec((tm, a.shape[1]), lambda i: (i, 0)), _const_spec(wo.shape), _const_spec((1, d))] + mlp_specs
        args = (x, a, wo, gmix, g1, wup, wdn, g2)
    return pl.pallas_call(
        kern,
        grid=(rows // tm,),
        in_specs=specs,
        out_specs=tok,
        out_shape=jax.ShapeDtypeStruct((rows, d), F32),
        compiler_params=_cparams("arbitrary"),
        name="mlp_layer",
    )(*args)


def _proj_kernel(x_ref, gkv_ref, wkv_ref, gq_ref, wq_ref, wg_ref, bg_ref, kv_ref, win_ref, sw_ref, q_ref, gate_ref,
                 *kt_refs):
    x = x_ref[...]
    rows = _dot(_rms(x, gkv_ref[...]).astype(BF16), wkv_ref[...])
    n_kv = kv_ref.shape[1]
    gw = N_KV_GROUPS * HEAD_DIM
    kv_ref[...] = rows[:, :n_kv]
    win_ref[...] = rows[:, n_kv:]
    sw_ref[...] = rows[:, n_kv // 2:].astype(BF16)
    if kt_refs:
        kst_ref, kwt_ref, vsa_ref, kvt_ref, wint_ref = kt_refs
        rows_t = rows.T
        kvt_ref[0] = rows_t[:n_kv]
        wint_ref[0] = rows_t[n_kv:]
        kst_ref[0] = rows_t[2 * gw:3 * gw].astype(BF16)
        kwt_ref[0] = rows_t[4 * gw:5 * gw].astype(BF16)
        v_sel = rows[:, 3 * gw:4 * gw]
        lane_grp = lax.broadcasted_iota(jnp.int32, v_sel.shape, 1) // HEAD_DIM
        vsa_ref[...] = jnp.concatenate([jnp.where(lane_grp == ONES_GROUP[0], 1.0, v_sel),
                                        jnp.where(lane_grp == ONES_GROUP[1], 1.0, v_sel)], axis=1).astype(BF16)
    u = _rms(x, gq_ref[...]).astype(BF16)
    q_ref[...] = (_dot(u, wq_ref[...]) * HEAD_DIM ** -0.5).astype(BF16)
    gate_ref[...] = jax.nn.sigmoid(_dot(u, wg_ref[...]) + bg_ref[...])


def _proj_layer(x, gkv, wkv, gq, wq, wg, bg, *, tm, seq_len=None):
    rows, d = x.shape
    n_all = wkv.shape[1]
    n_kv = 4 * N_KV_GROUPS * HEAD_DIM
    gw = N_KV_GROUPS * HEAD_DIM
    tok = lambda w: pl.BlockSpec((tm, w), lambda i: (i, 0))
    out_specs = [tok(n_kv), tok(n_all - n_kv), tok(n_all - n_kv // 2), tok(wq.shape[1]), tok(wg.shape[1])]
    out_shape = [jax.ShapeDtypeStruct((rows, n_kv), F32), jax.ShapeDtypeStruct((rows, n_all - n_kv), F32),
                 jax.ShapeDtypeStruct((rows, n_all - n_kv // 2), BF16), jax.ShapeDtypeStruct((rows, wq.shape[1]), BF16),
                 jax.ShapeDtypeStruct((rows, wg.shape[1]), F32)]
    if seq_len is not None:
        tiles = seq_len // tm
        kt = pl.BlockSpec((1, gw, tm), lambda i: (i // tiles, 0, i % tiles))
        ktw = lambda w: pl.BlockSpec((1, w, tm), lambda i: (i // tiles, 0, i % tiles))
        out_specs += [kt, kt, tok(2 * gw), ktw(n_kv), ktw(n_all - n_kv)]
        out_shape += [jax.ShapeDtypeStruct((rows // seq_len, gw, seq_len), BF16)] * 2
        out_shape += [jax.ShapeDtypeStruct((rows, 2 * gw), BF16),
                      jax.ShapeDtypeStruct((rows // seq_len, n_kv, seq_len), F32),
                      jax.ShapeDtypeStruct((rows // seq_len, n_all - n_kv, seq_len), F32)]
    return pl.pallas_call(
        _proj_kernel,
        grid=(rows // tm,),
        in_specs=[tok(d), _const_spec((1, d)), _const_spec(wkv.shape), _const_spec((1, d)), _const_spec(wq.shape),
                  _const_spec(wg.shape), _const_spec(bg.shape)],
        out_specs=out_specs,
        out_shape=out_shape,
        compiler_params=_cparams("arbitrary"),
        name="kv_q_proj",
    )(x, gkv, wkv, gq, wq, wg, bg)


CMP_GROUP_PAIR = 2 * HEAD_DIM


def _compress_kernel(*refs, n_pref, n_in):
    x_refs = refs[n_pref:n_pref + n_in]
    pe_ref, w1_ref, b1_ref, w2_ref, kc_ref, vc_ref, carry_sc, xs_sc = refs[n_pref + n_in:]
    step = pl.program_id(1)

    @pl.when(step == 0)
    def _():
        carry_sc[...] = jnp.zeros_like(carry_sc)

    set_w = N_KV_GROUPS * HEAD_DIM
    n_chunks = 2 * set_w // CMP_GROUP_PAIR
    page = x_refs[0].shape[2]
    halves = page // CMP_STRIDE
    for p, x_ref in enumerate(x_refs):
        x = x_ref[0].T
        for cc in range(n_chunks):
            xs_sc[p, cc] = x[:, cc * CMP_GROUP_PAIR:(cc + 1) * CMP_GROUP_PAIR]
    m = n_in * halves
    row = lax.broadcasted_iota(jnp.int32, (m, 2 * LANES), 0)
    for s, out_ref in enumerate((kc_ref, vc_ref)):
        for jp in range(set_w // CMP_GROUP_PAIR):
            cc = s * (set_w // CMP_GROUP_PAIR) + jp
            xj = jnp.concatenate(
                [jnp.concatenate([xs_sc[p, cc, pl.ds(tt, halves, stride=CMP_STRIDE), :] for tt in range(CMP_STRIDE)], axis=1)
                 for p in range(n_in)], axis=0)
            first = _dot((xj + pe_ref[s, 0]).astype(BF16), w1_ref[s, 0])
            second = _dot((xj + pe_ref[s, 1]).astype(BF16), w1_ref[s, 1])
            slot = s * 2 + jp
            prev = jnp.where(row == 0, carry_sc[slot, 0:1, :], pltpu.roll(first, 1, 0))
            carry_sc[slot, 0:1, :] = first[m - 1:m, :]
            hid = jax.nn.gelu(prev + second + b1_ref[s])
            out_ref[0, :, jp * CMP_GROUP_PAIR:(jp + 1) * CMP_GROUP_PAIR] = _dot(hid.astype(BF16), w2_ref[s]).astype(BF16)


CMP_PAGES_PER_STEP = 32


def _compress(src, cw, *, n_seq, seq_pages, page, page_table=None):
    n_in = CMP_PAGES_PER_STEP
    assert seq_pages % n_in == 0
    set_w = N_KV_GROUPS * HEAD_DIM
    halves = page // CMP_STRIDE
    m = n_in * halves
    steps = seq_pages // n_in
    block = (1, 2 * set_w, page)
    if page_table is None:
        x_specs = [pl.BlockSpec(block, functools.partial(lambda b, s, p: (b, 0, s * n_in + p), p=p)) for p in range(n_in)]
        n_pref = 0
    else:
        x_specs = [pl.BlockSpec(block, functools.partial(lambda b, s, pt, p: (pt[b, s * n_in + p], 0, 0), p=p))
                   for p in range(n_in)]
        n_pref = 1
    cidx = lambda nd: (lambda *_: (0,) * nd)
    const = lambda a: pl.BlockSpec(a.shape, cidx(a.ndim), pipeline_mode=pl.Buffered(1))
    out_spec = pl.BlockSpec((1, m, set_w), lambda b, s, *_: (b, s, 0))
    grid_spec = pltpu.PrefetchScalarGridSpec(
        num_scalar_prefetch=n_pref,
        grid=(n_seq, steps),
        in_specs=x_specs + [const(cw['pe']), const(cw['w1']), const(cw['b1']), const(cw['w2'])],
        out_specs=[out_spec, out_spec],
        scratch_shapes=[pltpu.VMEM((4, SUBLANES, 2 * LANES), F32),
                        pltpu.VMEM((n_in, 2 * set_w // CMP_GROUP_PAIR, page, CMP_GROUP_PAIR), F32)],
    )
    call = pl.pallas_call(
        functools.partial(_compress_kernel, n_pref=n_pref, n_in=n_in),
        grid_spec=grid_spec,
        out_shape=[jax.ShapeDtypeStruct((n_seq, seq_pages * halves, set_w), BF16)] * 2,
        compiler_params=_cparams("arbitrary", "arbitrary"),
        name="compress",
    )
    args = ([page_table] if page_table is not None else []) + [src] * n_in + [cw['pe'], cw['w1'], cw['b1'], cw['w2']]
    return call(*args)


KEY_TILE = 128
N_NEAR_TILES = REL_MAX_DIST // KEY_TILE + 1
TILE_FUTURE, TILE_FAR, TILE_WINDOW = 0, N_NEAR_TILES + 1, N_NEAR_TILES + 2


def _rel_bucket(dist):
    n = jnp.maximum(dist, 0)
    max_exact = NUM_BUCKETS // 2
    nf = jnp.maximum(n, 1).astype(F32)
    large = max_exact + (jnp.log(nf / max_exact) / math.log(REL_MAX_DIST / max_exact)
                         * (NUM_BUCKETS - max_exact)).astype(jnp.int32)
    return jnp.where(n < max_exact, n, jnp.minimum(large, NUM_BUCKETS - 1))


BIAS_LEFT = 2 * LANES


def _bias_by_distance(table, reach):
    bd = table[_rel_bucket(jnp.arange(REL_MAX_DIST))].T.astype(F32)
    heads = bd.shape[0]
    return jnp.concatenate([jnp.zeros((heads, BIAS_LEFT), F32), bd,
                            jnp.broadcast_to(bd[:, -1:], (heads, reach - REL_MAX_DIST))], axis=1)


def _toeplitz(vec, c, n_rows, n_cols):
    batch = vec.shape[0]
    period = n_rows + n_cols
    assert c - n_cols + 1 >= 0 and c + n_rows <= vec.shape[1]
    b = jnp.concatenate([vec[:, c - n_cols + 1:c + 1][:, ::-1], jnp.zeros((batch, 1), vec.dtype),
                         vec[:, c + 1:c + n_rows][:, ::-1]], axis=1)
    m = jnp.tile(b, (1, n_rows))[:, :n_rows * (period - 1)].reshape(batch, n_rows, period - 1)
    return m[:, :, :n_cols]


def _bias_toeplitz(v, c, n_rows, n_cols, col_step=1):
    heads = v.shape[0]
    c += BIAS_LEFT
    if col_step == 1:
        return _toeplitz(v, c, n_rows, n_cols)
    n_a = -(-n_rows // col_step)
    first = c - col_step * (n_cols - 1)
    assert first >= 0 and c + col_step * n_a <= v.shape[1]
    w = v[:, first:c + col_step * n_a].reshape(heads, n_cols - 1 + n_a, col_step).transpose(0, 2, 1)
    t = _toeplitz(w.reshape(heads * col_step, -1), n_cols - 1, n_a, n_cols)
    return t.reshape(heads, col_step, n_a, n_cols).transpose(0, 2, 1, 3).reshape(heads, n_a * col_step, n_cols)[:, :n_rows]


CMP_BAND_W = 2 * LANES
CMP_BAND_E0 = Q_BLOCK // CMP_STRIDE - CMP_BAND_W


def _compressed_band(table):
    i = np.arange(Q_BLOCK)[:, None]
    e = CMP_BAND_E0 + np.arange(CMP_BAND_W)[None, :]
    dist = i - CMP_STRIDE * e - (CMP_STRIDE - 1)
    assert -CMP_STRIDE * (CMP_BAND_E0 - 1) - (CMP_STRIDE - 1) >= REL_MAX_DIST
    c0 = -CMP_STRIDE * CMP_BAND_E0 - (CMP_STRIDE - 1)
    v = _bias_by_distance(table, c0 + Q_BLOCK)
    far = v[:, -1][:, None, None]
    return jnp.where(dist >= 0, _bias_toeplitz(v, c0, Q_BLOCK, CMP_BAND_W, CMP_STRIDE) - far, 0.0)


def _toeplitz_tiles(table):
    i = np.arange(Q_BLOCK)[:, None]
    j = np.arange(KEY_TILE)[None, :]
    offs = [-1] + list(range(N_NEAR_TILES)) + [N_NEAR_TILES + REL_MAX_DIST // KEY_TILE, WINDOW // KEY_TILE]
    v = _bias_by_distance(table, KEY_TILE * (max(offs) + 1))
    far = v[:, -1][:, None, None]
    tiles = []
    for idx, off in enumerate(offs):
        dist = KEY_TILE * off + i - j
        ok = (dist >= 0) & ((idx != TILE_WINDOW) | (dist < WINDOW))
        tiles.append(jnp.where(ok, _bias_toeplitz(v, KEY_TILE * off, Q_BLOCK, KEY_TILE) - far, NEG_INF))
    return jnp.stack(tiles)


SEL_CHUNK = 1024
ONES_GROUP = (1, 0)
WIN_KEYS = WINDOW + Q_BLOCK


def _masked_softmax(s, mask):
    s = jnp.where(mask, s, NEG_INF)
    e = jnp.exp(s - jnp.max(s, axis=-1, keepdims=True))
    p = e / jnp.sum(e, axis=-1, keepdims=True)
    return jnp.where(mask, p, 0.0)


def _topk_masks_t(vals, key_sc, corr_sc, n_rows):
    shape = vals[0].shape
    for i, v in enumerate(vals):
        key_sc[i] = v
    corr_sc[...] = jnp.zeros_like(corr_sc)
    jg = lax.broadcasted_iota(jnp.int32, shape, 0) // SUBLANES
    jj = lax.broadcasted_iota(jnp.int32, (SUBLANES, shape[1]), 0)

    def group_body(kg, cnts):
        base = pl.multiple_of(kg * SUBLANES, SUBLANES)
        later = jnp.where(jg > kg, 1, 0)
        out = []
        for i, cnt in enumerate(cnts):
            key = key_sc[i]
            diag = key_sc[i, pl.ds(base, SUBLANES), :]
            corr = jnp.zeros(diag.shape, jnp.int32)
            for kk in range(SUBLANES):
                k_row = key_sc[i, pl.ds(base + kk, 1), :]
                k_all = jnp.broadcast_to(k_row, shape)
                cnt = cnt + jnp.where(k_all > key, 1, jnp.where(k_all == key, later, 0))
                corr = corr + jnp.where(jnp.broadcast_to(k_row, diag.shape) == diag, jnp.where(jj > kk, 1, 0), 0)
            corr_sc[i, pl.ds(base, SUBLANES), :] = corr
            out.append(cnt)
        return tuple(out)

    cnts = lax.fori_loop(0, (n_rows + SUBLANES - 1) // SUBLANES, group_body,
                         tuple(jnp.zeros(shape, jnp.int32) for _ in vals))
    return [jnp.where(cnt + corr_sc[i] < N_SELECT, 1.0, 0.0) for i, cnt in enumerate(cnts)]


def _nsa_prompt_kernel(q_ref, gate_ref, kct_ref, vc_ref, ksa_ref, vsa_ref, kwt_ref, vw_ref, tb_ref, cb_ref,
                       selmap_ref, eg_ref, o_ref, key_sc, corr_sc, unsel_sc, ocmp_sc, osel_sc, m_sc,
                       acc_sc, s0_sc, s1_sc):
    qb = pl.program_id(1)
    t0 = qb * Q_BLOCK
    gw = N_KV_GROUPS * HEAD_DIM
    n_cmp = kct_ref.shape[2]
    rows = GROUP_SIZE * Q_BLOCK

    qi = lax.broadcasted_iota(jnp.int32, (Q_BLOCK, n_cmp), 0)
    hc = lax.broadcasted_iota(jnp.int32, (Q_BLOCK, n_cmp), 1)
    mask_c = jnp.where(hc >= 1, t0 + qi - (CMP_STRIDE * hc + CMP_STRIDE - 1), -1) >= 0
    first_row = qb * (Q_BLOCK // CMP_STRIDE)
    band_tile = (first_row + Q_BLOCK // CMP_STRIDE - 1) // LANES
    band_shift = (first_row % LANES + LANES + CMP_BAND_E0 + CMP_BAND_W) % CMP_BAND_W

    def compressed_bias(head):
        band = pltpu.roll(cb_ref[head], band_shift, 1)
        lo, hi = band[:, :LANES], band[:, LANES:]
        return jnp.concatenate([jnp.where(band_tile == c, hi, jnp.where(band_tile - 1 == c, lo, 0.0))
                                for c in range(n_cmp // LANES)], axis=1)

    lane_g = lax.broadcasted_iota(jnp.int32, (Q_BLOCK, gw), 1) // HEAD_DIM

    def group_queries(g):
        return jnp.concatenate(
            [jnp.where(lane_g == g, q_ref[0, :, r * gw:(r + 1) * gw], 0.0).astype(BF16) for r in range(GROUP_SIZE)], axis=0)

    nb = selmap_ref.shape[0]
    jb = lax.broadcasted_iota(jnp.int32, (nb, Q_BLOCK), 0)
    qpos_t = t0 + lax.broadcasted_iota(jnp.int32, (nb, Q_BLOCK), 1)
    cur = qpos_t // SEL_BLOCK
    forced = jnp.where(jb == 0, 1, 0) + jnp.where(jb == cur, 1, 0) + jnp.where(jb == cur - 1, 1, 0)
    valid_t = jb * SEL_BLOCK <= qpos_t
    n_valid_blocks = (t0 + Q_BLOCK - 1) // SEL_BLOCK + 1
    n_chunks = (t0 + Q_BLOCK - 1) // SEL_CHUNK + 1

    importance = []
    for g in range(N_KV_GROUPS):
        s_c = _dot(group_queries(g), kct_ref[0])
        p_rows = []
        for r in range(GROUP_SIZE):
            bias = compressed_bias(g * GROUP_SIZE + r)
            p_rows.append(_masked_softmax(s_c[r * Q_BLOCK:(r + 1) * Q_BLOCK] + bias, mask_c).astype(BF16))
        ocmp_sc[g] = _dot(jnp.concatenate(p_rows, axis=0), vc_ref[0])
        imp = jnp.zeros((nb, Q_BLOCK), F32)
        for r in range(GROUP_SIZE):
            imp = imp + lax.dot_general(selmap_ref[...], p_rows[r], (((1,), (1,)), ((), ())), preferred_element_type=F32)
        importance.append(jnp.where(forced > 0, 1e9, jnp.where(valid_t, imp, -1e9)))

    for g, mask_t in enumerate(_topk_masks_t(importance, key_sc, corr_sc, n_valid_blocks)):
        unsel_sc[g] = (1.0 - mask_t.T).astype(BF16)

    def group_body(g, carry):
        qm = group_queries(g)
        unsel = unsel_sc[g]

        m_sc[...] = jnp.full_like(m_sc, NEG_INF)
        acc_sc[...] = jnp.zeros_like(acc_sc)
        ones_at = jnp.where(g == ONES_GROUP[0], 1, 0)
        v_col = pl.multiple_of(ones_at * gw, gw)
        tiles_per_chunk = SEL_CHUNK // KEY_TILE

        def scores(c):
            return _dot_split_rows(qm, ksa_ref[0, :gw, pl.ds(pl.multiple_of(c * SEL_CHUNK, SEL_CHUNK), SEL_CHUNK)])

        def sweep_step(c, s_cur, s_next, near):
            off = pl.multiple_of(c * SEL_CHUNK, SEL_CHUNK)
            masked = _dot(unsel, ksa_ref[0, gw:, pl.ds(off, SEL_CHUNK)])
            s_next[...] = scores(jnp.minimum(c + 1, n_chunks - 1))
            parts = []
            for r in range(GROUP_SIZE):
                extra = masked
                if near:
                    extra = masked + jnp.concatenate(
                        [tb_ref[jnp.clip(qb - (c * tiles_per_chunk + h), -1, N_NEAR_TILES) + 1, g * GROUP_SIZE + r]
                         for h in range(tiles_per_chunk)], axis=1)
                parts.append(s_cur[r * Q_BLOCK:(r + 1) * Q_BLOCK, :] + extra)
            s = jnp.concatenate(parts, axis=0)
            m_old = m_sc[...]
            m_new = jnp.maximum(m_old, jnp.max(s, axis=-1, keepdims=True))
            alpha = jnp.exp(m_old - m_new)
            p = jnp.exp(s - jnp.concatenate([m_new] * (SEL_CHUNK // LANES), axis=1))
            pv = _dot_split_rows(p.astype(BF16), vsa_ref[0, pl.ds(off, SEL_CHUNK), pl.ds(v_col, gw)])
            acc_sc[...] = jnp.concatenate([alpha] * (gw // LANES), axis=1) * acc_sc[...] + pv
            m_sc[...] = m_new

        s0_sc[...] = scores(0)

        n_far = jnp.maximum((qb - N_NEAR_TILES + 1) // tiles_per_chunk, 0)

        def step(c, s_cur, s_next):
            @pl.when(c < n_far)
            def _():
                sweep_step(c, s_cur, s_next, False)

            @pl.when((c >= n_far) & (c < n_chunks))
            def _():
                sweep_step(c, s_cur, s_next, True)

        def pair_body(pair, carry2):
            step(2 * pair, s0_sc, s1_sc)
            step(2 * pair + 1, s1_sc, s0_sc)
            return carry2

        lax.fori_loop(0, (n_chunks + 1) // 2, pair_body, 0)
        acc = acc_sc[...]
        denom = jnp.where(ones_at == 1, acc[:, ONES_GROUP[1] * HEAD_DIM:ONES_GROUP[1] * HEAD_DIM + 1],
                          acc[:, ONES_GROUP[0] * HEAD_DIM:ONES_GROUP[0] * HEAD_DIM + 1])
        osel_sc[g] = acc / denom
        return carry

    lax.fori_loop(0, N_KV_GROUPS, group_body, 0)

    n_wt = WIN_KEYS // KEY_TILE
    w_off = [pl.multiple_of(jnp.maximum(qb - (n_wt - 1) + w, 0) * KEY_TILE, KEY_TILE) for w in range(n_wt)]
    w_tile = [jnp.where(qb - (n_wt - 1) + w >= 0, TILE_WINDOW if w == 0 else n_wt - w, TILE_FUTURE) for w in range(n_wt)]
    k_win = jnp.concatenate([kwt_ref[0, :, pl.ds(o, KEY_TILE)] for o in w_off], axis=1)
    v_win = jnp.concatenate([vw_ref[0, pl.ds(o, KEY_TILE), :] for o in w_off], axis=0)
    gexp = jnp.dot(gate_ref[0], eg_ref[...], preferred_element_type=F32, precision=lax.Precision.HIGHEST)
    out = [jnp.zeros((Q_BLOCK, gw), F32) for _ in range(GROUP_SIZE)]
    for g in range(N_KV_GROUPS):
        s_w = _dot(group_queries(g), k_win)
        pw_rows = []
        for r in range(GROUP_SIZE):
            bias = jnp.concatenate([tb_ref[ti, g * GROUP_SIZE + r] for ti in w_tile], axis=1)
            s_r = s_w[r * Q_BLOCK:(r + 1) * Q_BLOCK] + bias
            e = jnp.exp(s_r - jnp.max(s_r, axis=-1, keepdims=True))
            pw_rows.append((e / jnp.sum(e, axis=-1, keepdims=True)).astype(BF16))
        o_win = _dot(jnp.concatenate(pw_rows, axis=0), v_win)
        for r in range(GROUP_SIZE):
            rs = slice(r * Q_BLOCK, (r + 1) * Q_BLOCK)
            mix = (gexp[:, (3 * r) * gw:(3 * r + 1) * gw] * ocmp_sc[g, rs, :]
                   + gexp[:, (3 * r + 1) * gw:(3 * r + 2) * gw] * osel_sc[g, rs, :]
                   + gexp[:, (3 * r + 2) * gw:(3 * r + 3) * gw] * o_win[rs])
            out[r] = out[r] + jnp.where(lane_g == g, mix, 0.0)
    for r in range(GROUP_SIZE):
        o_ref[0, :, r * gw:(r + 1) * gw] = out[r].astype(BF16)


def _nsa_prompt(q, gates, kct, vc, ksa, vsa, kwt, selwin, tiles, band, selmap, eg):
    n, t, hw = q.shape
    gw = N_KV_GROUPS * HEAD_DIM
    n_cmp = kct.shape[2]
    rows = GROUP_SIZE * Q_BLOCK
    per_n = lambda shape, col=0: pl.BlockSpec(shape, lambda b, j: (b, 0, col), pipeline_mode=pl.Buffered(1))
    blk = lambda w: pl.BlockSpec((1, Q_BLOCK, w), lambda b, j: (b, j, 0))
    return pl.pallas_call(
        _nsa_prompt_kernel,
        grid=(n, t // Q_BLOCK),
        in_specs=[blk(hw), blk(gates.shape[2]),
                  per_n((1, gw, n_cmp)), per_n((1, n_cmp, gw)), per_n((1, ksa.shape[1], t)), per_n((1, t, 2 * gw)),
                  per_n((1, gw, t)), per_n((1, t, gw), 3),
                  _const_spec(tiles.shape), _const_spec(band.shape), _const_spec(selmap.shape), _const_spec(eg.shape)],
        out_specs=blk(hw),
        out_shape=jax.ShapeDtypeStruct((n, t, hw), BF16),
        scratch_shapes=[pltpu.VMEM((N_KV_GROUPS, selmap.shape[0], Q_BLOCK), F32),
                        pltpu.VMEM((N_KV_GROUPS, selmap.shape[0], Q_BLOCK), jnp.int32),
                        pltpu.VMEM((N_KV_GROUPS, Q_BLOCK, selmap.shape[0]), BF16),
                        pltpu.VMEM((N_KV_GROUPS, rows, gw), F32), pltpu.VMEM((N_KV_GROUPS, rows, gw), F32),
                        pltpu.VMEM((rows, LANES), F32), pltpu.VMEM((rows, gw), F32),
                        pltpu.VMEM((rows, SEL_CHUNK), F32), pltpu.VMEM((rows, SEL_CHUNK), F32)],
        compiler_params=_cparams("arbitrary", "arbitrary"),
        name="nsa_prompt",
    )(q, gates, kct, vc, ksa, vsa, kwt, selwin, tiles, band, selmap, eg)


DEC_T_PAD = SUBLANES
SAMPLE_BLOCKS_PAD = 256
NEW_KEYS_PAD = LANES


def _nsa_sample_kernel(*refs, n_pages, n_blocks):
    pt_ref, q_ref, gate_ref, kct_ref, vc_ref = refs[:5]
    page_refs = refs[5:5 + n_pages]
    (bsel_ref, bcmp_ref, bwin_ref, bnew_ref, new_ref, cwin_ref, selmap_ref, eg_ref, o_ref,
     key_sc, corr_sc, sel_sc, m_sc, l_sc, acc_sc, oc_sc) = refs[5 + n_pages:]
    c = pl.program_id(1)
    gw = N_KV_GROUPS * HEAD_DIM
    rows = N_HEADS * DEC_T_PAD
    page = page_refs[0].shape[2]
    lane_g = lax.broadcasted_iota(jnp.int32, (DEC_T_PAD, gw), 1) // HEAD_DIM
    qm = jnp.concatenate(
        [jnp.where(lane_g == g, q_ref[0, :, r * gw:(r + 1) * gw], 0.0)
         for r in range(GROUP_SIZE) for g in range(N_KV_GROUPS)], axis=0).astype(BF16)
    nt = (((1,), (1,)), ((), ()))

    @pl.when(c == 0)
    def _():
        p_c = _masked_softmax(_dot(qm, kct_ref[0]) + bcmp_ref[...], bcmp_ref[...] > 0.5 * NEG_INF).astype(BF16)
        oc_sc[...] = _dot(p_c, vc_ref[0])
        imp_all = lax.dot_general(selmap_ref[...], p_c, nt, preferred_element_type=F32)
        per_r = rows // GROUP_SIZE
        imp = imp_all
        for r in range(1, GROUP_SIZE):
            imp = imp + pltpu.roll(imp_all, r * per_r, 1)
        jb = lax.broadcasted_iota(jnp.int32, imp.shape, 0)
        last = n_blocks - 1
        forced = jnp.where(jb == 0, 1, 0) + jnp.where(jb == last, 1, 0) + jnp.where(jb == last - 1, 1, 0)
        imp = jnp.where(forced > 0, 1e9, jnp.where(jb <= last, imp, -1e9))
        live = key_sc.shape[1]
        picked_t = _topk_masks_t([imp[:live]], key_sc, corr_sc, n_blocks)[0]
        sel_sc[...] = jnp.concatenate([picked_t, jnp.zeros((imp.shape[0] - live, rows), F32)], axis=0).T.astype(BF16)
        m_sc[...] = jnp.full_like(m_sc, NEG_INF)
        l_sc[...] = jnp.zeros_like(l_sc)
        acc_sc[...] = jnp.zeros_like(acc_sc)

    def online_update(s, pv_of):
        m_old = m_sc[...]
        m_new = jnp.maximum(m_old, jnp.max(s, axis=-1, keepdims=True))
        alpha = jnp.exp(m_old - m_new)
        p = jnp.exp(s - m_new)
        l_sc[...] = alpha * l_sc[...] + jnp.sum(p, axis=-1, keepdims=True)
        acc_sc[...] = alpha * acc_sc[...] + pv_of(p.astype(BF16))
        m_sc[...] = m_new

    keys = n_pages * page
    s = jnp.concatenate([_dot(qm, pr[0, :gw, :].astype(BF16)) for pr in page_refs], axis=1)
    eb = lax.broadcasted_iota(jnp.int32, (SAMPLE_BLOCKS_PAD, keys), 0)
    ek = lax.broadcasted_iota(jnp.int32, (SAMPLE_BLOCKS_PAD, keys), 1)
    expand = jnp.where(eb == c * (keys // SEL_BLOCK) + ek // SEL_BLOCK, 1.0, 0.0).astype(BF16)
    picked = _dot(sel_sc[...], expand) > 0.5
    v_t = jnp.concatenate([pr[0, gw:, :].astype(BF16) for pr in page_refs], axis=1)
    online_update(jnp.where(picked, s + bsel_ref[...], NEG_INF),
                  lambda p: lax.dot_general(p, v_t, nt, preferred_element_type=F32))

    @pl.when(c == pl.num_programs(1) - 1)
    def _():
        s_new = lax.dot_general(qm, new_ref[0, :, :gw], nt, preferred_element_type=F32) + bnew_ref[0]
        online_update(s_new, lambda p: _dot(p, new_ref[0, :, gw:2 * gw]))
        o_sel = acc_sc[...] / l_sc[...]

        s_w = _dot(qm, cwin_ref[0, :gw, :].astype(BF16)) + bwin_ref[...]
        s_wn = lax.dot_general(qm, new_ref[0, :, 2 * gw:3 * gw], nt, preferred_element_type=F32) + bnew_ref[1]
        m_w = jnp.maximum(jnp.max(s_w, axis=-1, keepdims=True), jnp.max(s_wn, axis=-1, keepdims=True))
        e_w = jnp.exp(s_w - m_w)
        e_wn = jnp.exp(s_wn - m_w)
        l_w = jnp.sum(e_w, axis=-1, keepdims=True) + jnp.sum(e_wn, axis=-1, keepdims=True)
        o_win = lax.dot_general((e_w / l_w).astype(BF16), cwin_ref[0, gw:, :].astype(BF16), nt,
                                preferred_element_type=F32) + _dot((e_wn / l_w).astype(BF16), new_ref[0, :, 3 * gw:])

        gexp = jnp.dot(gate_ref[0], eg_ref[...], preferred_element_type=F32, precision=lax.Precision.HIGHEST)
        o_cmp = oc_sc[...]
        for r in range(GROUP_SIZE):
            out_r = jnp.zeros((DEC_T_PAD, gw), F32)
            for g in range(N_KV_GROUPS):
                rs = slice((r * N_KV_GROUPS + g) * DEC_T_PAD, (r * N_KV_GROUPS + g + 1) * DEC_T_PAD)
                mix = (gexp[:, (3 * r) * gw:(3 * r + 1) * gw] * o_cmp[rs]
                       + gexp[:, (3 * r + 1) * gw:(3 * r + 2) * gw] * o_sel[rs]
                       + gexp[:, (3 * r + 2) * gw:(3 * r + 3) * gw] * o_win[rs])
                out_r = out_r + jnp.where(lane_g == g, mix, 0.0)
            o_ref[0, :, r * gw:(r + 1) * gw] = out_r


PAGES_PER_STEP = 16


def _nsa_sample(page_table, q, gates, kct, vc, cache, bias, new_rows, cache_win, selmap, eg, n_blocks):
    n, _, hw = q.shape
    gw = N_KV_GROUPS * HEAD_DIM
    n_cmp = kct.shape[2]
    rows = N_HEADS * DEC_T_PAD
    page = cache.shape[2]
    steps = page_table.shape[1] // PAGES_PER_STEP
    keys = PAGES_PER_STEP * page
    per_n = lambda a: pl.BlockSpec((1,) + a.shape[1:], lambda b, c, pt: (b, 0, 0))
    cidx = lambda nd: (lambda *_: (0,) * nd)
    const = lambda a: pl.BlockSpec(a.shape, cidx(a.ndim), pipeline_mode=pl.Buffered(1))
    page_specs = [pl.BlockSpec((1, 2 * gw, page), functools.partial(
        lambda b, c, pt, p: (pt[b, c * PAGES_PER_STEP + p], 1, 0), p=p)) for p in range(PAGES_PER_STEP)]
    grid_spec = pltpu.PrefetchScalarGridSpec(
        num_scalar_prefetch=1,
        grid=(n, steps),
        in_specs=[per_n(q), per_n(gates), per_n(kct), per_n(vc)] + page_specs + [
            pl.BlockSpec((rows, keys), lambda b, c, pt: (0, c)), const(bias['cmp']), const(bias['win']), const(bias['new']),
            per_n(new_rows), per_n(cache_win), const(selmap), const(eg)],
        out_specs=pl.BlockSpec((1, DEC_T_PAD, hw), lambda b, c, pt: (b, 0, 0)),
        scratch_shapes=[pltpu.VMEM((1, -(-n_blocks // SUBLANES) * SUBLANES, rows), F32),
                        pltpu.VMEM((1, -(-n_blocks // SUBLANES) * SUBLANES, rows), jnp.int32),
                        pltpu.VMEM((rows, SAMPLE_BLOCKS_PAD), BF16),
                        pltpu.VMEM((rows, 1), F32), pltpu.VMEM((rows, 1), F32), pltpu.VMEM((rows, gw), F32),
                        pltpu.VMEM((rows, gw), F32)],
    )
    return pl.pallas_call(
        functools.partial(_nsa_sample_kernel, n_pages=PAGES_PER_STEP, n_blocks=n_blocks),
        grid_spec=grid_spec,
        out_shape=jax.ShapeDtypeStruct((n, DEC_T_PAD, hw), F32),
        compiler_params=_cparams("arbitrary", "arbitrary"),
        name="nsa_sample",
    )(page_table, q, gates, kct, vc, *([cache] * PAGES_PER_STEP), bias['sel'], bias['cmp'], bias['win'], bias['new'],
      new_rows, cache_win, selmap, eg)


def _sample_bias(table, past, dec_t, win_len, n_cmp_rows):
    t = np.arange(DEC_T_PAD)[:, None]
    v = _bias_by_distance(table, max(past + DEC_T_PAD, REL_MAX_DIST + 1))

    def rows(c, n_cols, col_step, ok):
        b = jnp.where(ok, _bias_toeplitz(v, c, DEC_T_PAD, n_cols, col_step), NEG_INF)
        b = b.reshape(N_KV_GROUPS, GROUP_SIZE, DEC_T_PAD, -1).transpose(1, 0, 2, 3)
        return b.reshape(N_HEADS * DEC_T_PAD, -1)

    h = np.arange(n_cmp_rows)[None, :]
    dist_c = past + t - (CMP_STRIDE * h + CMP_STRIDE - 1)
    j = np.arange(win_len)[None, :]
    dist_w = win_len + t - j
    tn = np.arange(NEW_KEYS_PAD)[None, :]
    new = rows(0, NEW_KEYS_PAD, 1, (tn <= t) & (tn < dec_t))
    return dict(sel=rows(past, past, 1, np.ones((DEC_T_PAD, past), bool)),
                cmp=rows(past - (CMP_STRIDE - 1), n_cmp_rows, CMP_STRIDE, (h >= 1) & (dist_c >= 0)),
                win=rows(win_len, win_len, 1, (dist_w < WINDOW) & (past - win_len + j >= 0)), new=jnp.stack([new, new]))


GATE_LANES = LANES


def _prepare(p):
    d = p['norm_pre_mix'].shape[1]
    hw = N_HEADS * HEAD_DIM
    gw = N_KV_GROUPS * HEAD_DIM
    row = lambda v: v.reshape(1, -1)
    w = {}
    w['rg'] = dict(
        g_pre=row(p['norm_pre_mix'][0]), w_in=p['rg_w_in'][0].astype(BF16), conv_w=p['rg_conv_w'][0],
        conv_b=row(p['rg_conv_b'][0]),
        w_ax=jnp.concatenate([p['rg_w_a'][0], p['rg_w_x'][0]], axis=2).astype(BF16),
        b_ax=jnp.stack([p['rg_b_a'][0], p['rg_b_x'][0]]), lam=row(p['rg_lambda'][0]),
        w_out=p['rg_w_out'][0].astype(BF16), g_post=row(p['norm_post_mix'][0]))
    w['mlp'] = [dict(g1=row(p['norm_pre_mlp'][l]), wup=p['w_mlp_up'][l].astype(BF16),
                     wdn=p['w_mlp_down'][l].astype(BF16), g2=row(p['norm_post_mlp'][l])) for l in range(2)]
    wqg = p['nsa_w_qg'][0]
    n_gate = wqg.shape[1] - hw
    w['proj'] = dict(
        gkv=row(p['kv_norm']), wkv=p['w_kv'].astype(BF16), gq=row(p['norm_pre_mix'][1]),
        wq=wqg[:, :hw].reshape(d, N_KV_GROUPS, GROUP_SIZE, HEAD_DIM).transpose(0, 2, 1, 3).reshape(d, hw).astype(BF16),
        wg=jnp.pad(wqg[:, hw:], ((0, 0), (0, GATE_LANES - n_gate))).astype(BF16),
        bg=jnp.pad(p['nsa_b_g'][0], (0, GATE_LANES - n_gate)).reshape(1, -1))
    w['wo'] = p['nsa_w_o'][0].reshape(N_KV_GROUPS, GROUP_SIZE, HEAD_DIM, d).transpose(1, 0, 2, 3).reshape(hw, d).astype(BF16)
    w['g_post_mix1'] = row(p['norm_post_mix'][1])
    eye2 = jnp.eye(2, dtype=F32)
    half = L_CMP // CMP_STRIDE
    w['cmp'] = dict(
        pe=jnp.broadcast_to(p['cmp_pe'].reshape(2, half, CMP_STRIDE, 1, HEAD_DIM),
                            (2, half, CMP_STRIDE, 2, HEAD_DIM)).reshape(2, half, 1, CMP_STRIDE * CMP_GROUP_PAIR),
        w1=jnp.einsum('sftdc,gh->sftgdhc', p['cmp_w1'].reshape(2, half, CMP_STRIDE, HEAD_DIM, -1), eye2)
        .reshape(2, half, CMP_STRIDE * CMP_GROUP_PAIR, -1).astype(BF16),
        b1=jnp.tile(p['cmp_b1'], (1, 2)).reshape(2, 1, -1),
        w2=jnp.einsum('scd,gh->sgchd', p['cmp_w2'], eye2).reshape(2, -1, CMP_GROUP_PAIR).astype(BF16))
    w['table'] = p['rel_bias_table']
    w['tiles'] = _toeplitz_tiles(p['rel_bias_table'])
    w['band'] = _compressed_band(p['rel_bias_table'])
    h_i = jnp.arange(N_HEADS)
    src = (h_i[:, None] * 3 + jnp.arange(3)[None, :]).reshape(-1)
    dst0 = (((h_i % GROUP_SIZE)[:, None] * 3 + jnp.arange(3)[None, :]) * gw
            + (h_i // GROUP_SIZE)[:, None] * HEAD_DIM).reshape(-1)
    lanes = jnp.arange(3 * GROUP_SIZE * gw)
    w['eg'] = jnp.zeros((GATE_LANES, lanes.shape[0]), F32).at[src].set(
        ((lanes[None, :] >= dst0[:, None]) & (lanes[None, :] < dst0[:, None] + HEAD_DIM)).astype(F32))
    return w


def _selmap_t(n_blocks, n_cmp_rows):
    ratio = SEL_BLOCK // CMP_STRIDE
    h = jnp.arange(n_cmp_rows)[None, :]
    j = jnp.arange(n_blocks)[:, None]
    m = ((h - 1) // ratio == j).astype(F32) + (h // ratio == j).astype(F32)
    return jnp.where(h >= 1, m, 0.0).astype(BF16)


def _trunk_prompt(x, w):
    n, t, d = x.shape
    gw = N_KV_GROUPS * HEAD_DIM
    zeros = lambda r: jnp.zeros((n, r, d), F32)
    x1, h_last, c_last = _rglru_layer(x, zeros(1), zeros(CONV_W - 1), w['rg'], stride=1, tm=256)
    m0 = w['mlp'][0]
    x2 = _mlp_layer(x1.reshape(n * t, d), m0['g1'], m0['wup'], m0['wdn'], m0['g2'], tm=512)
    pj = w['proj']
    kv4, _, selwin, q, gates, kst, kwt, vsa, kv_t, win_t = _proj_layer(
        x2, pj['gkv'], pj['wkv'], pj['gq'], pj['wq'], pj['wg'], pj['bg'], tm=512, seq_len=t)
    nh = t // CMP_STRIDE
    n_blocks = t // SEL_BLOCK
    kc, vc = _compress(kv_t, w['cmp'], n_seq=n, seq_pages=t // KEY_TILE, page=KEY_TILE)
    selneg = jnp.where(jnp.arange(t)[None, :] // SEL_BLOCK == jnp.arange(n_blocks)[:, None], NEG_INF, 0.0).astype(BF16)
    ksa = jnp.concatenate([kst, jnp.broadcast_to(selneg, (n,) + selneg.shape)], axis=1)
    attn = _nsa_prompt(q.reshape(n, t, -1), gates.reshape(n, t, -1), kc.transpose(0, 2, 1), vc, ksa, vsa.reshape(n, t, -1),
                       kwt, selwin.reshape(n, t, -1), w['tiles'], w['band'], _selmap_t(n_blocks, nh), w['eg'])
    m1 = w['mlp'][1]
    y = _mlp_layer(x2, m1['g1'], m1['wup'], m1['wdn'], m1['g2'], tm=512,
                   mix=(attn.reshape(n * t, -1), w['wo'], w['g_post_mix1']))
    wlen = min(WINDOW, t)
    rows_of = lambda a, sets: a.reshape(n, sets, N_KV_GROUPS, HEAD_DIM, -1).transpose(0, 4, 1, 2, 3)
    return (y.reshape(n, t, d), h_last.reshape(1, n, d), c_last.reshape(1, n, CONV_W - 1, d),
            rows_of(kv_t, 4), rows_of(win_t[:, :, t - wlen:], 2))


def _trunk_sample(x, h0, c0, cache_kv, cache_win, page_table, w):
    n, t, d = x.shape
    n_phys, page = cache_kv.shape[:2]
    past = page_table.shape[1] * page
    win_len = cache_win.shape[1]
    assert t < CMP_STRIDE and past % CMP_STRIDE == 0 and t <= DEC_T_PAD and page % CMP_STRIDE == 0
    assert page_table.shape[1] % PAGES_PER_STEP == 0 and page == KEY_TILE
    to_tn = lambda a: a.transpose(1, 0, 2).reshape(a.shape[0] * a.shape[1], -1)
    to_nt = lambda a, k: a.reshape(k, n, -1).transpose(1, 0, 2)
    tail = CONV_W - 1
    x1, h_last, c_last = _rglru_layer(to_tn(x)[None], h0[None], to_tn(c0)[None], w['rg'], stride=n, tm=t * n)
    m0 = w['mlp'][0]
    x2 = _mlp_layer(x1[0], m0['g1'], m0['wup'], m0['wdn'], m0['g2'], tm=t * n)
    pj = w['proj']
    kv4, win, selwin, q, gates = _proj_layer(x2, pj['gkv'], pj['wkv'], pj['gq'], pj['wq'], pj['wg'], pj['bg'], tm=t * n)

    row_w = cache_kv.shape[2] * cache_kv.shape[3] * cache_kv.shape[4]
    nh = past // CMP_STRIDE
    cache_t = cache_kv.reshape(n_phys, page, row_w).transpose(0, 2, 1)
    cache_win_t = cache_win.reshape(n, win_len, -1).transpose(0, 2, 1)
    kc, vc = _compress(cache_t, w['cmp'], n_seq=n, seq_pages=page_table.shape[1], page=page, page_table=page_table)
    pad_t = lambda a, k: jnp.pad(a, ((0, 0), (0, k - a.shape[1]), (0, 0)))
    n_blocks = -(-(past + t) // SEL_BLOCK)
    attn = _nsa_sample(
        page_table, pad_t(to_nt(q, t).astype(F32), DEC_T_PAD), pad_t(to_nt(gates, t), DEC_T_PAD), kc.transpose(0, 2, 1), vc,
        cache_t, _sample_bias(w['table'], past, t, win_len, nh),
        pad_t(to_nt(selwin, t), NEW_KEYS_PAD), cache_win_t, _selmap_t(SAMPLE_BLOCKS_PAD, nh), w['eg'], n_blocks)
    m1 = w['mlp'][1]
    y = _mlp_layer(x2, m1['g1'], m1['wup'], m1['wdn'], m1['g2'], tm=t * n,
                   mix=(to_tn(attn[:, :t]).astype(BF16), w['wo'], w['g_post_mix1']))
    win_rows = to_nt(win, t).reshape(n, t, 2, N_KV_GROUPS, HEAD_DIM)
    win_new = jnp.concatenate([cache_win, win_rows], axis=1)[:, t:]
    return (to_nt(y, t), h_last, to_nt(c_last[0], tail)[None],
            to_nt(kv4, t).reshape(n, t, 4, N_KV_GROUPS, HEAD_DIM), win_new)


def kernel(x_prompt, x_sample, state_rglru_h, state_rglru_conv, cache_kv, cache_win, page_table, norm_pre_mix,
           norm_post_mix, norm_pre_mlp, norm_post_mlp, w_mlp_up, w_mlp_down, rg_w_in, rg_conv_w, rg_conv_b, rg_w_a, rg_b_a,
           rg_w_x, rg_b_x, rg_lambda, rg_w_out, kv_norm, w_kv, cmp_pe, cmp_w1, cmp_b1, cmp_w2, nsa_w_qg, nsa_b_g, nsa_w_o,
           rel_bias_table):
    w = _prepare(dict(
        norm_pre_mix=norm_pre_mix, norm_post_mix=norm_post_mix, norm_pre_mlp=norm_pre_mlp, norm_post_mlp=norm_post_mlp,
        w_mlp_up=w_mlp_up, w_mlp_down=w_mlp_down, rg_w_in=rg_w_in, rg_conv_w=rg_conv_w, rg_conv_b=rg_conv_b,
        rg_w_a=rg_w_a, rg_b_a=rg_b_a, rg_w_x=rg_w_x, rg_b_x=rg_b_x, rg_lambda=rg_lambda, rg_w_out=rg_w_out,
        kv_norm=kv_norm, w_kv=w_kv, cmp_pe=cmp_pe, cmp_w1=cmp_w1, cmp_b1=cmp_b1, cmp_w2=cmp_w2, nsa_w_qg=nsa_w_qg,
        nsa_b_g=nsa_b_g, nsa_w_o=nsa_w_o, rel_bias_table=rel_bias_table))
    y_p, p_h, p_conv, p_kv, p_win = _trunk_prompt(x_prompt, w)
    y_s, s_h, s_conv, s_kv, s_win = _trunk_sample(x_sample, state_rglru_h[0], state_rglru_conv[0], cache_kv, cache_win,
                                                  page_table, w)
    return (y_p, y_s, p_h, p_conv, p_kv, p_win, s_h, s_conv, s_kv, s_win)
```

```python
import functools
import math

import jax
import jax.numpy as jnp
import numpy as np
from jax import lax
from jax.experimental import pallas as pl
from jax.experimental.pallas import tpu as pltpu

F32 = jnp.float32
BF16 = jnp.bfloat16

N_HEADS = 16
N_KV_GROUPS = 4
GROUP_SIZE = N_HEADS // N_KV_GROUPS
HEAD_DIM = 64
N_RNN_BLOCKS = 4
CONV_W = 4
LRU_C = 8.0
L_CMP = 32
CMP_STRIDE = 16
SEL_BLOCK = 64
N_SELECT = 16
WINDOW = 512
Q_BLOCK = 128
NUM_BUCKETS = 32
REL_MAX_DIST = 1024
EPS = 1e-6
NEG_INF = -1e30

SUBLANES = 8
LANES = 128
VMEM_LIMIT_BYTES = 56 * 1024 * 1024


def _cparams(*sem):
    return pltpu.CompilerParams(dimension_semantics=sem, vmem_limit_bytes=VMEM_LIMIT_BYTES)


def _const_spec(shape):
    nd = len(shape)
    return pl.BlockSpec(shape, lambda *_: (0,) * nd, pipeline_mode=pl.Buffered(1))


def _rms(x, g):
    return x * lax.rsqrt(jnp.mean(x * x, axis=-1, keepdims=True) + EPS) * g


def _dot(a, b):
    return jnp.dot(a, b, preferred_element_type=F32)


def _dot_split_rows(a, b):
    half = a.shape[0] // 2
    return jnp.concatenate([_dot(a[:half], b), _dot(a[half:], b)], axis=0)


def _rglru_kernel(x_ref, h0_ref, c0_ref, gpre_ref, win_ref, cw_ref, cb_ref, wax_ref, bax_ref, lam_ref,
                  wout_ref, gpost_ref, x1_ref, hlast_ref, clast_ref, xcat_sc, hprev_sc, *, stride, tm, pad):
    d = x_ref.shape[-1]
    tail = (CONV_W - 1) * stride
    j = pl.program_id(1)

    @pl.when(j == 0)
    def _():
        xcat_sc[pad - tail:pad, :] = c0_ref[0]
        hprev_sc[...] = jnp.zeros_like(hprev_sc)
        hprev_sc[tm - stride:tm, :] = h0_ref[0]

    x = x_ref[0]
    u = _rms(x, gpre_ref[...]).astype(BF16)
    proj = _dot(u, win_ref[...])
    gate = jax.nn.gelu(proj[:, :d])
    xcat_sc[pad:pad + tm, :] = proj[:, d:]
    xc = cb_ref[...] + cw_ref[CONV_W - 1:CONV_W, :] * xcat_sc[pad:pad + tm, :]
    for lag in range(1, CONV_W):
        xc = xc + cw_ref[CONV_W - 1 - lag:CONV_W - lag, :] * xcat_sc[pad - lag * stride:pad - lag * stride + tm, :]
    xcat_sc[pad - tail:pad, :] = xcat_sc[pad + tm - tail:pad + tm, :]
    clast_ref[0] = xcat_sc[pad - tail:pad, :]

    xcb = xc.astype(BF16)
    bw = d // N_RNN_BLOCKS
    ra, rx = [], []
    for blk in range(N_RNN_BLOCKS):
        pre = _dot(xcb[:, blk * bw:(blk + 1) * bw], wax_ref[blk])
        ra.append(pre[:, :bw])
        rx.append(pre[:, bw:])
    r = jax.nn.sigmoid(jnp.concatenate(ra, axis=1) + bax_ref[0:1, :])
    i = jax.nn.sigmoid(jnp.concatenate(rx, axis=1) + bax_ref[1:2, :])
    log_a = r * (-LRU_C * jax.nn.softplus(-lam_ref[...]))
    a = jnp.exp(log_a)
    b = jnp.sqrt(-jnp.tanh(log_a) * (a * a + 1.0)) * i * xc

    row = lax.broadcasted_iota(jnp.int32, (tm, d), 0)
    b = b + jnp.where(row < stride, a * pltpu.roll(hprev_sc[...], stride, 0), 0.0)
    s = stride
    while s < tm:
        keep = row >= s
        b = a * jnp.where(keep, pltpu.roll(b, s, 0), 0.0) + b
        a = a * jnp.where(keep, pltpu.roll(a, s, 0), 1.0)
        s *= 2
    h = b
    hprev_sc[...] = h
    hlast_ref[0] = h[tm - stride:tm, :]

    y = _dot((h * gate).astype(BF16), wout_ref[...])
    x1_ref[0] = x + _rms(y, gpost_ref[...])


def _rglru_layer(x, h0, c0, w, *, stride, tm):
    n, rows, d = x.shape
    tail = (CONV_W - 1) * stride
    pad = -(-tail // SUBLANES) * SUBLANES
    kern = functools.partial(_rglru_kernel, stride=stride, tm=tm, pad=pad)
    seq = lambda shape: pl.BlockSpec(shape, lambda b, j: (b, j, 0))
    per_n = lambda shape: pl.BlockSpec(shape, lambda b, j: (b, 0, 0))
    return pl.pallas_call(
        kern,
        grid=(n, rows // tm),
        in_specs=[seq((1, tm, d)), per_n((1, stride, d)), per_n((1, tail, d)),
                  _const_spec((1, d)), _const_spec(w['w_in'].shape), _const_spec((CONV_W, d)), _const_spec((1, d)),
                  _const_spec(w['w_ax'].shape), _const_spec((2, d)), _const_spec((1, d)),
                  _const_spec(w['w_out'].shape), _const_spec((1, d))],
        out_specs=[seq((1, tm, d)), per_n((1, stride, d)), per_n((1, tail, d))],
        out_shape=[jax.ShapeDtypeStruct((n, rows, d), F32), jax.ShapeDtypeStruct((n, stride, d), F32),
                   jax.ShapeDtypeStruct((n, tail, d), F32)],
        scratch_shapes=[pltpu.VMEM((pad + tm, d), F32), pltpu.VMEM((tm, d), F32)],
        compiler_params=_cparams("arbitrary", "arbitrary"),
        name="rglru_layer",
    )(x, h0, c0, w['g_pre'], w['w_in'], w['conv_w'], w['conv_b'], w['w_ax'], w['b_ax'], w['lam'], w['w_out'], w['g_post'])


MLP_HIDDEN_CHUNK = 1024


def _mlp_body(x, g1_ref, wup_ref, wdn_ref, g2_ref):
    u = _rms(x, g1_ref[...]).astype(BF16)
    f = wup_ref.shape[1]
    acc = jnp.zeros(x.shape, F32)
    for c in range(f // MLP_HIDDEN_CHUNK):
        cols = slice(c * MLP_HIDDEN_CHUNK, (c + 1) * MLP_HIDDEN_CHUNK)
        hid = jnp.maximum(_dot(u, wup_ref[:, cols]), 0.0)
        acc = acc + _dot((hid * hid).astype(BF16), wdn_ref[cols, :])
    return x + _rms(acc, g2_ref[...])


def _mlp_kernel(x_ref, g1_ref, wup_ref, wdn_ref, g2_ref, o_ref):
    o_ref[...] = _mlp_body(x_ref[...], g1_ref, wup_ref, wdn_ref, g2_ref)


def _mix_mlp_kernel(x_ref, a_ref, wo_ref, gmix_ref, g1_ref, wup_ref, wdn_ref, g2_ref, o_ref):
    x = x_ref[...] + _rms(_dot(a_ref[...], wo_ref[...]), gmix_ref[...])
    o_ref[...] = _mlp_body(x, g1_ref, wup_ref, wdn_ref, g2_ref)


def _mlp_layer(x, g1, wup, wdn, g2, *, tm, mix=None):
    rows, d = x.shape
    f = wup.shape[1]
    tok = pl.BlockSpec((tm, d), lambda i: (i, 0))
    mlp_specs = [_const_spec((1, d)), _const_spec((d, f)), _const_spec((f, d)), _const_spec((1, d))]
    if mix is None:
        kern, specs, args = _mlp_kernel, [tok] + mlp_specs, (x, g1, wup, wdn, g2)
    else:
        a, wo, gmix = mix
        kern = _mix_mlp_kernel
        specs = [tok, pl.BlockSpec((tm, a.shape[1]), lambda i: (i, 0)), _const_spec(wo.shape), _const_spec((1, d))] + mlp_specs
        args = (x, a, wo, gmix, g1, wup, wdn, g2)
    return pl.pallas_call(
        kern,
        grid=(rows // tm,),
        in_specs=specs,
        out_specs=tok,
        out_shape=jax.ShapeDtypeStruct((rows, d), F32),
        compiler_params=_cparams("arbitrary"),
        name="mlp_layer",
    )(*args)


def _proj_kernel(x_ref, gkv_ref, wkv_ref, gq_ref, wq_ref, wg_ref, bg_ref, kv_ref, win_ref, sw_ref, q_ref, gate_ref,
                 *kt_refs):
    x = x_ref[...]
    rows = _dot(_rms(x, gkv_ref[...]).astype(BF16), wkv_ref[...])
    n_kv = kv_ref.shape[1]
    gw = N_KV_GROUPS * HEAD_DIM
    kv_ref[...] = rows[:, :n_kv]
    win_ref[...] = rows[:, n_kv:]
    sw_ref[...] = rows[:, n_kv // 2:].astype(BF16)
    if kt_refs:
        kst_ref, kwt_ref, vsa_ref, kvt_ref, wint_ref = kt_refs
        rows_t = rows.T
        kvt_ref[0] = rows_t[:n_kv]
        wint_ref[0] = rows_t[n_kv:]
        kst_ref[0] = rows_t[2 * gw:3 * gw].astype(BF16)
        kwt_ref[0] = rows_t[4 * gw:5 * gw].astype(BF16)
        v_sel = rows[:, 3 * gw:4 * gw]
        lane_grp = lax.broadcasted_iota(jnp.int32, v_sel.shape, 1) // HEAD_DIM
        vsa_ref[...] = jnp.concatenate([jnp.where(lane_grp == ONES_GROUP[0], 1.0, v_sel),
                                        jnp.where(lane_grp == ONES_GROUP[1], 1.0, v_sel)], axis=1).astype(BF16)
    u = _rms(x, gq_ref[...]).astype(BF16)
    q_ref[...] = (_dot(u, wq_ref[...]) * HEAD_DIM ** -0.5).astype(BF16)
    gate_ref[...] = jax.nn.sigmoid(_dot(u, wg_ref[...]) + bg_ref[...])


def _proj_layer(x, gkv, wkv, gq, wq, wg, bg, *, tm, seq_len=None):
    rows, d = x.shape
    n_all = wkv.shape[1]
    n_kv = 4 * N_KV_GROUPS * HEAD_DIM
    gw = N_KV_GROUPS * HEAD_DIM
    tok = lambda w: pl.BlockSpec((tm, w), lambda i: (i, 0))
    out_specs = [tok(n_kv), tok(n_all - n_kv), tok(n_all - n_kv // 2), tok(wq.shape[1]), tok(wg.shape[1])]
    out_shape = [jax.ShapeDtypeStruct((rows, n_kv), F32), jax.ShapeDtypeStruct((rows, n_all - n_kv), F32),
                 jax.ShapeDtypeStruct((rows, n_all - n_kv // 2), BF16), jax.ShapeDtypeStruct((rows, wq.shape[1]), BF16),
                 jax.ShapeDtypeStruct((rows, wg.shape[1]), F32)]
    if seq_len is not None:
        tiles = seq_len // tm
        kt = pl.BlockSpec((1, gw, tm), lambda i: (i // tiles, 0, i % tiles))
        ktw = lambda w: pl.BlockSpec((1, w, tm), lambda i: (i // tiles, 0, i % tiles))
        out_specs += [kt, kt, tok(2 * gw), ktw(n_kv), ktw(n_all - n_kv)]
        out_shape += [jax.ShapeDtypeStruct((rows // seq_len, gw, seq_len), BF16)] * 2
        out_shape += [jax.ShapeDtypeStruct((rows, 2 * gw), BF16),
                      jax.ShapeDtypeStruct((rows // seq_len, n_kv, seq_len), F32),
                      jax.ShapeDtypeStruct((rows // seq_len, n_all - n_kv, seq_len), F32)]
    return pl.pallas_call(
        _proj_kernel,
        grid=(rows // tm,),
        in_specs=[tok(d), _const_spec((1, d)), _const_spec(wkv.shape), _const_spec((1, d)), _const_spec(wq.shape),
                  _const_spec(wg.shape), _const_spec(bg.shape)],
        out_specs=out_specs,
        out_shape=out_shape,
        compiler_params=_cparams("arbitrary"),
        name="kv_q_proj",
    )(x, gkv, wkv, gq, wq, wg, bg)


CMP_GROUP_PAIR = 2 * HEAD_DIM


def _compress_kernel(*refs, n_pref, n_in):
    x_refs = refs[n_pref:n_pref + n_in]
    pe_ref, w1_ref, b1_ref, w2_ref, kc_ref, vc_ref, carry_sc, xs_sc = refs[n_pref + n_in:]
    step = pl.program_id(1)

    @pl.when(step == 0)
    def _():
        carry_sc[...] = jnp.zeros_like(carry_sc)

    set_w = N_KV_GROUPS * HEAD_DIM
    n_chunks = 2 * set_w // CMP_GROUP_PAIR
    page = x_refs[0].shape[2]
    halves = page // CMP_STRIDE
    for p, x_ref in enumerate(x_refs):
        x = x_ref[0].T
        for cc in range(n_chunks):
            xs_sc[p, cc] = x[:, cc * CMP_GROUP_PAIR:(cc + 1) * CMP_GROUP_PAIR]
    m = n_in * halves
    row = lax.broadcasted_iota(jnp.int32, (m, 2 * LANES), 0)
    for s, out_ref in enumerate((kc_ref, vc_ref)):
        for jp in range(set_w // CMP_GROUP_PAIR):
            cc = s * (set_w // CMP_GROUP_PAIR) + jp
            xj = jnp.concatenate(
                [jnp.concatenate([xs_sc[p, cc, pl.ds(tt, halves, stride=CMP_STRIDE), :] for tt in range(CMP_STRIDE)], axis=1)
                 for p in range(n_in)], axis=0)
            first = _dot((xj + pe_ref[s, 0]).astype(BF16), w1_ref[s, 0])
            second = _dot((xj + pe_ref[s, 1]).astype(BF16), w1_ref[s, 1])
            slot = s * 2 + jp
            prev = jnp.where(row == 0, carry_sc[slot, 0:1, :], pltpu.roll(first, 1, 0))
            carry_sc[slot, 0:1, :] = first[m - 1:m, :]
            hid = jax.nn.gelu(prev + second + b1_ref[s])
            out_ref[0, :, jp * CMP_GROUP_PAIR:(jp + 1) * CMP_GROUP_PAIR] = _dot(hid.astype(BF16), w2_ref[s]).astype(BF16)


CMP_PAGES_PER_STEP = 32


def _compress(src, cw, *, n_seq, seq_pages, page, page_table=None):
    n_in = CMP_PAGES_PER_STEP
    assert seq_pages % n_in == 0
    set_w = N_KV_GROUPS * HEAD_DIM
    halves = page // CMP_STRIDE
    m = n_in * halves
    steps = seq_pages // n_in
    block = (1, 2 * set_w, page)
    if page_table is None:
        x_specs = [pl.BlockSpec(block, functools.partial(lambda b, s, p: (b, 0, s * n_in + p), p=p)) for p in range(n_in)]
        n_pref = 0
    else:
        x_specs = [pl.BlockSpec(block, functools.partial(lambda b, s, pt, p: (pt[b, s * n_in + p], 0, 0), p=p))
                   for p in range(n_in)]
        n_pref = 1
    cidx = lambda nd: (lambda *_: (0,) * nd)
    const = lambda a: pl.BlockSpec(a.shape, cidx(a.ndim), pipeline_mode=pl.Buffered(1))
    out_spec = pl.BlockSpec((1, m, set_w), lambda b, s, *_: (b, s, 0))
    grid_spec = pltpu.PrefetchScalarGridSpec(
        num_scalar_prefetch=n_pref,
        grid=(n_seq, steps),
        in_specs=x_specs + [const(cw['pe']), const(cw['w1']), const(cw['b1']), const(cw['w2'])],
        out_specs=[out_spec, out_spec],
        scratch_shapes=[pltpu.VMEM((4, SUBLANES, 2 * LANES), F32),
                        pltpu.VMEM((n_in, 2 * set_w // CMP_GROUP_PAIR, page, CMP_GROUP_PAIR), F32)],
    )
    call = pl.pallas_call(
        functools.partial(_compress_kernel, n_pref=n_pref, n_in=n_in),
        grid_spec=grid_spec,
        out_shape=[jax.ShapeDtypeStruct((n_seq, seq_pages * halves, set_w), BF16)] * 2,
        compiler_params=_cparams("arbitrary", "arbitrary"),
        name="compress",
    )
    args = ([page_table] if page_table is not None else []) + [src] * n_in + [cw['pe'], cw['w1'], cw['b1'], cw['w2']]
    return call(*args)


KEY_TILE = 128
N_NEAR_TILES = REL_MAX_DIST // KEY_TILE + 1
TILE_FUTURE, TILE_FAR, TILE_WINDOW = 0, N_NEAR_TILES + 1, N_NEAR_TILES + 2


def _rel_bucket(dist):
    n = jnp.maximum(dist, 0)
    max_exact = NUM_BUCKETS // 2
    nf = jnp.maximum(n, 1).astype(F32)
    large = max_exact + (jnp.log(nf / max_exact) / math.log(REL_MAX_DIST / max_exact)
                         * (NUM_BUCKETS - max_exact)).astype(jnp.int32)
    return jnp.where(n < max_exact, n, jnp.minimum(large, NUM_BUCKETS - 1))


BIAS_LEFT = 2 * LANES


def _bias_by_distance(table, reach):
    bd = table[_rel_bucket(jnp.arange(REL_MAX_DIST))].T.astype(F32)
    heads = bd.shape[0]
    return jnp.concatenate([jnp.zeros((heads, BIAS_LEFT), F32), bd,
                            jnp.broadcast_to(bd[:, -1:], (heads, reach - REL_MAX_DIST))], axis=1)


def _toeplitz(vec, c, n_rows, n_cols):
    batch = vec.shape[0]
    period = n_rows + n_cols
    assert c - n_cols + 1 >= 0 and c + n_rows <= vec.shape[1]
    b = jnp.concatenate([vec[:, c - n_cols + 1:c + 1][:, ::-1], jnp.zeros((batch, 1), vec.dtype),
                         vec[:, c + 1:c + n_rows][:, ::-1]], axis=1)
    m = jnp.tile(b, (1, n_rows))[:, :n_rows * (period - 1)].reshape(batch, n_rows, period - 1)
    return m[:, :, :n_cols]


def _bias_toeplitz(v, c, n_rows, n_cols, col_step=1):
    heads = v.shape[0]
    c += BIAS_LEFT
    if col_step == 1:
        return _toeplitz(v, c, n_rows, n_cols)
    n_a = -(-n_rows // col_step)
    first = c - col_step * (n_cols - 1)
    assert first >= 0 and c + col_step * n_a <= v.shape[1]
    w = v[:, first:c + col_step * n_a].reshape(heads, n_cols - 1 + n_a, col_step).transpose(0, 2, 1)
    t = _toeplitz(w.reshape(heads * col_step, -1), n_cols - 1, n_a, n_cols)
    return t.reshape(heads, col_step, n_a, n_cols).transpose(0, 2, 1, 3).reshape(heads, n_a * col_step, n_cols)[:, :n_rows]


CMP_BAND_W = 2 * LANES
CMP_BAND_E0 = Q_BLOCK // CMP_STRIDE - CMP_BAND_W


def _compressed_band(table):
    i = np.arange(Q_BLOCK)[:, None]
    e = CMP_BAND_E0 + np.arange(CMP_BAND_W)[None, :]
    dist = i - CMP_STRIDE * e - (CMP_STRIDE - 1)
    assert -CMP_STRIDE * (CMP_BAND_E0 - 1) - (CMP_STRIDE - 1) >= REL_MAX_DIST
    c0 = -CMP_STRIDE * CMP_BAND_E0 - (CMP_STRIDE - 1)
    v = _bias_by_distance(table, c0 + Q_BLOCK)
    far = v[:, -1][:, None, None]
    return jnp.where(dist >= 0, _bias_toeplitz(v, c0, Q_BLOCK, CMP_BAND_W, CMP_STRIDE) - far, 0.0)


def _toeplitz_tiles(table):
    i = np.arange(Q_BLOCK)[:, None]
    j = np.arange(KEY_TILE)[None, :]
    offs = [-1] + list(range(N_NEAR_TILES)) + [N_NEAR_TILES + REL_MAX_DIST // KEY_TILE, WINDOW // KEY_TILE]
    v = _bias_by_distance(table, KEY_TILE * (max(offs) + 1))
    far = v[:, -1][:, None, None]
    tiles = []
    for idx, off in enumerate(offs):
        dist = KEY_TILE * off + i - j
        ok = (dist >= 0) & ((idx != TILE_WINDOW) | (dist < WINDOW))
        tiles.append(jnp.where(ok, _bias_toeplitz(v, KEY_TILE * off, Q_BLOCK, KEY_TILE) - far, NEG_INF))
    return jnp.stack(tiles)


SEL_CHUNK = 1024
ONES_GROUP = (1, 0)
WIN_KEYS = WINDOW + Q_BLOCK


def _masked_softmax(s, mask):
    s = jnp.where(mask, s, NEG_INF)
    e = jnp.exp(s - jnp.max(s, axis=-1, keepdims=True))
    p = e / jnp.sum(e, axis=-1, keepdims=True)
    return jnp.where(mask, p, 0.0)


def _topk_masks_t(vals, key_sc, corr_sc, n_rows):
    shape = vals[0].shape
    for i, v in enumerate(vals):
        key_sc[i] = v
    corr_sc[...] = jnp.zeros_like(corr_sc)
    jg = lax.broadcasted_iota(jnp.int32, shape, 0) // SUBLANES
    jj = lax.broadcasted_iota(jnp.int32, (SUBLANES, shape[1]), 0)

    def group_body(kg, cnts):
        base = pl.multiple_of(kg * SUBLANES, SUBLANES)
        later = jnp.where(jg > kg, 1, 0)
        out = []
        for i, cnt in enumerate(cnts):
            key = key_sc[i]
            diag = key_sc[i, pl.ds(base, SUBLANES), :]
            corr = jnp.zeros(diag.shape, jnp.int32)
            for kk in range(SUBLANES):
                k_row = key_sc[i, pl.ds(base + kk, 1), :]
                k_all = jnp.broadcast_to(k_row, shape)
                cnt = cnt + jnp.where(k_all > key, 1, jnp.where(k_all == key, later, 0))
                corr = corr + jnp.where(jnp.broadcast_to(k_row, diag.shape) == diag, jnp.where(jj > kk, 1, 0), 0)
            corr_sc[i, pl.ds(base, SUBLANES), :] = corr
            out.append(cnt)
        return tuple(out)

    cnts = lax.fori_loop(0, (n_rows + SUBLANES - 1) // SUBLANES, group_body,
                         tuple(jnp.zeros(shape, jnp.int32) for _ in vals))
    return [jnp.where(cnt + corr_sc[i] < N_SELECT, 1.0, 0.0) for i, cnt in enumerate(cnts)]


def _nsa_prompt_kernel(q_ref, gate_ref, kct_ref, vc_ref, ksa_ref, vsa_ref, kwt_ref, vw_ref, tb_ref, cb_ref,
                       selmap_ref, eg_ref, o_ref, key_sc, corr_sc, unsel_sc, ocmp_sc, osel_sc, m_sc,
                       acc_sc, s0_sc, s1_sc):
    qb = pl.program_id(1)
    t0 = qb * Q_BLOCK
    gw = N_KV_GROUPS * HEAD_DIM
    n_cmp = kct_ref.shape[2]
    rows = GROUP_SIZE * Q_BLOCK

    qi = lax.broadcasted_iota(jnp.int32, (Q_BLOCK, n_cmp), 0)
    hc = lax.broadcasted_iota(jnp.int32, (Q_BLOCK, n_cmp), 1)
    mask_c = jnp.where(hc >= 1, t0 + qi - (CMP_STRIDE * hc + CMP_STRIDE - 1), -1) >= 0
    first_row = qb * (Q_BLOCK // CMP_STRIDE)
    band_tile = (first_row + Q_BLOCK // CMP_STRIDE - 1) // LANES
    band_shift = (first_row % LANES + LANES + CMP_BAND_E0 + CMP_BAND_W) % CMP_BAND_W

    def compressed_bias(head):
        band = pltpu.roll(cb_ref[head], band_shift, 1)
        lo, hi = band[:, :LANES], band[:, LANES:]
        return jnp.concatenate([jnp.where(band_tile == c, hi, jnp.where(band_tile - 1 == c, lo, 0.0))
                                for c in range(n_cmp // LANES)], axis=1)

    lane_g = lax.broadcasted_iota(jnp.int32, (Q_BLOCK, gw), 1) // HEAD_DIM

    def group_queries(g):
        return jnp.concatenate(
            [jnp.where(lane_g == g, q_ref[0, :, r * gw:(r + 1) * gw], 0.0).astype(BF16) for r in range(GROUP_SIZE)], axis=0)

    nb = selmap_ref.shape[0]
    jb = lax.broadcasted_iota(jnp.int32, (nb, Q_BLOCK), 0)
    qpos_t = t0 + lax.broadcasted_iota(jnp.int32, (nb, Q_BLOCK), 1)
    cur = qpos_t // SEL_BLOCK
    forced = jnp.where(jb == 0, 1, 0) + jnp.where(jb == cur, 1, 0) + jnp.where(jb == cur - 1, 1, 0)
    valid_t = jb * SEL_BLOCK <= qpos_t
    n_valid_blocks = (t0 + Q_BLOCK - 1) // SEL_BLOCK + 1
    n_chunks = (t0 + Q_BLOCK - 1) // SEL_CHUNK + 1

    importance = []
    for g in range(N_KV_GROUPS):
        s_c = _dot(group_queries(g), kct_ref[0])
        p_rows = []
        for r in range(GROUP_SIZE):
            bias = compressed_bias(g * GROUP_SIZE + r)
            p_rows.append(_masked_softmax(s_c[r * Q_BLOCK:(r + 1) * Q_BLOCK] + bias, mask_c).astype(BF16))
        ocmp_sc[g] = _dot(jnp.concatenate(p_rows, axis=0), vc_ref[0])
        imp = jnp.zeros((nb, Q_BLOCK), F32)
        for r in range(GROUP_SIZE):
            imp = imp + lax.dot_general(selmap_ref[...], p_rows[r], (((1,), (1,)), ((), ())), preferred_element_type=F32)
        importance.append(jnp.where(forced > 0, 1e9, jnp.where(valid_t, imp, -1e9)))

    for g, mask_t in enumerate(_topk_masks_t(importance, key_sc, corr_sc, n_valid_blocks)):
        unsel_sc[g] = (1.0 - mask_t.T).astype(BF16)

    def group_body(g, carry):
        qm = group_queries(g)
        unsel = unsel_sc[g]

        m_sc[...] = jnp.full_like(m_sc, NEG_INF)
        acc_sc[...] = jnp.zeros_like(acc_sc)
        ones_at = jnp.where(g == ONES_GROUP[0], 1, 0)
        v_col = pl.multiple_of(ones_at * gw, gw)
        tiles_per_chunk = SEL_CHUNK // KEY_TILE

        def scores(c):
            return _dot_split_rows(qm, ksa_ref[0, :gw, pl.ds(pl.multiple_of(c * SEL_CHUNK, SEL_CHUNK), SEL_CHUNK)])

        def sweep_step(c, s_cur, s_next, near):
            off = pl.multiple_of(c * SEL_CHUNK, SEL_CHUNK)
            masked = _dot(unsel, ksa_ref[0, gw:, pl.ds(off, SEL_CHUNK)])
            s_next[...] = scores(jnp.minimum(c + 1, n_chunks - 1))
            parts = []
            for r in range(GROUP_SIZE):
                extra = masked
                if near:
                    extra = masked + jnp.concatenate(
                        [tb_ref[jnp.clip(qb - (c * tiles_per_chunk + h), -1, N_NEAR_TILES) + 1, g * GROUP_SIZE + r]
                         for h in range(tiles_per_chunk)], axis=1)
                parts.append(s_cur[r * Q_BLOCK:(r + 1) * Q_BLOCK, :] + extra)
            s = jnp.concatenate(parts, axis=0)
            m_old = m_sc[...]
            m_new = jnp.maximum(m_old, jnp.max(s, axis=-1, keepdims=True))
            alpha = jnp.exp(m_old - m_new)
            p = jnp.exp(s - jnp.concatenate([m_new] * (SEL_CHUNK // LANES), axis=1))
            pv = _dot_split_rows(p.astype(BF16), vsa_ref[0, pl.ds(off, SEL_CHUNK), pl.ds(v_col, gw)])
            acc_sc[...] = jnp.concatenate([alpha] * (gw // LANES), axis=1) * acc_sc[...] + pv
            m_sc[...] = m_new

        s0_sc[...] = scores(0)

        n_far = jnp.maximum((qb - N_NEAR_TILES + 1) // tiles_per_chunk, 0)

        def step(c, s_cur, s_next):
            @pl.when(c < n_far)
            def _():
                sweep_step(c, s_cur, s_next, False)

            @pl.when((c >= n_far) & (c < n_chunks))
            def _():
                sweep_step(c, s_cur, s_next, True)

        def pair_body(pair, carry2):
            step(2 * pair, s0_sc, s1_sc)
            step(2 * pair + 1, s1_sc, s0_sc)
            return carry2

        lax.fori_loop(0, (n_chunks + 1) // 2, pair_body, 0)
        acc = acc_sc[...]
        denom = jnp.where(ones_at == 1, acc[:, ONES_GROUP[1] * HEAD_DIM:ONES_GROUP[1] * HEAD_DIM + 1],
                          acc[:, ONES_GROUP[0] * HEAD_DIM:ONES_GROUP[0] * HEAD_DIM + 1])
        osel_sc[g] = acc / denom
        return carry

    lax.fori_loop(0, N_KV_GROUPS, group_body, 0)

    n_wt = WIN_KEYS // KEY_TILE
    w_off = [pl.multiple_of(jnp.maximum(qb - (n_wt - 1) + w, 0) * KEY_TILE, KEY_TILE) for w in range(n_wt)]
    w_tile = [jnp.where(qb - (n_wt - 1) + w >= 0, TILE_WINDOW if w == 0 else n_wt - w, TILE_FUTURE) for w in range(n_wt)]
    k_win = jnp.concatenate([kwt_ref[0, :, pl.ds(o, KEY_TILE)] for o in w_off], axis=1)
    v_win = jnp.concatenate([vw_ref[0, pl.ds(o, KEY_TILE), :] for o in w_off], axis=0)
    gexp = jnp.dot(gate_ref[0], eg_ref[...], preferred_element_type=F32, precision=lax.Precision.HIGHEST)
    out = [jnp.zeros((Q_BLOCK, gw), F32) for _ in range(GROUP_SIZE)]
    for g in range(N_KV_GROUPS):
        s_w = _dot(group_queries(g), k_win)
        pw_rows = []
        for r in range(GROUP_SIZE):
            bias = jnp.concatenate([tb_ref[ti, g * GROUP_SIZE + r] for ti in w_tile], axis=1)
            s_r = s_w[r * Q_BLOCK:(r + 1) * Q_BLOCK] + bias
            e = jnp.exp(s_r - jnp.max(s_r, axis=-1, keepdims=True))
            pw_rows.append((e / jnp.sum(e, axis=-1, keepdims=True)).astype(BF16))
        o_win = _dot(jnp.concatenate(pw_rows, axis=0), v_win)
        for r in range(GROUP_SIZE):
            rs = slice(r * Q_BLOCK, (r + 1) * Q_BLOCK)
            mix = (gexp[:, (3 * r) * gw:(3 * r + 1) * gw] * ocmp_sc[g, rs, :]
                   + gexp[:, (3 * r + 1) * gw:(3 * r + 2) * gw] * osel_sc[g, rs, :]
                   + gexp[:, (3 * r + 2) * gw:(3 * r + 3) * gw] * o_win[rs])
            out[r] = out[r] + jnp.where(lane_g == g, mix, 0.0)
    for r in range(GROUP_SIZE):
        o_ref[0, :, r * gw:(r + 1) * gw] = out[r].astype(BF16)


def _nsa_prompt(q, gates, kct, vc, ksa, vsa, kwt, selwin, tiles, band, selmap, eg):
    n, t, hw = q.shape
    gw = N_KV_GROUPS * HEAD_DIM
    n_cmp = kct.shape[2]
    rows = GROUP_SIZE * Q_BLOCK
    per_n = lambda shape, col=0: pl.BlockSpec(shape, lambda b, j: (b, 0, col), pipeline_mode=pl.Buffered(1))
    blk = lambda w: pl.BlockSpec((1, Q_BLOCK, w), lambda b, j: (b, j, 0))
    return pl.pallas_call(
        _nsa_prompt_kernel,
        grid=(n, t // Q_BLOCK),
        in_specs=[blk(hw), blk(gates.shape[2]),
                  per_n((1, gw, n_cmp)), per_n((1, n_cmp, gw)), per_n((1, ksa.shape[1], t)), per_n((1, t, 2 * gw)),
                  per_n((1, gw, t)), per_n((1, t, gw), 3),
                  _const_spec(tiles.shape), _const_spec(band.shape), _const_spec(selmap.shape), _const_spec(eg.shape)],
        out_specs=blk(hw),
        out_shape=jax.ShapeDtypeStruct((n, t, hw), BF16),
        scratch_shapes=[pltpu.VMEM((N_KV_GROUPS, selmap.shape[0], Q_BLOCK), F32),
                        pltpu.VMEM((N_KV_GROUPS, selmap.shape[0], Q_BLOCK), jnp.int32),
                        pltpu.VMEM((N_KV_GROUPS, Q_BLOCK, selmap.shape[0]), BF16),
                        pltpu.VMEM((N_KV_GROUPS, rows, gw), F32), pltpu.VMEM((N_KV_GROUPS, rows, gw), F32),
                        pltpu.VMEM((rows, LANES), F32), pltpu.VMEM((rows, gw), F32),
                        pltpu.VMEM((rows, SEL_CHUNK), F32), pltpu.VMEM((rows, SEL_CHUNK), F32)],
        compiler_params=_cparams("arbitrary", "arbitrary"),
        name="nsa_prompt",
    )(q, gates, kct, vc, ksa, vsa, kwt, selwin, tiles, band, selmap, eg)


DEC_T_PAD = SUBLANES
SAMPLE_BLOCKS_PAD = 256
NEW_KEYS_PAD = LANES


def _nsa_sample_kernel(*refs, n_pages, n_blocks):
    pt_ref, q_ref, gate_ref, kct_ref, vc_ref = refs[:5]
    page_refs = refs[5:5 + n_pages]
    (bsel_ref, bcmp_ref, bwin_ref, bnew_ref, new_ref, cwin_ref, selmap_ref, eg_ref, o_ref,
     key_sc, corr_sc, sel_sc, m_sc, l_sc, acc_sc, oc_sc) = refs[5 + n_pages:]
    c = pl.program_id(1)
    gw = N_KV_GROUPS * HEAD_DIM
    rows = N_HEADS * DEC_T_PAD
    page = page_refs[0].shape[2]
    lane_g = lax.broadcasted_iota(jnp.int32, (DEC_T_PAD, gw), 1) // HEAD_DIM
    qm = jnp.concatenate(
        [jnp.where(lane_g == g, q_ref[0, :, r * gw:(r + 1) * gw], 0.0)
         for r in range(GROUP_SIZE) for g in range(N_KV_GROUPS)], axis=0).astype(BF16)
    nt = (((1,), (1,)), ((), ()))

    @pl.when(c == 0)
    def _():
        p_c = _masked_softmax(_dot(qm, kct_ref[0]) + bcmp_ref[...], bcmp_ref[...] > 0.5 * NEG_INF).astype(BF16)
        oc_sc[...] = _dot(p_c, vc_ref[0])
        imp_all = lax.dot_general(selmap_ref[...], p_c, nt, preferred_element_type=F32)
        per_r = rows // GROUP_SIZE
        imp = imp_all
        for r in range(1, GROUP_SIZE):
            imp = imp + pltpu.roll(imp_all, r * per_r, 1)
        jb = lax.broadcasted_iota(jnp.int32, imp.shape, 0)
        last = n_blocks - 1
        forced = jnp.where(jb == 0, 1, 0) + jnp.where(jb == last, 1, 0) + jnp.where(jb == last - 1, 1, 0)
        imp = jnp.where(forced > 0, 1e9, jnp.where(jb <= last, imp, -1e9))
        live = key_sc.shape[1]
        picked_t = _topk_masks_t([imp[:live]], key_sc, corr_sc, n_blocks)[0]
        sel_sc[...] = jnp.concatenate([picked_t, jnp.zeros((imp.shape[0] - live, rows), F32)], axis=0).T.astype(BF16)
        m_sc[...] = jnp.full_like(m_sc, NEG_INF)
        l_sc[...] = jnp.zeros_like(l_sc)
        acc_sc[...] = jnp.zeros_like(acc_sc)

    def online_update(s, pv_of):
        m_old = m_sc[...]
        m_new = jnp.maximum(m_old, jnp.max(s, axis=-1, keepdims=True))
        alpha = jnp.exp(m_old - m_new)
        p = jnp.exp(s - m_new)
        l_sc[...] = alpha * l_sc[...] + jnp.sum(p, axis=-1, keepdims=True)
        acc_sc[...] = alpha * acc_sc[...] + pv_of(p.astype(BF16))
        m_sc[...] = m_new

    keys = n_pages * page
    s = jnp.concatenate([_dot(qm, pr[0, :gw, :].astype(BF16)) for pr in page_refs], axis=1)
    eb = lax.broadcasted_iota(jnp.int32, (SAMPLE_BLOCKS_PAD, keys), 0)
    ek = lax.broadcasted_iota(jnp.int32, (SAMPLE_BLOCKS_PAD, keys), 1)
    expand = jnp.where(eb == c * (keys // SEL_BLOCK) + ek // SEL_BLOCK, 1.0, 0.0).astype(BF16)
    picked = _dot(sel_sc[...], expand) > 0.5
    v_t = jnp.concatenate([pr[0, gw:, :].astype(BF16) for pr in page_refs], axis=1)
    online_update(jnp.where(picked, s + bsel_ref[...], NEG_INF),
                  lambda p: lax.dot_general(p, v_t, nt, preferred_element_type=F32))

    @pl.when(c == pl.num_programs(1) - 1)
    def _():
        s_new = lax.dot_general(qm, new_ref[0, :, :gw], nt, preferred_element_type=F32) + bnew_ref[0]
        online_update(s_new, lambda p: _dot(p, new_ref[0, :, gw:2 * gw]))
        o_sel = acc_sc[...] / l_sc[...]

        s_w = _dot(qm, cwin_ref[0, :gw, :].astype(BF16)) + bwin_ref[...]
        s_wn = lax.dot_general(qm, new_ref[0, :, 2 * gw:3 * gw], nt, preferred_element_type=F32) + bnew_ref[1]
        m_w = jnp.maximum(jnp.max(s_w, axis=-1, keepdims=True), jnp.max(s_wn, axis=-1, keepdims=True))
        e_w = jnp.exp(s_w - m_w)
        e_wn = jnp.exp(s_wn - m_w)
        l_w = jnp.sum(e_w, axis=-1, keepdims=True) + jnp.sum(e_wn, axis=-1, keepdims=True)
        o_win = lax.dot_general((e_w / l_w).astype(BF16), cwin_ref[0, gw:, :].astype(BF16), nt,
                                preferred_element_type=F32) + _dot((e_wn / l_w).astype(BF16), new_ref[0, :, 3 * gw:])

        gexp = jnp.dot(gate_ref[0], eg_ref[...], preferred_element_type=F32, precision=lax.Precision.HIGHEST)
        o_cmp = oc_sc[...]
        for r in range(GROUP_SIZE):
            out_r = jnp.zeros((DEC_T_PAD, gw), F32)
            for g in range(N_KV_GROUPS):
                rs = slice((r * N_KV_GROUPS + g) * DEC_T_PAD, (r * N_KV_GROUPS + g + 1) * DEC_T_PAD)
                mix = (gexp[:, (3 * r) * gw:(3 * r + 1) * gw] * o_cmp[rs]
                       + gexp[:, (3 * r + 1) * gw:(3 * r + 2) * gw] * o_sel[rs]
                       + gexp[:, (3 * r + 2) * gw:(3 * r + 3) * gw] * o_win[rs])
                out_r = out_r + jnp.where(lane_g == g, mix, 0.0)
            o_ref[0, :, r * gw:(r + 1) * gw] = out_r


PAGES_PER_STEP = 16


def _nsa_sample(page_table, q, gates, kct, vc, cache, bias, new_rows, cache_win, selmap, eg, n_blocks):
    n, _, hw = q.shape
    gw = N_KV_GROUPS * HEAD_DIM
    n_cmp = kct.shape[2]
    rows = N_HEADS * DEC_T_PAD
    page = cache.shape[2]
    steps = page_table.shape[1] // PAGES_PER_STEP
    keys = PAGES_PER_STEP * page
    per_n = lambda a: pl.BlockSpec((1,) + a.shape[1:], lambda b, c, pt: (b, 0, 0))
    cidx = lambda nd: (lambda *_: (0,) * nd)
    const = lambda a: pl.BlockSpec(a.shape, cidx(a.ndim), pipeline_mode=pl.Buffered(1))
    page_specs = [pl.BlockSpec((1, 2 * gw, page), functools.partial(
        lambda b, c, pt, p: (pt[b, c * PAGES_PER_STEP + p], 1, 0), p=p)) for p in range(PAGES_PER_STEP)]
    grid_spec = pltpu.PrefetchScalarGridSpec(
        num_scalar_prefetch=1,
        grid=(n, steps),
        in_specs=[per_n(q), per_n(gates), per_n(kct), per_n(vc)] + page_specs + [
            pl.BlockSpec((rows, keys), lambda b, c, pt: (0, c)), const(bias['cmp']), const(bias['win']), const(bias['new']),
            per_n(new_rows), per_n(cache_win), const(selmap), const(eg)],
        out_specs=pl.BlockSpec((1, DEC_T_PAD, hw), lambda b, c, pt: (b, 0, 0)),
        scratch_shapes=[pltpu.VMEM((1, -(-n_blocks // SUBLANES) * SUBLANES, rows), F32),
                        pltpu.VMEM((1, -(-n_blocks // SUBLANES) * SUBLANES, rows), jnp.int32),
                        pltpu.VMEM((rows, SAMPLE_BLOCKS_PAD), BF16),
                        pltpu.VMEM((rows, 1), F32), pltpu.VMEM((rows, 1), F32), pltpu.VMEM((rows, gw), F32),
                        pltpu.VMEM((rows, gw), F32)],
    )
    return pl.pallas_call(
        functools.partial(_nsa_sample_kernel, n_pages=PAGES_PER_STEP, n_blocks=n_blocks),
        grid_spec=grid_spec,
        out_shape=jax.ShapeDtypeStruct((n, DEC_T_PAD, hw), F32),
        compiler_params=_cparams("arbitrary", "arbitrary"),
        name="nsa_sample",
    )(page_table, q, gates, kct, vc, *([cache] * PAGES_PER_STEP), bias['sel'], bias['cmp'], bias['win'], bias['new'],
      new_rows, cache_win, selmap, eg)


def _sample_bias(table, past, dec_t, win_len, n_cmp_rows):
    t = np.arange(DEC_T_PAD)[:, None]
    v = _bias_by_distance(table, max(past + DEC_T_PAD, REL_MAX_DIST + 1))

    def rows(c, n_cols, col_step, ok):
        b = jnp.where(ok, _bias_toeplitz(v, c, DEC_T_PAD, n_cols, col_step), NEG_INF)
        b = b.reshape(N_KV_GROUPS, GROUP_SIZE, DEC_T_PAD, -1).transpose(1, 0, 2, 3)
        return b.reshape(N_HEADS * DEC_T_PAD, -1)

    h = np.arange(n_cmp_rows)[None, :]
    dist_c = past + t - (CMP_STRIDE * h + CMP_STRIDE - 1)
    j = np.arange(win_len)[None, :]
    dist_w = win_len + t - j
    tn = np.arange(NEW_KEYS_PAD)[None, :]
    new = rows(0, NEW_KEYS_PAD, 1, (tn <= t) & (tn < dec_t))
    return dict(sel=rows(past, past, 1, np.ones((DEC_T_PAD, past), bool)),
                cmp=rows(past - (CMP_STRIDE - 1), n_cmp_rows, CMP_STRIDE, (h >= 1) & (dist_c >= 0)),
                win=rows(win_len, win_len, 1, (dist_w < WINDOW) & (past - win_len + j >= 0)), new=jnp.stack([new, new]))


GATE_LANES = LANES


def _prepare(p):
    d = p['norm_pre_mix'].shape[1]
    hw = N_HEADS * HEAD_DIM
    gw = N_KV_GROUPS * HEAD_DIM
    row = lambda v: v.reshape(1, -1)
    w = {}
    w['rg'] = dict(
        g_pre=row(p['norm_pre_mix'][0]), w_in=p['rg_w_in'][0].astype(BF16), conv_w=p['rg_conv_w'][0],
        conv_b=row(p['rg_conv_b'][0]),
        w_ax=jnp.concatenate([p['rg_w_a'][0], p['rg_w_x'][0]], axis=2).astype(BF16),
        b_ax=jnp.stack([p['rg_b_a'][0], p['rg_b_x'][0]]), lam=row(p['rg_lambda'][0]),
        w_out=p['rg_w_out'][0].astype(BF16), g_post=row(p['norm_post_mix'][0]))
    w['mlp'] = [dict(g1=row(p['norm_pre_mlp'][l]), wup=p['w_mlp_up'][l].astype(BF16),
                     wdn=p['w_mlp_down'][l].astype(BF16), g2=row(p['norm_post_mlp'][l])) for l in range(2)]
    wqg = p['nsa_w_qg'][0]
    n_gate = wqg.shape[1] - hw
    w['proj'] = dict(
        gkv=row(p['kv_norm']), wkv=p['w_kv'].astype(BF16), gq=row(p['norm_pre_mix'][1]),
        wq=wqg[:, :hw].reshape(d, N_KV_GROUPS, GROUP_SIZE, HEAD_DIM).transpose(0, 2, 1, 3).reshape(d, hw).astype(BF16),
        wg=jnp.pad(wqg[:, hw:], ((0, 0), (0, GATE_LANES - n_gate))).astype(BF16),
        bg=jnp.pad(p['nsa_b_g'][0], (0, GATE_LANES - n_gate)).reshape(1, -1))
    w['wo'] = p['nsa_w_o'][0].reshape(N_KV_GROUPS, GROUP_SIZE, HEAD_DIM, d).transpose(1, 0, 2, 3).reshape(hw, d).astype(BF16)
    w['g_post_mix1'] = row(p['norm_post_mix'][1])
    eye2 = jnp.eye(2, dtype=F32)
    half = L_CMP // CMP_STRIDE
    w['cmp'] = dict(
        pe=jnp.broadcast_to(p['cmp_pe'].reshape(2, half, CMP_STRIDE, 1, HEAD_DIM),
                            (2, half, CMP_STRIDE, 2, HEAD_DIM)).reshape(2, half, 1, CMP_STRIDE * CMP_GROUP_PAIR),
        w1=jnp.einsum('sftdc,gh->sftgdhc', p['cmp_w1'].reshape(2, half, CMP_STRIDE, HEAD_DIM, -1), eye2)
        .reshape(2, half, CMP_STRIDE * CMP_GROUP_PAIR, -1).astype(BF16),
        b1=jnp.tile(p['cmp_b1'], (1, 2)).reshape(2, 1, -1),
        w2=jnp.einsum('scd,gh->sgchd', p['cmp_w2'], eye2).reshape(2, -1, CMP_GROUP_PAIR).astype(BF16))
    w['table'] = p['rel_bias_table']
    w['tiles'] = _toeplitz_tiles(p['rel_bias_table'])
    w['band'] = _compressed_band(p['rel_bias_table'])
    h_i = jnp.arange(N_HEADS)
    src = (h_i[:, None] * 3 + jnp.arange(3)[None, :]).reshape(-1)
    dst0 = (((h_i % GROUP_SIZE)[:, None] * 3 + jnp.arange(3)[None, :]) * gw
            + (h_i // GROUP_SIZE)[:, None] * HEAD_DIM).reshape(-1)
    lanes = jnp.arange(3 * GROUP_SIZE * gw)
    w['eg'] = jnp.zeros((GATE_LANES, lanes.shape[0]), F32).at[src].set(
        ((lanes[None, :] >= dst0[:, None]) & (lanes[None, :] < dst0[:, None] + HEAD_DIM)).astype(F32))
    return w


def _selmap_t(n_blocks, n_cmp_rows):
    ratio = SEL_BLOCK // CMP_STRIDE
    h = jnp.arange(n_cmp_rows)[None, :]
    j = jnp.arange(n_blocks)[:, None]
    m = ((h - 1) // ratio == j).astype(F32) + (h // ratio == j).astype(F32)
    return jnp.where(h >= 1, m, 0.0).astype(BF16)


SCAN_ROW_TILE = 256
TOKEN_ROW_TILE = 512


def _trunk_prompt(x, w):
    n, t, d = x.shape
    zeros = lambda r: jnp.zeros((n, r, d), F32)
    x1, h_last, c_last = _rglru_layer(x, zeros(1), zeros(CONV_W - 1), w['rg'], stride=1, tm=SCAN_ROW_TILE)
    m0 = w['mlp'][0]
    x2 = _mlp_layer(x1.reshape(n * t, d), m0['g1'], m0['wup'], m0['wdn'], m0['g2'], tm=TOKEN_ROW_TILE)
    pj = w['proj']
    kv4, _, selwin, q, gates, kst, kwt, vsa, kv_t, win_t = _proj_layer(
        x2, pj['gkv'], pj['wkv'], pj['gq'], pj['wq'], pj['wg'], pj['bg'], tm=TOKEN_ROW_TILE, seq_len=t)
    nh = t // CMP_STRIDE
    n_blocks = t // SEL_BLOCK
    kc, vc = _compress(kv_t, w['cmp'], n_seq=n, seq_pages=t // KEY_TILE, page=KEY_TILE)
    selneg = jnp.where(jnp.arange(t)[None, :] // SEL_BLOCK == jnp.arange(n_blocks)[:, None], NEG_INF, 0.0).astype(BF16)
    ksa = jnp.concatenate([kst, jnp.broadcast_to(selneg, (n,) + selneg.shape)], axis=1)
    attn = _nsa_prompt(q.reshape(n, t, -1), gates.reshape(n, t, -1), kc.transpose(0, 2, 1), vc, ksa, vsa.reshape(n, t, -1),
                       kwt, selwin.reshape(n, t, -1), w['tiles'], w['band'], _selmap_t(n_blocks, nh), w['eg'])
    m1 = w['mlp'][1]
    y = _mlp_layer(x2, m1['g1'], m1['wup'], m1['wdn'], m1['g2'], tm=TOKEN_ROW_TILE,
                   mix=(attn.reshape(n * t, -1), w['wo'], w['g_post_mix1']))
    wlen = min(WINDOW, t)
    rows_of = lambda a, sets: a.reshape(n, sets, N_KV_GROUPS, HEAD_DIM, -1).transpose(0, 4, 1, 2, 3)
    return (y.reshape(n, t, d), h_last.reshape(1, n, d), c_last.reshape(1, n, CONV_W - 1, d),
            rows_of(kv_t, 4), rows_of(win_t[:, :, t - wlen:], 2))


def _trunk_sample(x, h0, c0, cache_kv, cache_win, page_table, w):
    n, t, d = x.shape
    n_phys, page = cache_kv.shape[:2]
    past = page_table.shape[1] * page
    win_len = cache_win.shape[1]
    assert t < CMP_STRIDE and past % CMP_STRIDE == 0 and t <= DEC_T_PAD and page % CMP_STRIDE == 0
    assert page_table.shape[1] % PAGES_PER_STEP == 0 and page == KEY_TILE
    to_tn = lambda a: a.transpose(1, 0, 2).reshape(a.shape[0] * a.shape[1], -1)
    to_nt = lambda a, k: a.reshape(k, n, -1).transpose(1, 0, 2)
    tail = CONV_W - 1
    x1, h_last, c_last = _rglru_layer(to_tn(x)[None], h0[None], to_tn(c0)[None], w['rg'], stride=n, tm=t * n)
    m0 = w['mlp'][0]
    x2 = _mlp_layer(x1[0], m0['g1'], m0['wup'], m0['wdn'], m0['g2'], tm=t * n)
    pj = w['proj']
    kv4, win, selwin, q, gates = _proj_layer(x2, pj['gkv'], pj['wkv'], pj['gq'], pj['wq'], pj['wg'], pj['bg'], tm=t * n)

    row_w = cache_kv.shape[2] * cache_kv.shape[3] * cache_kv.shape[4]
    nh = past // CMP_STRIDE
    cache_t = cache_kv.reshape(n_phys, page, row_w).transpose(0, 2, 1)
    cache_win_t = cache_win.reshape(n, win_len, -1).transpose(0, 2, 1)
    kc, vc = _compress(cache_t, w['cmp'], n_seq=n, seq_pages=page_table.shape[1], page=page, page_table=page_table)
    pad_t = lambda a, k: jnp.pad(a, ((0, 0), (0, k - a.shape[1]), (0, 0)))
    n_blocks = -(-(past + t) // SEL_BLOCK)
    attn = _nsa_sample(
        page_table, pad_t(to_nt(q, t).astype(F32), DEC_T_PAD), pad_t(to_nt(gates, t), DEC_T_PAD), kc.transpose(0, 2, 1), vc,
        cache_t, _sample_bias(w['table'], past, t, win_len, nh),
        pad_t(to_nt(selwin, t), NEW_KEYS_PAD), cache_win_t, _selmap_t(SAMPLE_BLOCKS_PAD, nh), w['eg'], n_blocks)
    m1 = w['mlp'][1]
    y = _mlp_layer(x2, m1['g1'], m1['wup'], m1['wdn'], m1['g2'], tm=t * n,
                   mix=(to_tn(attn[:, :t]).astype(BF16), w['wo'], w['g_post_mix1']))
    win_rows = to_nt(win, t).reshape(n, t, 2, N_KV_GROUPS, HEAD_DIM)
    win_new = jnp.concatenate([cache_win, win_rows], axis=1)[:, t:]
    return (to_nt(y, t), h_last, to_nt(c_last[0], tail)[None],
            to_nt(kv4, t).reshape(n, t, 4, N_KV_GROUPS, HEAD_DIM), win_new)


def kernel(x_prompt, x_sample, state_rglru_h, state_rglru_conv, cache_kv, cache_win, page_table, norm_pre_mix,
           norm_post_mix, norm_pre_mlp, norm_post_mlp, w_mlp_up, w_mlp_down, rg_w_in, rg_conv_w, rg_conv_b, rg_w_a, rg_b_a,
           rg_w_x, rg_b_x, rg_lambda, rg_w_out, kv_norm, w_kv, cmp_pe, cmp_w1, cmp_b1, cmp_w2, nsa_w_qg, nsa_b_g, nsa_w_o,
           rel_bias_table):
    w = _prepare(dict(
        norm_pre_mix=norm_pre_mix, norm_post_mix=norm_post_mix, norm_pre_mlp=norm_pre_mlp, norm_post_mlp=norm_post_mlp,
        w_mlp_up=w_mlp_up, w_mlp_down=w_mlp_down, rg_w_in=rg_w_in, rg_conv_w=rg_conv_w, rg_conv_b=rg_conv_b,
        rg_w_a=rg_w_a, rg_b_a=rg_b_a, rg_w_x=rg_w_x, rg_b_x=rg_b_x, rg_lambda=rg_lambda, rg_w_out=rg_w_out,
        kv_norm=kv_norm, w_kv=w_kv, cmp_pe=cmp_pe, cmp_w1=cmp_w1, cmp_b1=cmp_b1, cmp_w2=cmp_w2, nsa_w_qg=nsa_w_qg,
        nsa_b_g=nsa_b_g, nsa_w_o=nsa_w_o, rel_bias_table=rel_bias_table))
    y_p, p_h, p_conv, p_kv, p_win = _trunk_prompt(x_prompt, w)
    y_s, s_h, s_conv, s_kv, s_win = _trunk_sample(x_sample, state_rglru_h[0], state_rglru_conv[0], cache_kv, cache_win,
                                                  page_table, w)
    return (y_p, y_s, p_h, p_conv, p_kv, p_win, s_h, s_conv, s_kv, s_win)
```
